```python
import math
import jax, jax.numpy as jnp
from jax import lax
import numpy as np

D_MODEL = 1024
BATCH = 1
SEQ = 16384
DEPTH = 2

N_A = DEPTH // 2
N_B = DEPTH - N_A

HG_HEADS = 8
HG_EXPAND = 128
HG_FDIM = HG_HEADS * HG_EXPAND
HG_HEAD_V = D_MODEL // HG_HEADS
HG_CHUNK = 64

DA_HEADS = 8
DA_HEAD_DIM = D_MODEL // (2 * DA_HEADS)
DA_V_DIM = 2 * DA_HEAD_DIM
DA_QDIM = DA_HEADS * 2 * DA_HEAD_DIM
DA_VDIM_TOTAL = DA_HEADS * DA_V_DIM
Q_BLOCK = 128

REL_BUCKETS = 32
REL_MAX_DIST = 128

D_FF = 2816
CONV_W = 3

EPS = 1e-6

kernel_name = 'yoco_hgrn2_diffattn_convffn'


def rmsnorm(x, g):
    x32 = x.astype(jnp.float32)
    y = x32 * lax.rsqrt(jnp.mean(x32 * x32, axis=-1, keepdims=True) + EPS)
    return (y * g.astype(jnp.float32)).astype(x.dtype)


def hgrn2_mixer(h, w_in, w_out, g_norm, lb):
    B, T, _ = h.shape
    proj = h @ w_in
    q, f, i, g = jnp.split(proj, [HG_FDIM, 2 * HG_FDIM, 3 * HG_FDIM], axis=-1)
    q = jax.nn.silu(q)
    log_f = jnp.logaddexp(jnp.log(lb), jnp.log1p(-lb) + jax.nn.log_sigmoid(f.astype(jnp.float32)))
    k = -jnp.expm1(log_f)
    nc = T // HG_CHUNK

    def chunks(a, hd):
        return a.reshape(B, nc, HG_CHUNK, HG_HEADS, hd).swapaxes(0, 1)

    xs = (chunks(q, HG_EXPAND), chunks(k, HG_EXPAND), chunks(i, HG_HEAD_V), chunks(log_f, HG_EXPAND))
    causal = jnp.tril(jnp.ones((HG_CHUNK, HG_CHUNK), dtype=bool))[None, :, :, None, None]

    def step(S, blk):
        qb, kb, vb, lfb = blk
        b = jnp.cumsum(lfb, axis=1)
        b_last = b[:, -1]
        o_inter = jnp.einsum('bthk,bhkv->bthv', qb * jnp.exp(b), S)
        rel = jnp.where(causal, b[:, :, None] - b[:, None, :], -jnp.inf)
        decay = jnp.exp(rel)
        scores = jnp.einsum('bthk,bshk,btshk->bhts', qb, kb, decay)
        o_intra = jnp.einsum('bhts,bshv->bthv', scores, vb)
        k_dec = kb * jnp.exp(b_last[:, None] - b)
        S_new = jnp.exp(b_last)[..., None] * S + jnp.einsum('bshk,bshv->bhkv', k_dec, vb)
        return S_new, o_inter + o_intra

    S0 = jnp.zeros((B, HG_HEADS, HG_EXPAND, HG_HEAD_V), jnp.float32)
    _, o = lax.scan(step, S0, xs)
    o = o.swapaxes(0, 1).reshape(B, T, HG_HEADS * HG_HEAD_V)
    o = rmsnorm(o, g_norm) * jax.nn.silu(g.astype(jnp.float32))
    return (o @ w_out).astype(h.dtype)


def t5_bucket(rel):
    max_exact = REL_BUCKETS // 2
    n = jnp.maximum(rel, 0)
    log_ratio = jnp.log(jnp.maximum(n, 1).astype(jnp.float32) / max_exact) / math.log(REL_MAX_DIST / max_exact)
    large = jnp.minimum(max_exact + (log_ratio * (REL_BUCKETS - max_exact)).astype(jnp.int32), REL_BUCKETS - 1)
    return jnp.where(n < max_exact, n, large)


def shared_kv(x, kv_norm, kv_w):
    B, T, _ = x.shape
    kv = rmsnorm(x, kv_norm) @ kv_w
    k = kv[..., :DA_QDIM].reshape(B, T, DA_HEADS, 2, DA_HEAD_DIM)
    v = kv[..., DA_QDIM:].reshape(B, T, DA_HEADS, DA_V_DIM)
    return k, v


def diff_attention(h, k, v, w_q, w_o, lam_q1, lam_k1, lam_q2, lam_k2, subln_g, rel_table, lambda_init):
    B, T, _ = h.shape
    q = (h @ w_q).reshape(B, T, DA_HEADS, 2, DA_HEAD_DIM) * (DA_HEAD_DIM ** -0.5)
    f32 = jnp.float32
    lam = (jnp.exp(jnp.sum(lam_q1.astype(f32) * lam_k1.astype(f32)))
           - jnp.exp(jnp.sum(lam_q2.astype(f32) * lam_k2.astype(f32))) + lambda_init)
    nb = T // Q_BLOCK
    q_blocks = q.reshape(B, nb, Q_BLOCK, DA_HEADS, 2, DA_HEAD_DIM).swapaxes(0, 1)
    k_pos = jnp.arange(T, dtype=jnp.int32)
    table = rel_table.astype(f32)

    def block(args):
        qb, bi = args
        q_pos = bi * Q_BLOCK + jnp.arange(Q_BLOCK, dtype=jnp.int32)
        rel = q_pos[:, None] - k_pos[None, :]
        bias = jnp.where((rel >= 0)[..., None], table[t5_bucket(rel)], -jnp.inf)
        bias = jnp.transpose(bias, (2, 0, 1))[None, :, None]
        s = jnp.einsum('bqhcd,bkhcd->bhcqk', qb, k, preferred_element_type=f32) + bias
        p = jax.nn.softmax(s, axis=-1)
        attn = p[:, :, 0] - lam * p[:, :, 1]
        return jnp.einsum('bhqk,bkhv->bqhv', attn, v, preferred_element_type=f32)

    o = lax.map(block, (q_blocks, jnp.arange(nb, dtype=jnp.int32)))
    o = o.swapaxes(0, 1).reshape(B, T, DA_HEADS, DA_V_DIM)
    o = rmsnorm(o, subln_g) * (1.0 - lambda_init)
    return (o.reshape(B, T, DA_VDIM_TOTAL) @ w_o).astype(h.dtype)


def conv_ffn(h, w_up, conv_w, conv_b, w_down):
    u = h @ w_up
    c = u.shape[-1]
    u = lax.conv_general_dilated(
        u, conv_w.reshape(CONV_W, 1, c).astype(u.dtype), window_strides=(1,),
        padding=[(CONV_W - 1, 0)], dimension_numbers=('NWC', 'WIO', 'NWC'),
        feature_group_count=c) + conv_b
    gate, val = jnp.split(u, 2, axis=-1)
    return ((jax.nn.silu(gate) * val) @ w_down).astype(h.dtype)


def setup_inputs(seed: int = 0) -> dict:
    key = jax.random.key(seed)
    ks = jax.random.split(key, 22)

    def nrm(k, shape, scale):
        return jax.random.normal(k, shape, jnp.float32) * scale

    def gain(k, shape):
        return 1.0 + nrm(k, shape, 0.02)

    D = D_MODEL
    return {
        'x': nrm(ks[0], (BATCH, SEQ, D), 1.0),
        'a_w_in': nrm(ks[1], (N_A, D, 3 * HG_FDIM + HG_HEADS * HG_HEAD_V), D ** -0.5),
        'a_w_out': nrm(ks[2], (N_A, HG_HEADS * HG_HEAD_V, D), (HG_HEADS * HG_HEAD_V) ** -0.5),
        'a_gnorm': gain(ks[3], (N_A, HG_HEADS * HG_HEAD_V)),
        'a_lb_logits': nrm(ks[4], (N_A + 1, HG_FDIM), 0.5),
        'b_w_q': nrm(ks[5], (N_B, D, DA_QDIM), D ** -0.5),
        'b_w_o': nrm(ks[6], (N_B, DA_VDIM_TOTAL, D), DA_VDIM_TOTAL ** -0.5),
        'b_lam_q1': nrm(ks[7], (N_B, DA_HEAD_DIM), 0.1),
        'b_lam_k1': nrm(ks[8], (N_B, DA_HEAD_DIM), 0.1),
        'b_lam_q2': nrm(ks[9], (N_B, DA_HEAD_DIM), 0.1),
        'b_lam_k2': nrm(ks[10], (N_B, DA_HEAD_DIM), 0.1),
        'b_subln': gain(ks[11], (N_B, DA_V_DIM)),
        'kv_norm': gain(ks[12], (D,)),
        'kv_w': nrm(ks[13], (D, DA_QDIM + DA_VDIM_TOTAL), D ** -0.5),
        'rel_table': nrm(ks[14], (REL_BUCKETS, DA_HEADS), 0.5),
        'norm_mix': gain(ks[15], (DEPTH, D)),
        'norm_ffn': gain(ks[16], (DEPTH, D)),
        'ffn_w_up': nrm(ks[17], (DEPTH, D, 2 * D_FF), D ** -0.5),
        'ffn_conv_w': nrm(ks[18], (DEPTH, CONV_W, 2 * D_FF), CONV_W ** -0.5),
        'ffn_conv_b': nrm(ks[19], (DEPTH, 2 * D_FF), 0.02),
        'ffn_w_down': nrm(ks[20], (DEPTH, D_FF, D), D_FF ** -0.5),
        'final_norm': gain(ks[21], (D,)),
    }


def reference(x, a_w_in, a_w_out, a_gnorm, a_lb_logits, b_w_q, b_w_o, b_lam_q1, b_lam_k1,
              b_lam_q2, b_lam_k2, b_subln, kv_norm, kv_w, rel_table, norm_mix, norm_ffn,
              ffn_w_up, ffn_conv_w, ffn_conv_b, ffn_w_down, final_norm):
    lbs = jnp.cumsum(jax.nn.softmax(a_lb_logits.astype(jnp.float32), axis=0), axis=0)
    k_shared = None
    v_shared = None
    for li in range(DEPTH):
        if li == N_A:
            k_shared, v_shared = shared_kv(x, kv_norm, kv_w)
        h = rmsnorm(x, norm_mix[li])
        if li < N_A:
            x = x + hgrn2_mixer(h, a_w_in[li], a_w_out[li], a_gnorm[li], lbs[li])
        else:
            j = li - N_A
            lambda_init = 0.8 - 0.6 * math.exp(-0.3 * li)
            x = x + diff_attention(h, k_shared, v_shared, b_w_q[j], b_w_o[j], b_lam_q1[j], b_lam_k1[j],
                                   b_lam_q2[j], b_lam_k2[j], b_subln[j], rel_table, lambda_init)
        x = x + conv_ffn(rmsnorm(x, norm_ffn[li]), ffn_w_up[li], ffn_conv_w[li], ffn_conv_b[li], ffn_w_down[li])
    return rmsnorm(x, final_norm)
```

```python
import functools
import math

import jax
import jax.numpy as jnp
from jax import lax
from jax.experimental import pallas as pl
from jax.experimental.pallas import tpu as pltpu

F32 = jnp.float32
BF16 = jnp.bfloat16
EPS = 1e-6

LANES = 128
SUBLANES = 8
VMEM_LIMIT_BYTES = 56 * 2**20

HG_HEADS = 8
DA_HEADS = 8
DA_HEAD_DIM = 64
REL_BUCKETS = 32
REL_MAX_DIST = 128
CONV_W = 3
NEG = -1e30

HG_CHUNK = 128
ATT_BLOCK = 512


def _t5_bucket_lower_bounds():
    max_exact = REL_BUCKETS // 2
    lo = list(range(max_exact))
    bucket_of = lambda n: min(
        max_exact + int(math.log(n / max_exact) / math.log(REL_MAX_DIST / max_exact) * (REL_BUCKETS - max_exact)),
        REL_BUCKETS - 1)
    n = max_exact
    for b in range(max_exact, REL_BUCKETS):
        while bucket_of(n) < b:
            n += 1
        lo.append(n)
    return tuple(lo)


T5_BUCKET_LO = _t5_bucket_lower_bounds()


def _dot(a, b):
    return jnp.dot(a, b, preferred_element_type=F32)


def _dot_nt(a, b):
    return lax.dot_general(a, b, (((1,), (1,)), ((), ())), preferred_element_type=F32)


def _dot_tn(a, b):
    return lax.dot_general(a, b, (((0,), (0,)), ((), ())), preferred_element_type=F32)


def _rms_inv(x):
    return lax.rsqrt(jnp.mean(x * x, axis=-1, keepdims=True) + EPS)


def _resident(shape):
    nd = len(shape)
    return pl.BlockSpec(shape, lambda *_: (0,) * nd, pipeline_mode=pl.Buffered(1))


def _params(sem):
    return pltpu.CompilerParams(dimension_semantics=sem, vmem_limit_bytes=VMEM_LIMIT_BYTES)


def _hg_in_kernel(x_ref, g_ref, w_ref, lbl_ref, q_ref, k_ref, lf_ref, v_ref, gs_ref, *, layer):
    fd = q_ref.shape[1]
    x = x_ref[...]
    h = (x * _rms_inv(x) * g_ref[...]).astype(BF16)
    lg = lbl_ref[...]
    ex = jnp.exp(lg - jnp.max(lg, axis=0, keepdims=True))
    sm = ex / jnp.sum(ex, axis=0, keepdims=True)
    lb = jnp.sum(sm[:layer + 1], axis=0, keepdims=True)
    log_lb = jnp.log(lb)
    log_1m_lb = jnp.log1p(-lb)

    zq = _dot(h, w_ref[:, 0:fd])
    q_ref[...] = (zq * jax.nn.sigmoid(zq)).astype(q_ref.dtype)
    zf = _dot(h, w_ref[:, fd:2 * fd])
    c = log_1m_lb + jax.nn.log_sigmoid(zf)
    lf_ref[...] = jnp.logaddexp(log_lb, c)
    k_ref[...] = jnp.exp(c - zf).astype(k_ref.dtype)
    zi = _dot(h, w_ref[:, 2 * fd:3 * fd])
    v_ref[...] = zi.astype(v_ref.dtype)
    zg = _dot(h, w_ref[:, 3 * fd:])
    gs_ref[...] = (zg * jax.nn.sigmoid(zg)).astype(gs_ref.dtype)


def _hg_in(x, g, w, lb_logits, layer, tm):
    T, D = x.shape
    fd = lb_logits.shape[1]
    dvt = w.shape[1] - 3 * fd
    row = lambda n: pl.BlockSpec((tm, n), lambda i: (i, 0))
    return pl.pallas_call(
        functools.partial(_hg_in_kernel, layer=layer),
        out_shape=(jax.ShapeDtypeStruct((T, fd), BF16), jax.ShapeDtypeStruct((T, fd), BF16),
                   jax.ShapeDtypeStruct((T, fd), F32), jax.ShapeDtypeStruct((T, dvt), BF16),
                   jax.ShapeDtypeStruct((T, dvt), BF16)),
        grid=(T // tm,),
        in_specs=[row(D), _resident((1, D)), _resident(w.shape), _resident(lb_logits.shape)],
        out_specs=(row(fd), row(fd), row(fd), row(dvt), row(dvt)),
        compiler_params=_params(("parallel",)),
        name="hgrn2_in_proj",
    )(x, g, w, lb_logits)


def _level_ref(bh, w):
    C, d = bh.shape
    n = 2 * w
    if w >= SUBLANES:
        return jnp.concatenate(
            [jnp.broadcast_to(bh[i * n + w - 1:i * n + w, :], (n, d)) for i in range(C // n)], axis=0)
    b3 = bh.reshape(C // SUBLANES, SUBLANES, d)
    pick = lambda s: jnp.broadcast_to(b3[:, s:s + 1, :], b3.shape)
    sub = lax.broadcasted_iota(jnp.int32, b3.shape, 1)
    r = pick(SUBLANES - n + w - 1)
    for s in range(SUBLANES - 2 * n, -1, -n):
        r = jnp.where(sub < s + n, pick(s + w - 1), r)
    return r.reshape(C, d)


def _hg_rec_kernel(q_ref, k_ref, lf_ref, v_ref, o_ref, st_ref, b_ref, *, heads):
    C, fd = q_ref.shape
    dk = fd // heads
    dv = v_ref.shape[1] // heads

    @pl.when(pl.program_id(0) == 0)
    def _zero_state():
        st_ref[...] = jnp.zeros_like(st_ref)

    row = lax.broadcasted_iota(jnp.int32, (C, C), 0)
    col = lax.broadcasted_iota(jnp.int32, (C, C), 1)

    tri = jnp.where(row >= col, 1.0, 0.0).astype(BF16)
    lf = lf_ref[...]
    p0 = lf.astype(BF16)
    r1 = lf - p0.astype(F32)
    p1 = r1.astype(BF16)
    p2 = (r1 - p1.astype(F32)).astype(BF16)
    b_ref[...] = _dot(tri, p0) + _dot(tri, p1) + _dot(tri, p2)

    levels = [1 << i for i in range(int(math.log2(C)))]
    xor = row ^ col
    below = row > col
    level_mask = [below & ((xor >> int(math.log2(w))) == 1) for w in levels]
    diag = row == col

    for h in range(heads):
        sk = slice(h * dk, (h + 1) * dk)
        sv = slice(h * dv, (h + 1) * dv)
        bh = b_ref[:, sk]
        qh = q_ref[:, sk].astype(F32)
        kh = k_ref[:, sk].astype(F32)
        vh = v_ref[:, sv]
        b_last = bh[C - 1:C, :]

        a = jnp.where(diag, _dot_nt(q_ref[:, sk], k_ref[:, sk]), 0.0)
        for w, mk in zip(levels, level_mask):
            e = jnp.exp(-jnp.abs(bh - _level_ref(bh, w)))
            a = a + jnp.where(mk, _dot_nt((qh * e).astype(BF16), (kh * e).astype(BF16)), 0.0)
        o = _dot(a.astype(BF16), vh)

        st = st_ref[h]
        o = o + _dot_nt((qh * jnp.exp(bh)).astype(BF16), st.astype(BF16))
        k_dec = (kh * jnp.exp(b_last - bh)).astype(BF16)
        st_ref[h] = st * jnp.exp(b_last) + _dot_tn(vh, k_dec)
        o_ref[:, sv] = o.astype(o_ref.dtype)


def _hg_rec(q, k, lf, v, heads, chunk):
    T, fd = q.shape
    dvt = v.shape[1]
    row = lambda n: pl.BlockSpec((chunk, n), lambda i: (i, 0))
    return pl.pallas_call(
        functools.partial(_hg_rec_kernel, heads=heads),
        out_shape=jax.ShapeDtypeStruct((T, dvt), BF16),
        grid=(T // chunk,),
        in_specs=[row(fd), row(fd), row(fd), row(dvt)],
        out_specs=row(dvt),
        scratch_shapes=[pltpu.VMEM((heads, dvt // heads, fd // heads), F32), pltpu.VMEM((chunk, fd), F32)],
        compiler_params=_params(("arbitrary",)),
        name="hgrn2_recurrence",
    )(q, k, lf, v)


def _hg_out_kernel(x_ref, o_ref, gs_ref, gn_ref, w_ref, y_ref):
    o = o_ref[...].astype(F32)
    n = o * _rms_inv(o) * gn_ref[...] * gs_ref[...].astype(F32)
    y_ref[...] = x_ref[...] + _dot(n.astype(BF16), w_ref[...])


def _hg_out(x, o, gs, gn, w, tm):
    T, D = x.shape
    row = lambda n: pl.BlockSpec((tm, n), lambda i: (i, 0))
    return pl.pallas_call(
        _hg_out_kernel,
        out_shape=jax.ShapeDtypeStruct((T, D), F32),
        grid=(T // tm,),
        in_specs=[row(D), row(o.shape[1]), row(gs.shape[1]), _resident(gn.shape), _resident(w.shape)],
        out_specs=row(D),
        compiler_params=_params(("parallel",)),
        name="hgrn2_out_proj",
    )(x, o, gs, gn, w)


def _attn_out_kernel(x_ref, a_ref, w_ref, y_ref):
    y_ref[...] = x_ref[...] + _dot(a_ref[...], w_ref[...])


def _attn_out(x, a, w, tm):
    T, D = x.shape
    row = lambda n: pl.BlockSpec((tm, n), lambda i: (i, 0))
    return pl.pallas_call(
        _attn_out_kernel,
        out_shape=jax.ShapeDtypeStruct((T, D), F32),
        grid=(T // tm,),
        in_specs=[row(D), row(a.shape[1]), _resident(w.shape)],
        out_specs=row(D),
        compiler_params=_params(("parallel",)),
        name="attn_out_proj",
    )(x, a, w)


def _ffn_kernel(x_ref, g_ref, wup_ref, cw_ref, cb_ref, wdn_ref, fg_ref, y_ref, u_ref, act_ref, *, final_norm):
    tm = x_ref.shape[0]
    ff = wdn_ref.shape[0]
    halo = SUBLANES

    @pl.when(pl.program_id(0) == 0)
    def _zero_halo():
        u_ref[0:halo, :] = jnp.zeros((halo, u_ref.shape[1]), F32)

    x = x_ref[...]
    h = (x * _rms_inv(x) * g_ref[...]).astype(BF16)
    up_cols = 512
    for c0 in range(0, 2 * ff, up_cols):
        u_ref[halo:halo + tm, c0:c0 + up_cols] = _dot(h, wup_ref[:, c0:c0 + up_cols])

    def conv(cs):
        acc = cb_ref[:, cs] + cw_ref[CONV_W - 1:CONV_W, cs] * u_ref[halo:halo + tm, cs]
        for j in range(CONV_W - 1):
            off = halo - (CONV_W - 1) + j
            acc = acc + cw_ref[j:j + 1, cs] * u_ref[off:off + tm, cs]
        return acc

    act_cols = 256
    for c0 in range(0, ff, act_cols):
        gate = conv(slice(c0, c0 + act_cols))
        val = conv(slice(ff + c0, ff + c0 + act_cols))
        act_ref[:, c0:c0 + act_cols] = (gate * jax.nn.sigmoid(gate) * val).astype(BF16)

    u_ref[0:halo, :] = u_ref[tm:tm + halo, :]
    y = x + _dot(act_ref[...], wdn_ref[...])
    if final_norm:
        y = y * _rms_inv(y) * fg_ref[...]
    y_ref[...] = y


def _ffn(x, g, wup, cw, cb, wdn, fg, final_norm, tm):
    T, D = x.shape
    ff = wdn.shape[0]
    row = pl.BlockSpec((tm, D), lambda i: (i, 0))
    return pl.pallas_call(
        functools.partial(_ffn_kernel, final_norm=final_norm),
        out_shape=jax.ShapeDtypeStruct((T, D), F32),
        grid=(T // tm,),
        in_specs=[row, _resident(g.shape), _resident(wup.shape), _resident(cw.shape), _resident(cb.shape),
                  _resident(wdn.shape), _resident(fg.shape)],
        out_specs=row,
        scratch_shapes=[pltpu.VMEM((tm + SUBLANES, 2 * ff), F32), pltpu.VMEM((tm, ff), BF16)],
        compiler_params=_params(("arbitrary",)),
        name="conv_ffn",
    )(x, g, wup, cw, cb, wdn, fg)


def _qkv_kernel(x_ref, gkv_ref, gq_ref, wk_ref, wv_ref, wq_ref, qT_ref, k_ref, vT_ref, *, scale):
    heads = k_ref.shape[0]
    hd = k_ref.shape[2]
    x = x_ref[...]
    xn = x * _rms_inv(x)
    hkv = (xn * gkv_ref[...]).astype(BF16)
    hq = (xn * gq_ref[...]).astype(BF16)
    k = _dot(hkv, wk_ref[...]).astype(BF16)
    for h in range(heads):
        k_ref[h] = k[:, h * hd:(h + 1) * hd]
    vT_ref[...] = _dot(hkv, wv_ref[...]).T.astype(BF16)
    qT_ref[...] = (_dot(hq, wq_ref[...]) * scale).T.astype(BF16)


def _qkv(x, gkv, gq, wk, wv, wq, heads, scale, tm):
    T, D = x.shape
    hd = wk.shape[1] // heads
    colblk = lambda n: pl.BlockSpec((n, tm), lambda i: (0, i))
    return pl.pallas_call(
        functools.partial(_qkv_kernel, scale=scale),
        out_shape=(jax.ShapeDtypeStruct((wq.shape[1], T), BF16), jax.ShapeDtypeStruct((heads, T, hd), BF16),
                   jax.ShapeDtypeStruct((wv.shape[1], T), BF16)),
        grid=(T // tm,),
        in_specs=[pl.BlockSpec((tm, D), lambda i: (i, 0)), _resident(gkv.shape), _resident(gq.shape),
                  _resident(wk.shape), _resident(wv.shape), _resident(wq.shape)],
        out_specs=(colblk(wq.shape[1]), pl.BlockSpec((heads, tm, hd), lambda i: (0, i, 0)), colblk(wv.shape[1])),
        compiler_params=_params(("parallel",)),
        name="qkv_proj",
    )(x, gkv, gq, wk, wv, wq)


def _attn_kernel(tbl_ref, lam_ref, g_ref, qT_ref, k_ref, vT_ref, o_ref,
                 qp_ref, bias_ref, m_ref, l_ref, acc_ref, *, lambda_init):
    hw, B = qT_ref.shape
    hd = hw // 2
    h = pl.program_id(0)
    qi = pl.program_id(1)
    nsub = B // LANES

    @pl.when(qi == 0)
    def _build_bias_tiles():
        far = tbl_ref[h * REL_BUCKETS + REL_BUCKETS - 1]
        a = lax.broadcasted_iota(jnp.int32, (LANES, LANES), 0)
        b = lax.broadcasted_iota(jnp.int32, (LANES, LANES), 1)

        def rel_bias(n):
            val = jnp.zeros(n.shape, F32)
            for bucket in range(REL_BUCKETS - 2, -1, -1):
                val = jnp.where(n < T5_BUCKET_LO[bucket + 1], tbl_ref[h * REL_BUCKETS + bucket] - far, val)
            return val

        e0 = jnp.where(b >= a, rel_bias(b - a), NEG)
        e1 = rel_bias(LANES + b - a)
        zero = jnp.zeros((LANES, LANES), F32)
        masked = jnp.full((LANES, LANES), NEG, F32)
        for r in range(nsub):
            for c in range(nsub):
                rs, cs = slice(r * LANES, (r + 1) * LANES), slice(c * LANES, (c + 1) * LANES)
                bias_ref[0, rs, cs] = masked if c < r else e0 if c == r else e1 if c == r + 1 else zero
                bias_ref[1, rs, cs] = e1 if (r == nsub - 1 and c == 0) else zero

    qT = qT_ref[...].astype(F32)
    half = lax.broadcasted_iota(jnp.int32, qT.shape, 0) < hd
    qp_ref[0] = jnp.where(half, qT, 0.0).astype(BF16)
    qp_ref[1] = jnp.where(half, 0.0, qT).astype(BF16)
    m_ref[...] = jnp.full(m_ref.shape, NEG, F32)
    l_ref[...] = jnp.zeros(l_ref.shape, F32)
    acc_ref[...] = jnp.zeros(acc_ref.shape, F32)

    def tile(j, bias):
        start = pl.multiple_of(j * B, B)
        ks = k_ref[pl.ds(start, B), :]
        vt = vT_ref[:, pl.ds(start, B)]
        for c in range(2):
            s = _dot(ks, qp_ref[c])
            if bias is not None:
                s = s + bias_ref[bias]
            m_old = m_ref[c]
            m_new = jnp.maximum(m_old, jnp.max(s, axis=0, keepdims=True))
            p = jnp.exp(s - m_new)
            alpha = jnp.exp(m_old - m_new)
            l_ref[c] = alpha * l_ref[c] + jnp.sum(p, axis=0, keepdims=True)
            acc_ref[c] = alpha * acc_ref[c] + _dot(vt, p.astype(BF16))
            m_ref[c] = m_new

    def far_tile(j, carry):
        tile(j, None)
        return carry

    lax.fori_loop(0, qi - 1, far_tile, 0)

    @pl.when(qi >= 1)
    def _previous_tile():
        tile(qi - 1, 1)

    tile(qi, 0)

    lam = (jnp.exp(jnp.sum(lam_ref[0:1, :] * lam_ref[1:2, :], axis=-1, keepdims=True))
           - jnp.exp(jnp.sum(lam_ref[2:3, :] * lam_ref[3:4, :], axis=-1, keepdims=True)) + lambda_init)
    oT = acc_ref[0] / l_ref[0] - lam * (acc_ref[1] / l_ref[1])
    inv = lax.rsqrt(jnp.mean(oT * oT, axis=0, keepdims=True) + EPS)
    y = oT * inv * g_ref[...] * (1.0 - lambda_init)
    o_ref[...] = y.T.astype(o_ref.dtype)


def _attention(tbl, lam, g, qT, k, vT, lambda_init, block):
    heads, T, hw = k.shape
    dv = vT.shape[0] // heads
    B = block
    smem = pl.BlockSpec(memory_space=pltpu.SMEM)
    return pl.pallas_call(
        functools.partial(_attn_kernel, lambda_init=lambda_init),
        out_shape=jax.ShapeDtypeStruct((T, heads * dv), BF16),
        grid=(heads, T // B),
        in_specs=[smem,
                  pl.BlockSpec(lam.shape, lambda h, i: (0, 0)),
                  pl.BlockSpec(g.shape, lambda h, i: (0, 0)),
                  pl.BlockSpec((hw, B), lambda h, i: (h, i)),
                  pl.BlockSpec((None, T, hw), lambda h, i: (h, 0, 0)),
                  pl.BlockSpec((dv, T), lambda h, i: (h, 0))],
        out_specs=pl.BlockSpec((B, dv), lambda h, i: (i, h)),
        scratch_shapes=[pltpu.VMEM((2, hw, B), BF16), pltpu.VMEM((2, B, B), F32),
                        pltpu.VMEM((2, 1, B), F32), pltpu.VMEM((2, 1, B), F32), pltpu.VMEM((2, dv, B), F32)],
        compiler_params=_params(("arbitrary", "arbitrary")),
        name="diff_attention",
    )(tbl, lam, g, qT, k, vT)


def kernel(x, a_w_in, a_w_out, a_gnorm, a_lb_logits, b_w_q, b_w_o, b_lam_q1, b_lam_k1, b_lam_q2, b_lam_k2,
           b_subln, kv_norm, kv_w, rel_table, norm_mix, norm_ffn, ffn_w_up, ffn_conv_w, ffn_conv_b,
           ffn_w_down, final_norm):
    batch, T, D = x.shape
    depth = norm_mix.shape[0]
    n_a = a_w_in.shape[0]
    qdim = b_w_q.shape[2]
    row = lambda v: v.reshape(1, -1).astype(F32)
    tm = min(512, T)
    chunk = min(HG_CHUNK, T)
    block = min(ATT_BLOCK, T)
    tbl = rel_table.astype(F32).T.reshape(-1)

    outs = []
    for bi in range(batch):
        xs = x[bi]
        qT = k3 = vT = None
        for li in range(depth):
            if li == n_a:
                j = 0
                qT, k3, vT = _qkv(xs, row(kv_norm), row(norm_mix[li]), kv_w[:, :qdim].astype(BF16),
                                  kv_w[:, qdim:].astype(BF16), b_w_q[j].astype(BF16), DA_HEADS,
                                  DA_HEAD_DIM ** -0.5, tm)
            if li < n_a:
                q, k, lf, v, gs = _hg_in(xs, row(norm_mix[li]), a_w_in[li].astype(BF16),
                                         a_lb_logits.astype(F32), li, tm)
                o = _hg_rec(q, k, lf, v, HG_HEADS, chunk)
                xs = _hg_out(xs, o, gs, row(a_gnorm[li]), a_w_out[li].astype(BF16), tm)
            else:
                j = li - n_a
                if j > 0:
                    qT, _, _ = _qkv(xs, row(kv_norm), row(norm_mix[li]), kv_w[:, :qdim].astype(BF16),
                                    kv_w[:, qdim:].astype(BF16), b_w_q[j].astype(BF16), DA_HEADS,
                                    DA_HEAD_DIM ** -0.5, tm)
                lambda_init = 0.8 - 0.6 * math.exp(-0.3 * li)
                lam = jnp.stack([b_lam_q1[j], b_lam_k1[j], b_lam_q2[j], b_lam_k2[j]]).astype(F32)
                att = _attention(tbl, lam, b_subln[j].reshape(-1, 1).astype(F32), qT, k3, vT, lambda_init, block)
                xs = _attn_out(xs, att, b_w_o[j].astype(BF16), tm)
            last = li == depth - 1
            xs = _ffn(xs, row(norm_ffn[li]), ffn_w_up[li].astype(BF16), ffn_conv_w[li].astype(F32),
                      row(ffn_conv_b[li]), ffn_w_down[li].astype(BF16), row(final_norm), last, min(256, T))
        outs.append(xs)
    return jnp.stack(outs).astype(x.dtype)
```

```python
import functools
import math

import jax
import jax.numpy as jnp
from jax import lax
from jax.experimental import pallas as pl
from jax.experimental.pallas import tpu as pltpu

F32 = jnp.float32
BF16 = jnp.bfloat16
EPS = 1e-6

LANES = 128
SUBLANES = 8
VMEM_LIMIT_BYTES = 56 * 2**20

HG_HEADS = 8
DA_HEADS = 8
DA_HEAD_DIM = 64
REL_BUCKETS = 32
REL_MAX_DIST = 128
CONV_W = 3
NEG = -1e30
LOG2E = math.log2(math.e)
ONES_ROWS = 2 * SUBLANES

HG_CHUNK = 128
ATT_BLOCK = 512


def _t5_bucket_lower_bounds():
    max_exact = REL_BUCKETS // 2
    lo = list(range(max_exact))
    bucket_of = lambda n: min(
        max_exact + int(math.log(n / max_exact) / math.log(REL_MAX_DIST / max_exact) * (REL_BUCKETS - max_exact)),
        REL_BUCKETS - 1)
    n = max_exact
    for b in range(max_exact, REL_BUCKETS):
        while bucket_of(n) < b:
            n += 1
        lo.append(n)
    return tuple(lo)


T5_BUCKET_LO = _t5_bucket_lower_bounds()


def _dot(a, b):
    return jnp.dot(a, b, preferred_element_type=F32)


def _dot_nt(a, b):
    return lax.dot_general(a, b, (((1,), (1,)), ((), ())), preferred_element_type=F32)


def _dot_tn(a, b):
    return lax.dot_general(a, b, (((0,), (0,)), ((), ())), preferred_element_type=F32)


def _rms_inv(x):
    return lax.rsqrt(jnp.mean(x * x, axis=-1, keepdims=True) + EPS)


def _resident(shape):
    nd = len(shape)
    return pl.BlockSpec(shape, lambda *_: (0,) * nd, pipeline_mode=pl.Buffered(1))


def _params(sem):
    return pltpu.CompilerParams(dimension_semantics=sem, vmem_limit_bytes=VMEM_LIMIT_BYTES)


def _hg_in_kernel(x_ref, g_ref, w_ref, lbl_ref, q_ref, k_ref, lf_ref, v_ref, gs_ref, *, layer):
    fd = q_ref.shape[1]
    x = x_ref[...]
    h = (x * _rms_inv(x) * g_ref[...]).astype(BF16)
    lg = lbl_ref[...]
    ex = jnp.exp(lg - jnp.max(lg, axis=0, keepdims=True))
    sm = ex / jnp.sum(ex, axis=0, keepdims=True)
    lb = jnp.sum(sm[:layer + 1], axis=0, keepdims=True)
    log_lb = jnp.log(lb)
    log_1m_lb = jnp.log1p(-lb)

    zq = _dot(h, w_ref[:, 0:fd])
    q_ref[...] = (zq * jax.nn.sigmoid(zq)).astype(q_ref.dtype)
    zf = _dot(h, w_ref[:, fd:2 * fd])
    c = log_1m_lb + jax.nn.log_sigmoid(zf)
    lf_ref[...] = jnp.logaddexp(log_lb, c)
    k_ref[...] = jnp.exp(c - zf).astype(k_ref.dtype)
    zi = _dot(h, w_ref[:, 2 * fd:3 * fd])
    v_ref[...] = zi.astype(v_ref.dtype)
    zg = _dot(h, w_ref[:, 3 * fd:])
    gs_ref[...] = (zg * jax.nn.sigmoid(zg)).astype(gs_ref.dtype)


def _hg_in(x, g, w, lb_logits, layer, tm):
    T, D = x.shape
    fd = lb_logits.shape[1]
    dvt = w.shape[1] - 3 * fd
    row = lambda n: pl.BlockSpec((tm, n), lambda i: (i, 0))
    return pl.pallas_call(
        functools.partial(_hg_in_kernel, layer=layer),
        out_shape=(jax.ShapeDtypeStruct((T, fd), BF16), jax.ShapeDtypeStruct((T, fd), BF16),
                   jax.ShapeDtypeStruct((T, fd), F32), jax.ShapeDtypeStruct((T, dvt), BF16),
                   jax.ShapeDtypeStruct((T, dvt), BF16)),
        grid=(T // tm,),
        in_specs=[row(D), _resident((1, D)), _resident(w.shape), _resident(lb_logits.shape)],
        out_specs=(row(fd), row(fd), row(fd), row(dvt), row(dvt)),
        compiler_params=_params(("parallel",)),
        name="hgrn2_in_proj",
    )(x, g, w, lb_logits)


def _level_ref(bh, w):
    C, d = bh.shape
    n = 2 * w
    if w >= SUBLANES:
        return jnp.concatenate(
            [jnp.broadcast_to(bh[i * n + w - 1:i * n + w, :], (n, d)) for i in range(C // n)], axis=0)
    b3 = bh.reshape(C // SUBLANES, SUBLANES, d)
    pick = lambda s: jnp.broadcast_to(b3[:, s:s + 1, :], b3.shape)
    sub = lax.broadcasted_iota(jnp.int32, b3.shape, 1)
    r = pick(SUBLANES - n + w - 1)
    for s in range(SUBLANES - 2 * n, -1, -n):
        r = jnp.where(sub < s + n, pick(s + w - 1), r)
    return r.reshape(C, d)


def _hg_rec_kernel(q_ref, k_ref, lf_ref, v_ref, o_ref, st_ref, b_ref, *, heads):
    C, fd = q_ref.shape
    dk = fd // heads
    dv = v_ref.shape[1] // heads

    @pl.when(pl.program_id(0) == 0)
    def _zero_state():
        st_ref[...] = jnp.zeros_like(st_ref)

    row = lax.broadcasted_iota(jnp.int32, (C, C), 0)
    col = lax.broadcasted_iota(jnp.int32, (C, C), 1)

    tri = jnp.where(row >= col, 1.0, 0.0).astype(BF16)
    lf = lf_ref[...]
    p0 = lf.astype(BF16)
    r1 = lf - p0.astype(F32)
    p1 = r1.astype(BF16)
    p2 = (r1 - p1.astype(F32)).astype(BF16)
    b_ref[...] = _dot(tri, p0) + _dot(tri, p1) + _dot(tri, p2)

    levels = [1 << i for i in range(int(math.log2(C)))]
    xor = row ^ col
    below = row > col
    level_mask = [below & ((xor >> int(math.log2(w))) == 1) for w in levels]
    diag = row == col

    for h in range(heads):
        sk = slice(h * dk, (h + 1) * dk)
        sv = slice(h * dv, (h + 1) * dv)
        bh = b_ref[:, sk]
        qh = q_ref[:, sk].astype(F32)
        kh = k_ref[:, sk].astype(F32)
        vh = v_ref[:, sv]
        b_last = bh[C - 1:C, :]

        a = jnp.where(diag, _dot_nt(q_ref[:, sk], k_ref[:, sk]), 0.0)
        for w, mk in zip(levels, level_mask):
            e = jnp.exp(-jnp.abs(bh - _level_ref(bh, w)))
            a = a + jnp.where(mk, _dot_nt((qh * e).astype(BF16), (kh * e).astype(BF16)), 0.0)
        o = _dot(a.astype(BF16), vh)

        st = st_ref[h]
        o = o + _dot_nt((qh * jnp.exp(bh)).astype(BF16), st.astype(BF16))
        k_dec = (kh * jnp.exp(b_last - bh)).astype(BF16)
        st_ref[h] = st * jnp.exp(b_last) + _dot_tn(vh, k_dec)
        o_ref[:, sv] = o.astype(o_ref.dtype)


def _hg_rec(q, k, lf, v, heads, chunk):
    T, fd = q.shape
    dvt = v.shape[1]
    row = lambda n: pl.BlockSpec((chunk, n), lambda i: (i, 0))
    return pl.pallas_call(
        functools.partial(_hg_rec_kernel, heads=heads),
        out_shape=jax.ShapeDtypeStruct((T, dvt), BF16),
        grid=(T // chunk,),
        in_specs=[row(fd), row(fd), row(fd), row(dvt)],
        out_specs=row(dvt),
        scratch_shapes=[pltpu.VMEM((heads, dvt // heads, fd // heads), F32), pltpu.VMEM((chunk, fd), F32)],
        compiler_params=_params(("arbitrary",)),
        name="hgrn2_recurrence",
    )(q, k, lf, v)


def _hg_out_kernel(x_ref, o_ref, gs_ref, gn_ref, w_ref, y_ref):
    o = o_ref[...].astype(F32)
    n = o * _rms_inv(o) * gn_ref[...] * gs_ref[...].astype(F32)
    y_ref[...] = x_ref[...] + _dot(n.astype(BF16), w_ref[...])


def _hg_out(x, o, gs, gn, w, tm):
    T, D = x.shape
    row = lambda n: pl.BlockSpec((tm, n), lambda i: (i, 0))
    return pl.pallas_call(
        _hg_out_kernel,
        out_shape=jax.ShapeDtypeStruct((T, D), F32),
        grid=(T // tm,),
        in_specs=[row(D), row(o.shape[1]), row(gs.shape[1]), _resident(gn.shape), _resident(w.shape)],
        out_specs=row(D),
        compiler_params=_params(("parallel",)),
        name="hgrn2_out_proj",
    )(x, o, gs, gn, w)


def _attn_out_kernel(x_ref, a_ref, w_ref, y_ref):
    y_ref[...] = x_ref[...] + _dot(a_ref[...], w_ref[...])


def _attn_out(x, a, w, tm):
    T, D = x.shape
    row = lambda n: pl.BlockSpec((tm, n), lambda i: (i, 0))
    return pl.pallas_call(
        _attn_out_kernel,
        out_shape=jax.ShapeDtypeStruct((T, D), F32),
        grid=(T // tm,),
        in_specs=[row(D), row(a.shape[1]), _resident(w.shape)],
        out_specs=row(D),
        compiler_params=_params(("parallel",)),
        name="attn_out_proj",
    )(x, a, w)


def _ffn_kernel(x_ref, g_ref, wup_ref, cw_ref, cb_ref, wdn_ref, fg_ref, y_ref, u_ref, act_ref, *, final_norm):
    tm = x_ref.shape[0]
    ff = wdn_ref.shape[0]
    halo = SUBLANES

    @pl.when(pl.program_id(0) == 0)
    def _zero_halo():
        u_ref[0:halo, :] = jnp.zeros((halo, u_ref.shape[1]), F32)

    x = x_ref[...]
    h = (x * _rms_inv(x) * g_ref[...]).astype(BF16)
    up_cols = 512
    for c0 in range(0, 2 * ff, up_cols):
        u_ref[halo:halo + tm, c0:c0 + up_cols] = _dot(h, wup_ref[:, c0:c0 + up_cols])

    def conv(cs):
        acc = cb_ref[:, cs] + cw_ref[CONV_W - 1:CONV_W, cs] * u_ref[halo:halo + tm, cs]
        for j in range(CONV_W - 1):
            off = halo - (CONV_W - 1) + j
            acc = acc + cw_ref[j:j + 1, cs] * u_ref[off:off + tm, cs]
        return acc

    act_cols = 256
    for c0 in range(0, ff, act_cols):
        gate = conv(slice(c0, c0 + act_cols))
        val = conv(slice(ff + c0, ff + c0 + act_cols))
        act_ref[:, c0:c0 + act_cols] = (gate * jax.nn.sigmoid(gate) * val).astype(BF16)

    u_ref[0:halo, :] = u_ref[tm:tm + halo, :]
    y = x + _dot(act_ref[...], wdn_ref[...])
    if final_norm:
        y = y * _rms_inv(y) * fg_ref[...]
    y_ref[...] = y


def _ffn(x, g, wup, cw, cb, wdn, fg, final_norm, tm):
    T, D = x.shape
    ff = wdn.shape[0]
    row = pl.BlockSpec((tm, D), lambda i: (i, 0))
    return pl.pallas_call(
        functools.partial(_ffn_kernel, final_norm=final_norm),
        out_shape=jax.ShapeDtypeStruct((T, D), F32),
        grid=(T // tm,),
        in_specs=[row, _resident(g.shape), _resident(wup.shape), _resident(cw.shape), _resident(cb.shape),
                  _resident(wdn.shape), _resident(fg.shape)],
        out_specs=row,
        scratch_shapes=[pltpu.VMEM((tm + SUBLANES, 2 * ff), F32), pltpu.VMEM((tm, ff), BF16)],
        compiler_params=_params(("arbitrary",)),
        name="conv_ffn",
    )(x, g, wup, cw, cb, wdn, fg)


def _qkv_kernel(x_ref, gkv_ref, gq_ref, wk_ref, wv_ref, wq_ref, qT_ref, k_ref, vT_ref, *, scale):
    heads = k_ref.shape[0]
    hd = k_ref.shape[2]
    x = x_ref[...]
    xn = x * _rms_inv(x)
    hkv = (xn * gkv_ref[...]).astype(BF16)
    hq = (xn * gq_ref[...]).astype(BF16)
    k = _dot(hkv, wk_ref[...]).astype(BF16)
    for h in range(heads):
        k_ref[h] = k[:, h * hd:(h + 1) * hd]
    vT_ref[...] = _dot(hkv, wv_ref[...]).T.astype(BF16)
    qT_ref[...] = (_dot(hq, wq_ref[...]) * scale).T.astype(BF16)


def _qkv(x, gkv, gq, wk, wv, wq, heads, scale, tm):
    T, D = x.shape
    hd = wk.shape[1] // heads
    colblk = lambda n: pl.BlockSpec((n, tm), lambda i: (0, i))
    return pl.pallas_call(
        functools.partial(_qkv_kernel, scale=scale),
        out_shape=(jax.ShapeDtypeStruct((wq.shape[1], T), BF16), jax.ShapeDtypeStruct((heads, T, hd), BF16),
                   jax.ShapeDtypeStruct((wv.shape[1], T), BF16)),
        grid=(T // tm,),
        in_specs=[pl.BlockSpec((tm, D), lambda i: (i, 0)), _resident(gkv.shape), _resident(gq.shape),
                  _resident(wk.shape), _resident(wv.shape), _resident(wq.shape)],
        out_specs=(colblk(wq.shape[1]), pl.BlockSpec((heads, tm, hd), lambda i: (0, i, 0)), colblk(wv.shape[1])),
        compiler_params=_params(("parallel",)),
        name="qkv_proj",
    )(x, gkv, gq, wk, wv, wq)


def _attn_kernel(tbl_ref, lam_ref, g_ref, qT_ref, k_ref, vT_ref, o_ref,
                 qp_ref, bias_ref, m_ref, acc_ref, *, lambda_init):
    hw, B = qT_ref.shape
    hd = hw // 2
    dv = vT_ref.shape[0]
    h = pl.program_id(0)
    qi = pl.program_id(1)
    nsub = B // LANES

    @pl.when(qi == 0)
    def _build_bias_tiles():
        far = tbl_ref[h * REL_BUCKETS + REL_BUCKETS - 1]
        a = lax.broadcasted_iota(jnp.int32, (LANES, LANES), 0)
        b = lax.broadcasted_iota(jnp.int32, (LANES, LANES), 1)

        def rel_bias(n):
            val = jnp.zeros(n.shape, F32)
            for bucket in range(REL_BUCKETS - 2, -1, -1):
                val = jnp.where(n < T5_BUCKET_LO[bucket + 1], (tbl_ref[h * REL_BUCKETS + bucket] - far) * LOG2E, val)
            return val

        e0 = jnp.where(b >= a, rel_bias(b - a), NEG)
        e1 = rel_bias(LANES + b - a)
        zero = jnp.zeros((LANES, LANES), F32)
        masked = jnp.full((LANES, LANES), NEG, F32)
        for r in range(nsub):
            for c in range(nsub):
                rs, cs = slice(r * LANES, (r + 1) * LANES), slice(c * LANES, (c + 1) * LANES)
                bias_ref[0, rs, cs] = masked if c < r else e0 if c == r else e1 if c == r + 1 else zero
                bias_ref[1, rs, cs] = e1 if (r == nsub - 1 and c == 0) else zero

    qT = qT_ref[...].astype(F32)
    half = lax.broadcasted_iota(jnp.int32, qT.shape, 0) < hd
    qp_ref[0] = jnp.where(half, qT, 0.0).astype(BF16)
    qp_ref[1] = jnp.where(half, 0.0, qT).astype(BF16)
    m_ref[...] = jnp.full(m_ref.shape, NEG, F32)
    acc_ref[...] = jnp.zeros(acc_ref.shape, F32)
    ones = jnp.ones((ONES_ROWS, B), BF16)

    def process(tiles):
        starts = [pl.multiple_of(j * B, B) for j, _ in tiles]
        scores = [[_dot(k_ref[pl.ds(st, B), :], qp_ref[c]) for c in range(2)] for st in starts]
        for (j, bias), st, sc in zip(tiles, starts, scores):
            vt = jnp.concatenate([vT_ref[:, pl.ds(st, B)], ones], axis=0)
            for c in range(2):
                s = sc[c] if bias is None else sc[c] + bias_ref[bias]
                m_old = m_ref[c]
                m_new = jnp.maximum(m_old, jnp.max(s, axis=0, keepdims=True))
                p = jnp.exp2(s - m_new).astype(BF16)
                acc_ref[c] = jnp.exp2(m_old - m_new) * acc_ref[c] + _dot(vt, p)
                m_ref[c] = m_new

    n_far = jnp.maximum(qi - 1, 0)

    def far_pair(jj, carry):
        process([(2 * jj, None), (2 * jj + 1, None)])
        return carry

    lax.fori_loop(0, lax.shift_right_logical(n_far, 1), far_pair, 0)

    @pl.when((n_far & 1) == 1)
    def _odd_far_tile():
        process([(n_far - 1, None)])

    @pl.when(qi >= 1)
    def _previous_and_diagonal():
        process([(qi - 1, 1), (qi, 0)])

    @pl.when(qi == 0)
    def _diagonal_only():
        process([(qi, 0)])

    lam = (jnp.exp(jnp.sum(lam_ref[0:1, :] * lam_ref[1:2, :], axis=-1, keepdims=True))
           - jnp.exp(jnp.sum(lam_ref[2:3, :] * lam_ref[3:4, :], axis=-1, keepdims=True)) + lambda_init)
    o1 = acc_ref[0, 0:dv, :] / acc_ref[0, dv:dv + 1, :]
    o2 = acc_ref[1, 0:dv, :] / acc_ref[1, dv:dv + 1, :]
    oT = o1 - lam * o2
    inv = lax.rsqrt(jnp.mean(oT * oT, axis=0, keepdims=True) + EPS)
    y = oT * inv * g_ref[...] * (1.0 - lambda_init)
    o_ref[...] = y.T.astype(o_ref.dtype)


def _attention(tbl, lam, g, qT, k, vT, lambda_init, block):
    heads, T, hw = k.shape
    dv = vT.shape[0] // heads
    B = block
    smem = pl.BlockSpec(memory_space=pltpu.SMEM)
    return pl.pallas_call(
        functools.partial(_attn_kernel, lambda_init=lambda_init),
        out_shape=jax.ShapeDtypeStruct((T, heads * dv), BF16),
        grid=(heads, T // B),
        in_specs=[smem,
                  pl.BlockSpec(lam.shape, lambda h, i: (0, 0)),
                  pl.BlockSpec(g.shape, lambda h, i: (0, 0)),
                  pl.BlockSpec((hw, B), lambda h, i: (h, i)),
                  pl.BlockSpec((None, T, hw), lambda h, i: (h, 0, 0)),
                  pl.BlockSpec((dv, T), lambda h, i: (h, 0))],
        out_specs=pl.BlockSpec((B, dv), lambda h, i: (i, h)),
        scratch_shapes=[pltpu.VMEM((2, hw, B), BF16), pltpu.VMEM((2, B, B), F32),
                        pltpu.VMEM((2, 1, B), F32), pltpu.VMEM((2, dv + ONES_ROWS, B), F32)],
        compiler_params=_params(("arbitrary", "arbitrary")),
        name="diff_attention",
    )(tbl, lam, g, qT, k, vT)


def kernel(x, a_w_in, a_w_out, a_gnorm, a_lb_logits, b_w_q, b_w_o, b_lam_q1, b_lam_k1, b_lam_q2, b_lam_k2,
           b_subln, kv_norm, kv_w, rel_table, norm_mix, norm_ffn, ffn_w_up, ffn_conv_w, ffn_conv_b,
           ffn_w_down, final_norm):
    batch, T, D = x.shape
    depth = norm_mix.shape[0]
    n_a = a_w_in.shape[0]
    qdim = b_w_q.shape[2]
    row = lambda v: v.reshape(1, -1).astype(F32)
    tm = min(512, T)
    chunk = min(HG_CHUNK, T)
    block = min(ATT_BLOCK, T)
    tbl = rel_table.astype(F32).T.reshape(-1)

    outs = []
    for bi in range(batch):
        xs = x[bi]
        qT = k3 = vT = None
        for li in range(depth):
            if li == n_a:
                j = 0
                qT, k3, vT = _qkv(xs, row(kv_norm), row(norm_mix[li]), kv_w[:, :qdim].astype(BF16),
                                  kv_w[:, qdim:].astype(BF16), b_w_q[j].astype(BF16), DA_HEADS,
                                  DA_HEAD_DIM ** -0.5 * LOG2E, tm)
            if li < n_a:
                q, k, lf, v, gs = _hg_in(xs, row(norm_mix[li]), a_w_in[li].astype(BF16),
                                         a_lb_logits.astype(F32), li, tm)
                o = _hg_rec(q, k, lf, v, HG_HEADS, chunk)
                xs = _hg_out(xs, o, gs, row(a_gnorm[li]), a_w_out[li].astype(BF16), tm)
            else:
                j = li - n_a
                if j > 0:
                    qT, _, _ = _qkv(xs, row(kv_norm), row(norm_mix[li]), kv_w[:, :qdim].astype(BF16),
                                    kv_w[:, qdim:].astype(BF16), b_w_q[j].astype(BF16), DA_HEADS,
                                    DA_HEAD_DIM ** -0.5 * LOG2E, tm)
                lambda_init = 0.8 - 0.6 * math.exp(-0.3 * li)
                lam = jnp.stack([b_lam_q1[j], b_lam_k1[j], b_lam_q2[j], b_lam_k2[j]]).astype(F32)
                att = _attention(tbl, lam, b_subln[j].reshape(-1, 1).astype(F32), qT, k3, vT, lambda_init, block)
                xs = _attn_out(xs, att, b_w_o[j].astype(BF16), tm)
            last = li == depth - 1
            xs = _ffn(xs, row(norm_ffn[li]), ffn_w_up[li].astype(BF16), ffn_conv_w[li].astype(F32),
                      row(ffn_conv_b[li]), ffn_w_down[li].astype(BF16), row(final_norm), last, min(256, T))
        outs.append(xs)
    return jnp.stack(outs).astype(x.dtype)
```

```python
import functools
import math

import jax
import jax.numpy as jnp
from jax import lax
from jax.experimental import pallas as pl
from jax.experimental.pallas import tpu as pltpu

F32 = jnp.float32
BF16 = jnp.bfloat16
EPS = 1e-6

LANES = 128
SUBLANES = 8
VMEM_LIMIT_BYTES = 56 * 2**20

HG_HEADS = 8
DA_HEADS = 8
DA_HEAD_DIM = 64
REL_BUCKETS = 32
REL_MAX_DIST = 128
CONV_W = 3
NEG = -1e30
LOG2E = math.log2(math.e)
ONES_ROWS = 2 * SUBLANES

HG_CHUNK = 128
ATT_BLOCK = 512
FAR_GROUP = 4
STALE_MAX_MARGIN = 64.0


def _t5_bucket_lower_bounds():
    max_exact = REL_BUCKETS // 2
    lo = list(range(max_exact))
    bucket_of = lambda n: min(
        max_exact + int(math.log(n / max_exact) / math.log(REL_MAX_DIST / max_exact) * (REL_BUCKETS - max_exact)),
        REL_BUCKETS - 1)
    n = max_exact
    for b in range(max_exact, REL_BUCKETS):
        while bucket_of(n) < b:
            n += 1
        lo.append(n)
    return tuple(lo)


T5_BUCKET_LO = _t5_bucket_lower_bounds()


def _dot(a, b):
    return jnp.dot(a, b, preferred_element_type=F32)


def _dot_nt(a, b):
    return lax.dot_general(a, b, (((1,), (1,)), ((), ())), preferred_element_type=F32)


def _dot_tn(a, b):
    return lax.dot_general(a, b, (((0,), (0,)), ((), ())), preferred_element_type=F32)


def _rms_inv(x):
    return lax.rsqrt(jnp.mean(x * x, axis=-1, keepdims=True) + EPS)


def _resident(shape):
    nd = len(shape)
    return pl.BlockSpec(shape, lambda *_: (0,) * nd, pipeline_mode=pl.Buffered(1))


def _params(sem):
    return pltpu.CompilerParams(dimension_semantics=sem, vmem_limit_bytes=VMEM_LIMIT_BYTES)


def _hg_in_kernel(x_ref, g_ref, w_ref, lbl_ref, q_ref, k_ref, lf_ref, v_ref, gs_ref, *, layer):
    fd = q_ref.shape[1]
    x = x_ref[...]
    h = (x * _rms_inv(x) * g_ref[...]).astype(BF16)
    lg = lbl_ref[...]
    ex = jnp.exp(lg - jnp.max(lg, axis=0, keepdims=True))
    sm = ex / jnp.sum(ex, axis=0, keepdims=True)
    lb = jnp.sum(sm[:layer + 1], axis=0, keepdims=True)
    log_lb = jnp.log(lb)
    log_1m_lb = jnp.log1p(-lb)

    zq = _dot(h, w_ref[:, 0:fd])
    q_ref[...] = (zq * jax.nn.sigmoid(zq)).astype(q_ref.dtype)
    zf = _dot(h, w_ref[:, fd:2 * fd])
    c = log_1m_lb + jax.nn.log_sigmoid(zf)
    lf_ref[...] = jnp.logaddexp(log_lb, c)
    k_ref[...] = jnp.exp(c - zf).astype(k_ref.dtype)
    zi = _dot(h, w_ref[:, 2 * fd:3 * fd])
    v_ref[...] = zi.astype(v_ref.dtype)
    zg = _dot(h, w_ref[:, 3 * fd:])
    gs_ref[...] = (zg * jax.nn.sigmoid(zg)).astype(gs_ref.dtype)


def _hg_in(x, g, w, lb_logits, layer, tm):
    T, D = x.shape
    fd = lb_logits.shape[1]
    dvt = w.shape[1] - 3 * fd
    row = lambda n: pl.BlockSpec((tm, n), lambda i: (i, 0))
    return pl.pallas_call(
        functools.partial(_hg_in_kernel, layer=layer),
        out_shape=(jax.ShapeDtypeStruct((T, fd), BF16), jax.ShapeDtypeStruct((T, fd), BF16),
                   jax.ShapeDtypeStruct((T, fd), F32), jax.ShapeDtypeStruct((T, dvt), BF16),
                   jax.ShapeDtypeStruct((T, dvt), BF16)),
        grid=(T // tm,),
        in_specs=[row(D), _resident((1, D)), _resident(w.shape), _resident(lb_logits.shape)],
        out_specs=(row(fd), row(fd), row(fd), row(dvt), row(dvt)),
        compiler_params=_params(("parallel",)),
        name="hgrn2_in_proj",
    )(x, g, w, lb_logits)


def _level_ref(bh, w):
    C, d = bh.shape
    n = 2 * w
    if w >= SUBLANES:
        return jnp.concatenate(
            [jnp.broadcast_to(bh[i * n + w - 1:i * n + w, :], (n, d)) for i in range(C // n)], axis=0)
    b3 = bh.reshape(C // SUBLANES, SUBLANES, d)
    pick = lambda s: jnp.broadcast_to(b3[:, s:s + 1, :], b3.shape)
    sub = lax.broadcasted_iota(jnp.int32, b3.shape, 1)
    r = pick(SUBLANES - n + w - 1)
    for s in range(SUBLANES - 2 * n, -1, -n):
        r = jnp.where(sub < s + n, pick(s + w - 1), r)
    return r.reshape(C, d)


def _hg_rec_kernel(q_ref, k_ref, lf_ref, v_ref, o_ref, st_ref, b_ref, *, heads):
    C, fd = q_ref.shape
    dk = fd // heads
    dv = v_ref.shape[1] // heads

    @pl.when(pl.program_id(0) == 0)
    def _zero_state():
        st_ref[...] = jnp.zeros_like(st_ref)

    row = lax.broadcasted_iota(jnp.int32, (C, C), 0)
    col = lax.broadcasted_iota(jnp.int32, (C, C), 1)

    tri = jnp.where(row >= col, 1.0, 0.0).astype(BF16)
    lf = lf_ref[...]
    p0 = lf.astype(BF16)
    r1 = lf - p0.astype(F32)
    p1 = r1.astype(BF16)
    p2 = (r1 - p1.astype(F32)).astype(BF16)
    b_ref[...] = _dot(tri, p0) + _dot(tri, p1) + _dot(tri, p2)

    levels = [1 << i for i in range(int(math.log2(C)))]
    xor = row ^ col
    below = row > col
    level_mask = [below & ((xor >> int(math.log2(w))) == 1) for w in levels]
    diag = row == col

    for h in range(heads):
        sk = slice(h * dk, (h + 1) * dk)
        sv = slice(h * dv, (h + 1) * dv)
        bh = b_ref[:, sk]
        qh = q_ref[:, sk].astype(F32)
        kh = k_ref[:, sk].astype(F32)
        vh = v_ref[:, sv]
        b_last = bh[C - 1:C, :]

        a = jnp.where(diag, _dot_nt(q_ref[:, sk], k_ref[:, sk]), 0.0)
        for w, mk in zip(levels, level_mask):
            e = jnp.exp(-jnp.abs(bh - _level_ref(bh, w)))
            a = a + jnp.where(mk, _dot_nt((qh * e).astype(BF16), (kh * e).astype(BF16)), 0.0)
        o = _dot(a.astype(BF16), vh)

        st = st_ref[h]
        o = o + _dot_nt((qh * jnp.exp(bh)).astype(BF16), st.astype(BF16))
        k_dec = (kh * jnp.exp(b_last - bh)).astype(BF16)
        st_ref[h] = st * jnp.exp(b_last) + _dot_tn(vh, k_dec)
        o_ref[:, sv] = o.astype(o_ref.dtype)


def _hg_rec(q, k, lf, v, heads, chunk):
    T, fd = q.shape
    dvt = v.shape[1]
    row = lambda n: pl.BlockSpec((chunk, n), lambda i: (i, 0))
    return pl.pallas_call(
        functools.partial(_hg_rec_kernel, heads=heads),
        out_shape=jax.ShapeDtypeStruct((T, dvt), BF16),
        grid=(T // chunk,),
        in_specs=[row(fd), row(fd), row(fd), row(dvt)],
        out_specs=row(dvt),
        scratch_shapes=[pltpu.VMEM((heads, dvt // heads, fd // heads), F32), pltpu.VMEM((chunk, fd), F32)],
        compiler_params=_params(("arbitrary",)),
        name="hgrn2_recurrence",
    )(q, k, lf, v)


def _hg_out_kernel(x_ref, o_ref, gs_ref, gn_ref, w_ref, y_ref):
    o = o_ref[...].astype(F32)
    n = o * _rms_inv(o) * gn_ref[...] * gs_ref[...].astype(F32)
    y_ref[...] = x_ref[...] + _dot(n.astype(BF16), w_ref[...])


def _hg_out(x, o, gs, gn, w, tm):
    T, D = x.shape
    row = lambda n: pl.BlockSpec((tm, n), lambda i: (i, 0))
    return pl.pallas_call(
        _hg_out_kernel,
        out_shape=jax.ShapeDtypeStruct((T, D), F32),
        grid=(T // tm,),
        in_specs=[row(D), row(o.shape[1]), row(gs.shape[1]), _resident(gn.shape), _resident(w.shape)],
        out_specs=row(D),
        compiler_params=_params(("parallel",)),
        name="hgrn2_out_proj",
    )(x, o, gs, gn, w)


def _attn_out_kernel(x_ref, a_ref, w_ref, y_ref):
    y_ref[...] = x_ref[...] + _dot(a_ref[...], w_ref[...])


def _attn_out(x, a, w, tm):
    T, D = x.shape
    row = lambda n: pl.BlockSpec((tm, n), lambda i: (i, 0))
    return pl.pallas_call(
        _attn_out_kernel,
        out_shape=jax.ShapeDtypeStruct((T, D), F32),
        grid=(T // tm,),
        in_specs=[row(D), row(a.shape[1]), _resident(w.shape)],
        out_specs=row(D),
        compiler_params=_params(("parallel",)),
        name="attn_out_proj",
    )(x, a, w)


def _ffn_kernel(x_ref, g_ref, wup_ref, cw_ref, cb_ref, wdn_ref, fg_ref, y_ref, u_ref, act_ref, *, final_norm):
    tm = x_ref.shape[0]
    ff = wdn_ref.shape[0]
    halo = SUBLANES

    @pl.when(pl.program_id(0) == 0)
    def _zero_halo():
        u_ref[0:halo, :] = jnp.zeros((halo, u_ref.shape[1]), F32)

    x = x_ref[...]
    h = (x * _rms_inv(x) * g_ref[...]).astype(BF16)
    up_cols = 512
    for c0 in range(0, 2 * ff, up_cols):
        u_ref[halo:halo + tm, c0:c0 + up_cols] = _dot(h, wup_ref[:, c0:c0 + up_cols])

    def conv(cs):
        acc = cb_ref[:, cs] + cw_ref[CONV_W - 1:CONV_W, cs] * u_ref[halo:halo + tm, cs]
        for j in range(CONV_W - 1):
            off = halo - (CONV_W - 1) + j
            acc = acc + cw_ref[j:j + 1, cs] * u_ref[off:off + tm, cs]
        return acc

    act_cols = 256
    for c0 in range(0, ff, act_cols):
        gate = conv(slice(c0, c0 + act_cols))
        val = conv(slice(ff + c0, ff + c0 + act_cols))
        act_ref[:, c0:c0 + act_cols] = (gate * jax.nn.sigmoid(gate) * val).astype(BF16)

    u_ref[0:halo, :] = u_ref[tm:tm + halo, :]
    y = x + _dot(act_ref[...], wdn_ref[...])
    if final_norm:
        y = y * _rms_inv(y) * fg_ref[...]
    y_ref[...] = y


def _ffn(x, g, wup, cw, cb, wdn, fg, final_norm, tm):
    T, D = x.shape
    ff = wdn.shape[0]
    row = pl.BlockSpec((tm, D), lambda i: (i, 0))
    return pl.pallas_call(
        functools.partial(_ffn_kernel, final_norm=final_norm),
        out_shape=jax.ShapeDtypeStruct((T, D), F32),
        grid=(T // tm,),
        in_specs=[row, _resident(g.shape), _resident(wup.shape), _resident(cw.shape), _resident(cb.shape),
                  _resident(wdn.shape), _resident(fg.shape)],
        out_specs=row,
        scratch_shapes=[pltpu.VMEM((tm + SUBLANES, 2 * ff), F32), pltpu.VMEM((tm, ff), BF16)],
        compiler_params=_params(("arbitrary",)),
        name="conv_ffn",
    )(x, g, wup, cw, cb, wdn, fg)


def _qkv_kernel(x_ref, gkv_ref, gq_ref, wk_ref, wv_ref, wq_ref, qT_ref, k_ref, vT_ref, *, scale):
    heads = k_ref.shape[0]
    hd = k_ref.shape[2]
    x = x_ref[...]
    xn = x * _rms_inv(x)
    hkv = (xn * gkv_ref[...]).astype(BF16)
    hq = (xn * gq_ref[...]).astype(BF16)
    k = _dot(hkv, wk_ref[...]).astype(BF16)
    for h in range(heads):
        k_ref[h] = k[:, h * hd:(h + 1) * hd]
    vT_ref[...] = _dot(hkv, wv_ref[...]).T.astype(BF16)
    qT_ref[...] = (_dot(hq, wq_ref[...]) * scale).T.astype(BF16)


def _qkv(x, gkv, gq, wk, wv, wq, heads, scale, tm):
    T, D = x.shape
    hd = wk.shape[1] // heads
    colblk = lambda n: pl.BlockSpec((n, tm), lambda i: (0, i))
    return pl.pallas_call(
        functools.partial(_qkv_kernel, scale=scale),
        out_shape=(jax.ShapeDtypeStruct((wq.shape[1], T), BF16), jax.ShapeDtypeStruct((heads, T, hd), BF16),
                   jax.ShapeDtypeStruct((wv.shape[1], T), BF16)),
        grid=(T // tm,),
        in_specs=[pl.BlockSpec((tm, D), lambda i: (i, 0)), _resident(gkv.shape), _resident(gq.shape),
                  _resident(wk.shape), _resident(wv.shape), _resident(wq.shape)],
        out_specs=(colblk(wq.shape[1]), pl.BlockSpec((heads, tm, hd), lambda i: (0, i, 0)), colblk(wv.shape[1])),
        compiler_params=_params(("parallel",)),
        name="qkv_proj",
    )(x, gkv, gq, wk, wv, wq)


def _attn_kernel(tbl_ref, lam_ref, g_ref, qT_ref, k_ref, vT_ref, o_ref,
                 qp_ref, dbias_ref, pbias_ref, m_ref, acc_ref, *, lambda_init):
    hw, B = qT_ref.shape
    hd = hw // 2
    dv = vT_ref.shape[0]
    h = pl.program_id(0)
    qi = pl.program_id(1)
    nsub = B // LANES

    @pl.when(qi == 0)
    def _build_bias_tiles():
        far = tbl_ref[h * REL_BUCKETS + REL_BUCKETS - 1]
        a = lax.broadcasted_iota(jnp.int32, (LANES, LANES), 0)
        b = lax.broadcasted_iota(jnp.int32, (LANES, LANES), 1)

        def rel_bias(n):
            val = jnp.zeros(n.shape, F32)
            for bucket in range(REL_BUCKETS - 2, -1, -1):
                val = jnp.where(n < T5_BUCKET_LO[bucket + 1], (tbl_ref[h * REL_BUCKETS + bucket] - far) * LOG2E, val)
            return val

        e0 = jnp.where(b >= a, rel_bias(b - a), NEG)
        e1 = rel_bias(LANES + b - a)
        zero = jnp.zeros((LANES, LANES), F32)
        masked = jnp.full((LANES, LANES), NEG, F32)
        for r in range(nsub):
            for c in range(nsub):
                rs, cs = slice(r * LANES, (r + 1) * LANES), slice(c * LANES, (c + 1) * LANES)
                dbias_ref[rs, cs] = masked if c < r else e0 if c == r else e1 if c == r + 1 else zero
        for c in range(nsub):
            pbias_ref[:, c * LANES:(c + 1) * LANES] = e1 if c == 0 else zero

    qT = qT_ref[...].astype(F32)
    half = lax.broadcasted_iota(jnp.int32, qT.shape, 0) < hd
    qp_ref[0] = jnp.where(half, qT, 0.0).astype(BF16)
    qp_ref[1] = jnp.where(half, 0.0, qT).astype(BF16)
    m_ref[...] = jnp.full(m_ref.shape, NEG, F32)
    acc_ref[...] = jnp.zeros(acc_ref.shape, F32)
    ones = jnp.ones((ONES_ROWS, B), BF16)

    def scores(j, kind, c):
        st = pl.multiple_of(j * B, B)
        if kind == "prev":
            cut = B - LANES
            last = pl.multiple_of(st + cut, LANES)
            return jnp.concatenate([_dot(k_ref[pl.ds(st, cut), :], qp_ref[c]),
                                    _dot(k_ref[pl.ds(last, LANES), :], qp_ref[c]) + pbias_ref[...]], axis=0)
        s = _dot(k_ref[pl.ds(st, B), :], qp_ref[c])
        return s + dbias_ref[...] if kind == "diag" else s

    def update(j, sc):
        st = pl.multiple_of(j * B, B)
        vt = jnp.concatenate([vT_ref[:, pl.ds(st, B)], ones], axis=0)
        for c in range(2):
            m_old = m_ref[c]
            m_new = jnp.maximum(m_old, jnp.max(sc[c], axis=0, keepdims=True))
            p = jnp.exp2(sc[c] - m_new).astype(BF16)
            acc_ref[c] = jnp.exp2(m_old - m_new) * acc_ref[c] + _dot(vt, p)
            m_ref[c] = m_new

    def process(tiles):
        qk = lambda t: [scores(tiles[t][0], tiles[t][1], c) for c in range(2)]
        nxt = qk(0)
        for t in range(len(tiles)):
            cur, nxt = nxt, (qk(t + 1) if t + 1 < len(tiles) else None)
            update(tiles[t][0], cur)

    def process_far(tiles):
        qk = lambda t: [scores(tiles[t][0], "far", c) for c in range(2)]
        ref = [m_ref[c] for c in range(2)]
        top = [None, None]
        pv = [None, None]
        nxt = qk(0)
        for t in range(len(tiles)):
            cur, nxt = nxt, (qk(t + 1) if t + 1 < len(tiles) else None)
            st = pl.multiple_of(tiles[t][0] * B, B)
            vt = jnp.concatenate([vT_ref[:, pl.ds(st, B)], ones], axis=0)
            for c in range(2):
                cm = jnp.max(cur[c], axis=0, keepdims=True)
                top[c] = cm if top[c] is None else jnp.maximum(top[c], cm)
                d = _dot(vt, jnp.exp2(cur[c] - ref[c]).astype(BF16))
                pv[c] = d if pv[c] is None else pv[c] + d
        excess = jnp.max(jnp.maximum(top[0] - ref[0], top[1] - ref[1]))
        in_range = excess <= STALE_MAX_MARGIN

        @pl.when(in_range)
        def _commit():
            for c in range(2):
                acc_ref[c] = acc_ref[c] + pv[c]

        @pl.when(jnp.logical_not(in_range))
        def _redo():
            process(tiles)

    @pl.when(qi >= 1)
    def _previous_and_diagonal():
        process([(qi - 1, "prev"), (qi, "diag")])

    @pl.when(qi == 0)
    def _diagonal_only():
        process([(qi, "diag")])

    n_far = jnp.maximum(qi - 1, 0)
    far = lambda first, n: [(first + t, "far") for t in range(n)]

    def far_group(g, carry):
        process_far(far(FAR_GROUP * g, FAR_GROUP))
        return carry

    n_groups = n_far // FAR_GROUP
    lax.fori_loop(0, n_groups, far_group, 0)
    done = n_groups * FAR_GROUP
    size = FAR_GROUP // 2
    while size >= 1:
        @pl.when(((n_far - done) & size) != 0)
        def _leftover(size=size):
            process_far(far(done + ((n_far - done) & ~(2 * size - 1)), size))
        size //= 2

    lam = (jnp.exp(jnp.sum(lam_ref[0:1, :] * lam_ref[1:2, :], axis=-1, keepdims=True))
           - jnp.exp(jnp.sum(lam_ref[2:3, :] * lam_ref[3:4, :], axis=-1, keepdims=True)) + lambda_init)
    o1 = acc_ref[0, 0:dv, :] / acc_ref[0, dv:dv + 1, :]
    o2 = acc_ref[1, 0:dv, :] / acc_ref[1, dv:dv + 1, :]
    oT = o1 - lam * o2
    inv = lax.rsqrt(jnp.mean(oT * oT, axis=0, keepdims=True) + EPS)
    y = oT * inv * g_ref[...] * (1.0 - lambda_init)
    o_ref[...] = y.T.astype(o_ref.dtype)


def _attention(tbl, lam, g, qT, k, vT, lambda_init, block):
    heads, T, hw = k.shape
    dv = vT.shape[0] // heads
    B = block
    smem = pl.BlockSpec(memory_space=pltpu.SMEM)
    return pl.pallas_call(
        functools.partial(_attn_kernel, lambda_init=lambda_init),
        out_shape=jax.ShapeDtypeStruct((T, heads * dv), BF16),
        grid=(heads, T // B),
        in_specs=[smem,
                  pl.BlockSpec(lam.shape, lambda h, i: (0, 0)),
                  pl.BlockSpec(g.shape, lambda h, i: (0, 0)),
                  pl.BlockSpec((hw, B), lambda h, i: (h, i)),
                  pl.BlockSpec((None, T, hw), lambda h, i: (h, 0, 0)),
                  pl.BlockSpec((dv, T), lambda h, i: (h, 0))],
        out_specs=pl.BlockSpec((B, dv), lambda h, i: (i, h)),
        scratch_shapes=[pltpu.VMEM((2, hw, B), BF16), pltpu.VMEM((B, B), F32), pltpu.VMEM((LANES, B), F32),
                        pltpu.VMEM((2, 1, B), F32), pltpu.VMEM((2, dv + ONES_ROWS, B), F32)],
        compiler_params=_params(("arbitrary", "arbitrary")),
        name="diff_attention",
    )(tbl, lam, g, qT, k, vT)


def kernel(x, a_w_in, a_w_out, a_gnorm, a_lb_logits, b_w_q, b_w_o, b_lam_q1, b_lam_k1, b_lam_q2, b_lam_k2,
           b_subln, kv_norm, kv_w, rel_table, norm_mix, norm_ffn, ffn_w_up, ffn_conv_w, ffn_conv_b,
           ffn_w_down, final_norm):
    batch, T, D = x.shape
    depth = norm_mix.shape[0]
    n_a = a_w_in.shape[0]
    qdim = b_w_q.shape[2]
    row = lambda v: v.reshape(1, -1).astype(F32)
    tm = min(512, T)
    chunk = min(HG_CHUNK, T)
    block = min(ATT_BLOCK, T)
    tbl = rel_table.astype(F32).T.reshape(-1)

    outs = []
    for bi in range(batch):
        xs = x[bi]
        qT = k3 = vT = None
        for li in range(depth):
            if li == n_a:
                j = 0
                qT, k3, vT = _qkv(xs, row(kv_norm), row(norm_mix[li]), kv_w[:, :qdim].astype(BF16),
                                  kv_w[:, qdim:].astype(BF16), b_w_q[j].astype(BF16), DA_HEADS,
                                  DA_HEAD_DIM ** -0.5 * LOG2E, tm)
            if li < n_a:
                q, k, lf, v, gs = _hg_in(xs, row(norm_mix[li]), a_w_in[li].astype(BF16),
                                         a_lb_logits.astype(F32), li, tm)
                o = _hg_rec(q, k, lf, v, HG_HEADS, chunk)
                xs = _hg_out(xs, o, gs, row(a_gnorm[li]), a_w_out[li].astype(BF16), tm)
            else:
                j = li - n_a
                if j > 0:
                    qT, _, _ = _qkv(xs, row(kv_norm), row(norm_mix[li]), kv_w[:, :qdim].astype(BF16),
                                    kv_w[:, qdim:].astype(BF16), b_w_q[j].astype(BF16), DA_HEADS,
                                    DA_HEAD_DIM ** -0.5 * LOG2E, tm)
                lambda_init = 0.8 - 0.6 * math.exp(-0.3 * li)
                lam = jnp.stack([b_lam_q1[j], b_lam_k1[j], b_lam_q2[j], b_lam_k2[j]]).astype(F32)
                att = _attention(tbl, lam, b_subln[j].reshape(-1, 1).astype(F32), qT, k3, vT, lambda_init, block)
                xs = _attn_out(xs, att, b_w_o[j].astype(BF16), tm)
            last = li == depth - 1
            xs = _ffn(xs, row(norm_ffn[li]), ffn_w_up[li].astype(BF16), ffn_conv_w[li].astype(F32),
                      row(ffn_conv_b[li]), ffn_w_down[li].astype(BF16), row(final_norm), last, min(256, T))
        outs.append(xs)
    return jnp.stack(outs).astype(x.dtype)
```

```python
import functools
import math

import jax
import jax.numpy as jnp
from jax import lax
from jax.experimental import pallas as pl
from jax.experimental.pallas import tpu as pltpu

F32 = jnp.float32
BF16 = jnp.bfloat16
EPS = 1e-6

LANES = 128
SUBLANES = 8
MXU_COLS = 256
VMEM_LIMIT_BYTES = 56 * 2**20

HG_HEADS = 8
DA_HEADS = 8
DA_HEAD_DIM = 64
REL_BUCKETS = 32
REL_MAX_DIST = 128
CONV_W = 3
NEG = -1e30
LOG2E = math.log2(math.e)
ONES_ROWS = 2 * SUBLANES

HG_CHUNK = 128
ATT_BLOCK = 512
FAR_GROUP = 4
STALE_MAX_MARGIN = 64.0


def _t5_bucket_lower_bounds():
    max_exact = REL_BUCKETS // 2
    lo = list(range(max_exact))
    bucket_of = lambda n: min(
        max_exact + int(math.log(n / max_exact) / math.log(REL_MAX_DIST / max_exact) * (REL_BUCKETS - max_exact)),
        REL_BUCKETS - 1)
    n = max_exact
    for b in range(max_exact, REL_BUCKETS):
        while bucket_of(n) < b:
            n += 1
        lo.append(n)
    return tuple(lo)


T5_BUCKET_LO = _t5_bucket_lower_bounds()


def _dot(a, b):
    return jnp.dot(a, b, preferred_element_type=F32)


def _dot_nt(a, b):
    return lax.dot_general(a, b, (((1,), (1,)), ((), ())), preferred_element_type=F32)


def _dot_tn(a, b):
    return lax.dot_general(a, b, (((0,), (0,)), ((), ())), preferred_element_type=F32)


def _neg_abs(x):
    bits = pltpu.bitcast(x, jnp.uint32) | jnp.uint32(0x80000000)
    return pltpu.bitcast(bits, F32)


def _rms_inv(x):
    return lax.rsqrt(jnp.mean(x * x, axis=-1, keepdims=True) + EPS)


def _resident(shape):
    nd = len(shape)
    return pl.BlockSpec(shape, lambda *_: (0,) * nd, pipeline_mode=pl.Buffered(1))


def _params(sem):
    return pltpu.CompilerParams(dimension_semantics=sem, vmem_limit_bytes=VMEM_LIMIT_BYTES)


def _hg_in_kernel(x_ref, g_ref, w_ref, lbl_ref, q_ref, k_ref, lf_ref, v_ref, gs_ref, *, layer):
    fd = q_ref.shape[1]
    x = x_ref[...]
    h = (x * _rms_inv(x) * g_ref[...]).astype(BF16)
    lg = lbl_ref[...]
    ex = jnp.exp(lg - jnp.max(lg, axis=0, keepdims=True))
    sm = ex / jnp.sum(ex, axis=0, keepdims=True)
    lb = jnp.sum(sm[:layer + 1], axis=0, keepdims=True)
    log_lb = jnp.log(lb) * LOG2E
    log_1m_lb = jnp.log1p(-lb) * LOG2E

    def softplus_neg_abs(d):
        return jnp.log2(1.0 + jnp.exp2(_neg_abs(d)))

    def silu(z):
        return z / (1.0 + jnp.exp(-z))

    assert v_ref.shape[1] == fd and gs_ref.shape[1] == fd
    for c0 in range(0, fd, MXU_COLS):
        cs = slice(c0, c0 + MXU_COLS)
        col = lambda part: w_ref[:, part * fd + c0:part * fd + c0 + MXU_COLS]
        q_ref[:, cs] = silu(_dot(h, col(0))).astype(q_ref.dtype)
        zf = _dot(h, col(1)) * LOG2E
        v_ref[:, cs] = _dot(h, col(2)).astype(v_ref.dtype)
        c = log_1m_lb[:, cs] + jnp.minimum(zf, 0.0) - softplus_neg_abs(zf)
        a = log_lb[:, cs]
        lf_ref[:, cs] = jnp.maximum(a, c) + softplus_neg_abs(a - c)
        k_ref[:, cs] = jnp.exp2(c - zf).astype(k_ref.dtype)
        gs_ref[:, cs] = silu(_dot(h, col(3))).astype(gs_ref.dtype)


def _hg_in(x, g, w, lb_logits, layer, tm):
    T, D = x.shape
    fd = lb_logits.shape[1]
    dvt = w.shape[1] - 3 * fd
    row = lambda n: pl.BlockSpec((tm, n), lambda i: (i, 0))
    return pl.pallas_call(
        functools.partial(_hg_in_kernel, layer=layer),
        out_shape=(jax.ShapeDtypeStruct((T, fd), BF16), jax.ShapeDtypeStruct((T, fd), BF16),
                   jax.ShapeDtypeStruct((T, fd), F32), jax.ShapeDtypeStruct((T, dvt), BF16),
                   jax.ShapeDtypeStruct((T, dvt), BF16)),
        grid=(T // tm,),
        in_specs=[row(D), _resident((1, D)), _resident(w.shape), _resident(lb_logits.shape)],
        out_specs=(row(fd), row(fd), row(fd), row(dvt), row(dvt)),
        compiler_params=_params(("parallel",)),
        name="hgrn2_in_proj",
    )(x, g, w, lb_logits)


def _level_ref(bh, w):
    C, d = bh.shape
    n = 2 * w
    if w >= SUBLANES:
        return jnp.concatenate(
            [jnp.broadcast_to(bh[i * n + w - 1:i * n + w, :], (n, d)) for i in range(C // n)], axis=0)
    b3 = bh.reshape(C // SUBLANES, SUBLANES, d)
    pick = lambda s: jnp.broadcast_to(b3[:, s:s + 1, :], b3.shape)
    sub = lax.broadcasted_iota(jnp.int32, b3.shape, 1)
    r = pick(SUBLANES - n + w - 1)
    for s in range(SUBLANES - 2 * n, -1, -n):
        r = jnp.where(sub < s + n, pick(s + w - 1), r)
    return r.reshape(C, d)


def _hg_rec_kernel(q_ref, k_ref, lf_ref, v_ref, o_ref, st_ref, b_ref, *, heads):
    C, fd = q_ref.shape
    dk = fd // heads
    dv = v_ref.shape[1] // heads

    @pl.when(pl.program_id(0) == 0)
    def _zero_state():
        st_ref[...] = jnp.zeros_like(st_ref)

    row = lax.broadcasted_iota(jnp.int32, (C, C), 0)
    col = lax.broadcasted_iota(jnp.int32, (C, C), 1)

    tri = jnp.where(row >= col, 1.0, 0.0).astype(BF16)
    lf = lf_ref[...]
    p0 = lf.astype(BF16)
    r1 = lf - p0.astype(F32)
    p1 = r1.astype(BF16)
    p2 = (r1 - p1.astype(F32)).astype(BF16)
    b_ref[...] = _dot(tri, p0) + _dot(tri, p1) + _dot(tri, p2)

    levels = [1 << i for i in range(int(math.log2(C)))]
    xor = row ^ col
    below = row > col
    level_mask = [below & ((xor >> int(math.log2(w))) == 1) for w in levels]
    diag = row == col

    for h in range(heads):
        sk = slice(h * dk, (h + 1) * dk)
        sv = slice(h * dv, (h + 1) * dv)
        bh = b_ref[:, sk]
        qh = q_ref[:, sk]
        kh = k_ref[:, sk]
        vh = v_ref[:, sv]
        b_last = bh[C - 1:C, :]
        decay = lambda expo: jnp.exp2(expo).astype(BF16)

        a = jnp.where(diag, _dot_nt(qh, kh), 0.0)
        for w, mk in zip(levels, level_mask):
            e = decay(_neg_abs(bh - _level_ref(bh, w)))
            a = jnp.where(mk, _dot_nt(qh * e, kh * e), a)
        o = _dot(a.astype(BF16), vh)

        st = st_ref[h]
        o = o + _dot_nt(qh * decay(bh), st.astype(BF16))
        st_ref[h] = st * jnp.exp2(b_last) + _dot_tn(vh, kh * decay(b_last - bh))
        o_ref[:, sv] = o.astype(o_ref.dtype)


def _hg_rec(q, k, lf, v, heads, chunk):
    T, fd = q.shape
    dvt = v.shape[1]
    row = lambda n: pl.BlockSpec((chunk, n), lambda i: (i, 0))
    return pl.pallas_call(
        functools.partial(_hg_rec_kernel, heads=heads),
        out_shape=jax.ShapeDtypeStruct((T, dvt), BF16),
        grid=(T // chunk,),
        in_specs=[row(fd), row(fd), row(fd), row(dvt)],
        out_specs=row(dvt),
        scratch_shapes=[pltpu.VMEM((heads, dvt // heads, fd // heads), F32), pltpu.VMEM((chunk, fd), F32)],
        compiler_params=_params(("arbitrary",)),
        name="hgrn2_recurrence",
    )(q, k, lf, v)


def _ffn_kernel(x_ref, a_ref, *refs, gated, final_norm):
    if gated:
        gs_ref, gn_ref, *refs = refs
    wpre_ref, g_ref, wup_ref, cw_ref, cb_ref, wdn_ref, fg_ref, y_ref, u_ref, act_ref = refs
    tm = x_ref.shape[0]
    ff = wdn_ref.shape[0]
    halo = SUBLANES

    @pl.when(pl.program_id(0) == 0)
    def _zero_halo():
        u_ref[0:halo, :] = jnp.zeros((halo, u_ref.shape[1]), F32)

    a = a_ref[...]
    if gated:
        o = a.astype(F32)
        a = (o * _rms_inv(o) * gn_ref[...] * gs_ref[...].astype(F32)).astype(BF16)
    x = x_ref[...] + _dot(a, wpre_ref[...])
    h = (x * _rms_inv(x) * g_ref[...]).astype(BF16)
    up_cols = 512
    for c0 in range(0, 2 * ff, up_cols):
        u_ref[halo:halo + tm, c0:c0 + up_cols] = _dot(h, wup_ref[:, c0:c0 + up_cols])

    def conv(cs):
        acc = cb_ref[:, cs] + cw_ref[CONV_W - 1:CONV_W, cs] * u_ref[halo:halo + tm, cs]
        for j in range(CONV_W - 1):
            off = halo - (CONV_W - 1) + j
            acc = acc + cw_ref[j:j + 1, cs] * u_ref[off:off + tm, cs]
        return acc

    act_cols = 256
    for c0 in range(0, ff, act_cols):
        gate = conv(slice(c0, c0 + act_cols))
        val = conv(slice(ff + c0, ff + c0 + act_cols))
        act_ref[:, c0:c0 + act_cols] = (gate * jax.nn.sigmoid(gate) * val).astype(BF16)

    u_ref[0:halo, :] = u_ref[tm:tm + halo, :]
    y = x + _dot(act_ref[...], wdn_ref[...])
    if final_norm:
        y = y * _rms_inv(y) * fg_ref[...]
    y_ref[...] = y


def _ffn(x, a, gate, wpre, g, wup, cw, cb, wdn, fg, final_norm, tm):
    T, D = x.shape
    ff = wdn.shape[0]
    row = lambda n: pl.BlockSpec((tm, n), lambda i: (i, 0))
    resident = [wpre, g, wup, cw, cb, wdn, fg]
    operands = [x, a] + ([gate[0], gate[1]] if gate else []) + resident
    in_specs = ([row(D), row(a.shape[1])] + ([row(gate[0].shape[1]), _resident(gate[1].shape)] if gate else [])
                + [_resident(r.shape) for r in resident])
    return pl.pallas_call(
        functools.partial(_ffn_kernel, gated=gate is not None, final_norm=final_norm),
        out_shape=jax.ShapeDtypeStruct((T, D), F32),
        grid=(T // tm,),
        in_specs=in_specs,
        out_specs=row(D),
        scratch_shapes=[pltpu.VMEM((tm + SUBLANES, 2 * ff), F32), pltpu.VMEM((tm, ff), BF16)],
        compiler_params=_params(("arbitrary",)),
        name="conv_ffn",
    )(*operands)


def _qkv_kernel(x_ref, gkv_ref, gq_ref, wk_ref, wv_ref, wq_ref, qT_ref, k_ref, vT_ref, *, scale):
    heads = k_ref.shape[0]
    hd = k_ref.shape[2]
    x = x_ref[...]
    xn = x * _rms_inv(x)
    hkv = (xn * gkv_ref[...]).astype(BF16)
    hq = (xn * gq_ref[...]).astype(BF16)
    k = _dot(hkv, wk_ref[...]).astype(BF16)
    for h in range(heads):
        k_ref[h] = k[:, h * hd:(h + 1) * hd]
    vT_ref[...] = _dot(hkv, wv_ref[...]).T.astype(BF16)
    qT_ref[...] = (_dot(hq, wq_ref[...]) * scale).T.astype(BF16)


def _qkv(x, gkv, gq, wk, wv, wq, heads, scale, tm):
    T, D = x.shape
    hd = wk.shape[1] // heads
    colblk = lambda n: pl.BlockSpec((n, tm), lambda i: (0, i))
    return pl.pallas_call(
        functools.partial(_qkv_kernel, scale=scale),
        out_shape=(jax.ShapeDtypeStruct((wq.shape[1], T), BF16), jax.ShapeDtypeStruct((heads, T, hd), BF16),
                   jax.ShapeDtypeStruct((wv.shape[1], T), BF16)),
        grid=(T // tm,),
        in_specs=[pl.BlockSpec((tm, D), lambda i: (i, 0)), _resident(gkv.shape), _resident(gq.shape),
                  _resident(wk.shape), _resident(wv.shape), _resident(wq.shape)],
        out_specs=(colblk(wq.shape[1]), pl.BlockSpec((heads, tm, hd), lambda i: (0, i, 0)), colblk(wv.shape[1])),
        compiler_params=_params(("parallel",)),
        name="qkv_proj",
    )(x, gkv, gq, wk, wv, wq)


def _attn_kernel(tbl_ref, lam_ref, g_ref, qT_ref, k_ref, vT_ref, o_ref,
                 qp_ref, dbias_ref, pbias_ref, m_ref, acc_ref, *, lambda_init):
    hw, B = qT_ref.shape
    hd = hw // 2
    dv = vT_ref.shape[0]
    h = pl.program_id(0)
    qi = pl.program_id(1)
    nsub = B // LANES

    @pl.when(qi == 0)
    def _build_bias_tiles():
        far = tbl_ref[h * REL_BUCKETS + REL_BUCKETS - 1]
        a = lax.broadcasted_iota(jnp.int32, (LANES, LANES), 0)
        b = lax.broadcasted_iota(jnp.int32, (LANES, LANES), 1)

        def rel_bias(n):
            val = jnp.zeros(n.shape, F32)
            for bucket in range(REL_BUCKETS - 2, -1, -1):
                val = jnp.where(n < T5_BUCKET_LO[bucket + 1], (tbl_ref[h * REL_BUCKETS + bucket] - far) * LOG2E, val)
            return val

        e0 = jnp.where(b >= a, rel_bias(b - a), NEG)
        e1 = rel_bias(LANES + b - a)
        zero = jnp.zeros((LANES, LANES), F32)
        masked = jnp.full((LANES, LANES), NEG, F32)
        for r in range(nsub):
            for c in range(nsub):
                rs, cs = slice(r * LANES, (r + 1) * LANES), slice(c * LANES, (c + 1) * LANES)
                dbias_ref[rs, cs] = masked if c < r else e0 if c == r else e1 if c == r + 1 else zero
        for c in range(nsub):
            pbias_ref[:, c * LANES:(c + 1) * LANES] = e1 if c == 0 else zero

    qT = qT_ref[...].astype(F32)
    half = lax.broadcasted_iota(jnp.int32, qT.shape, 0) < hd
    qp_ref[0] = jnp.where(half, qT, 0.0).astype(BF16)
    qp_ref[1] = jnp.where(half, 0.0, qT).astype(BF16)
    m_ref[...] = jnp.full(m_ref.shape, NEG, F32)
    acc_ref[...] = jnp.zeros(acc_ref.shape, F32)
    ones = jnp.ones((ONES_ROWS, B), BF16)

    def scores(j, kind, c):
        st = pl.multiple_of(j * B, B)
        if kind == "prev":
            cut = B - LANES
            last = pl.multiple_of(st + cut, LANES)
            return jnp.concatenate([_dot(k_ref[pl.ds(st, cut), :], qp_ref[c]),
                                    _dot(k_ref[pl.ds(last, LANES), :], qp_ref[c]) + pbias_ref[...]], axis=0)
        s = _dot(k_ref[pl.ds(st, B), :], qp_ref[c])
        return s + dbias_ref[...] if kind == "diag" else s

    def update(j, sc):
        st = pl.multiple_of(j * B, B)
        vt = jnp.concatenate([vT_ref[:, pl.ds(st, B)], ones], axis=0)
        for c in range(2):
            m_old = m_ref[c]
            m_new = jnp.maximum(m_old, jnp.max(sc[c], axis=0, keepdims=True))
            p = jnp.exp2(sc[c] - m_new).astype(BF16)
            acc_ref[c] = jnp.exp2(m_old - m_new) * acc_ref[c] + _dot(vt, p)
            m_ref[c] = m_new

    def process(tiles):
        qk = lambda t: [scores(tiles[t][0], tiles[t][1], c) for c in range(2)]
        nxt = qk(0)
        for t in range(len(tiles)):
            cur, nxt = nxt, (qk(t + 1) if t + 1 < len(tiles) else None)
            update(tiles[t][0], cur)

    def process_far(tiles):
        qk = lambda t: [scores(tiles[t][0], "far", c) for c in range(2)]
        ref = [m_ref[c] for c in range(2)]
        top = [None, None]
        pv = [None, None]
        nxt = qk(0)
        for t in range(len(tiles)):
            cur, nxt = nxt, (qk(t + 1) if t + 1 < len(tiles) else None)
            st = pl.multiple_of(tiles[t][0] * B, B)
            vt = jnp.concatenate([vT_ref[:, pl.ds(st, B)], ones], axis=0)
            for c in range(2):
                cm = jnp.max(cur[c], axis=0, keepdims=True)
                top[c] = cm if top[c] is None else jnp.maximum(top[c], cm)
                d = _dot(vt, jnp.exp2(cur[c] - ref[c]).astype(BF16))
                pv[c] = d if pv[c] is None else pv[c] + d
        excess = jnp.max(jnp.maximum(top[0] - ref[0], top[1] - ref[1]))
        in_range = excess <= STALE_MAX_MARGIN

        @pl.when(in_range)
        def _commit():
            for c in range(2):
                acc_ref[c] = acc_ref[c] + pv[c]

        @pl.when(jnp.logical_not(in_range))
        def _redo():
            process(tiles)

    @pl.when(qi >= 1)
    def _previous_and_diagonal():
        process([(qi - 1, "prev"), (qi, "diag")])

    @pl.when(qi == 0)
    def _diagonal_only():
        process([(qi, "diag")])

    n_far = jnp.maximum(qi - 1, 0)
    far = lambda first, n: [(first + t, "far") for t in range(n)]

    def far_group(g, carry):
        process_far(far(FAR_GROUP * g, FAR_GROUP))
        return carry

    n_groups = n_far // FAR_GROUP
    lax.fori_loop(0, n_groups, far_group, 0)
    done = n_groups * FAR_GROUP
    size = FAR_GROUP // 2
    while size >= 1:
        @pl.when(((n_far - done) & size) != 0)
        def _leftover(size=size):
            process_far(far(done + ((n_far - done) & ~(2 * size - 1)), size))
        size //= 2

    lam = (jnp.exp(jnp.sum(lam_ref[0:1, :] * lam_ref[1:2, :], axis=-1, keepdims=True))
           - jnp.exp(jnp.sum(lam_ref[2:3, :] * lam_ref[3:4, :], axis=-1, keepdims=True)) + lambda_init)
    o1 = acc_ref[0, 0:dv, :] / acc_ref[0, dv:dv + 1, :]
    o2 = acc_ref[1, 0:dv, :] / acc_ref[1, dv:dv + 1, :]
    oT = o1 - lam * o2
    inv = lax.rsqrt(jnp.mean(oT * oT, axis=0, keepdims=True) + EPS)
    y = oT * inv * g_ref[...] * (1.0 - lambda_init)
    o_ref[...] = y.T.astype(o_ref.dtype)


def _attention(tbl, lam, g, qT, k, vT, lambda_init, block):
    heads, T, hw = k.shape
    dv = vT.shape[0] // heads
    B = block
    smem = pl.BlockSpec(memory_space=pltpu.SMEM)
    return pl.pallas_call(
        functools.partial(_attn_kernel, lambda_init=lambda_init),
        out_shape=jax.ShapeDtypeStruct((T, heads * dv), BF16),
        grid=(heads, T // B),
        in_specs=[smem,
                  pl.BlockSpec(lam.shape, lambda h, i: (0, 0)),
                  pl.BlockSpec(g.shape, lambda h, i: (0, 0)),
                  pl.BlockSpec((hw, B), lambda h, i: (h, i)),
                  pl.BlockSpec((None, T, hw), lambda h, i: (h, 0, 0)),
                  pl.BlockSpec((dv, T), lambda h, i: (h, 0))],
        out_specs=pl.BlockSpec((B, dv), lambda h, i: (i, h)),
        scratch_shapes=[pltpu.VMEM((2, hw, B), BF16), pltpu.VMEM((B, B), F32), pltpu.VMEM((LANES, B), F32),
                        pltpu.VMEM((2, 1, B), F32), pltpu.VMEM((2, dv + ONES_ROWS, B), F32)],
        compiler_params=_params(("arbitrary", "arbitrary")),
        name="diff_attention",
    )(tbl, lam, g, qT, k, vT)


def kernel(x, a_w_in, a_w_out, a_gnorm, a_lb_logits, b_w_q, b_w_o, b_lam_q1, b_lam_k1, b_lam_q2, b_lam_k2,
           b_subln, kv_norm, kv_w, rel_table, norm_mix, norm_ffn, ffn_w_up, ffn_conv_w, ffn_conv_b,
           ffn_w_down, final_norm):
    batch, T, D = x.shape
    depth = norm_mix.shape[0]
    n_a = a_w_in.shape[0]
    qdim = b_w_q.shape[2]
    row = lambda v: v.reshape(1, -1).astype(F32)
    tm = min(512, T)
    chunk = min(HG_CHUNK, T)
    block = min(ATT_BLOCK, T)
    tbl = rel_table.astype(F32).T.reshape(-1)

    outs = []
    for bi in range(batch):
        xs = x[bi]
        qT = k3 = vT = None
        for li in range(depth):
            if li == n_a:
                j = 0
                qT, k3, vT = _qkv(xs, row(kv_norm), row(norm_mix[li]), kv_w[:, :qdim].astype(BF16),
                                  kv_w[:, qdim:].astype(BF16), b_w_q[j].astype(BF16), DA_HEADS,
                                  DA_HEAD_DIM ** -0.5 * LOG2E, tm)
            if li < n_a:
                q, k, lf, v, gs = _hg_in(xs, row(norm_mix[li]), a_w_in[li].astype(BF16),
                                         a_lb_logits.astype(F32), li, min(256, T))
                mix = _hg_rec(q, k, lf, v, HG_HEADS, chunk)
                gate, w_mix = (gs, row(a_gnorm[li])), a_w_out[li].astype(BF16)
            else:
                j = li - n_a
                if j > 0:
                    qT, _, _ = _qkv(xs, row(kv_norm), row(norm_mix[li]), kv_w[:, :qdim].astype(BF16),
                                    kv_w[:, qdim:].astype(BF16), b_w_q[j].astype(BF16), DA_HEADS,
                                    DA_HEAD_DIM ** -0.5 * LOG2E, tm)
                lambda_init = 0.8 - 0.6 * math.exp(-0.3 * li)
                lam = jnp.stack([b_lam_q1[j], b_lam_k1[j], b_lam_q2[j], b_lam_k2[j]]).astype(F32)
                mix = _attention(tbl, lam, b_subln[j].reshape(-1, 1).astype(F32), qT, k3, vT, lambda_init, block)
                gate, w_mix = None, b_w_o[j].astype(BF16)
            last = li == depth - 1
            xs = _ffn(xs, mix, gate, w_mix, row(norm_ffn[li]), ffn_w_up[li].astype(BF16), ffn_conv_w[li].astype(F32),
                      row(ffn_conv_b[li]), ffn_w_down[li].astype(BF16), row(final_norm), last, tm)
        outs.append(xs)
    return jnp.stack(outs).astype(x.dtype)
```

```python
import functools
import math

import jax
import jax.numpy as jnp
from jax import lax
from jax.experimental import pallas as pl
from jax.experimental.pallas import tpu as pltpu

F32 = jnp.float32
BF16 = jnp.bfloat16
EPS = 1e-6

LANES = 128
SUBLANES = 8
MXU_COLS = 256
VMEM_LIMIT_BYTES = 56 * 2**20

HG_HEADS = 8
DA_HEADS = 8
DA_HEAD_DIM = 64
REL_BUCKETS = 32
REL_MAX_DIST = 128
CONV_W = 3
NEG = -1e30
LOG2E = math.log2(math.e)
ONES_ROWS = 2 * SUBLANES

HG_CHUNK = 128
ATT_BLOCK = 512
FAR_GROUP = 4
STALE_MAX_MARGIN = 64.0


def _t5_bucket_lower_bounds():
    max_exact = REL_BUCKETS // 2
    lo = list(range(max_exact))
    bucket_of = lambda n: min(
        max_exact + int(math.log(n / max_exact) / math.log(REL_MAX_DIST / max_exact) * (REL_BUCKETS - max_exact)),
        REL_BUCKETS - 1)
    n = max_exact
    for b in range(max_exact, REL_BUCKETS):
        while bucket_of(n) < b:
            n += 1
        lo.append(n)
    return tuple(lo)


T5_BUCKET_LO = _t5_bucket_lower_bounds()


def _dot(a, b):
    return jnp.dot(a, b, preferred_element_type=F32)


def _dot_nt(a, b):
    return lax.dot_general(a, b, (((1,), (1,)), ((), ())), preferred_element_type=F32)


def _dot_tn(a, b):
    return lax.dot_general(a, b, (((0,), (0,)), ((), ())), preferred_element_type=F32)


def _neg_abs(x):
    bits = pltpu.bitcast(x, jnp.uint32) | jnp.uint32(0x80000000)
    return pltpu.bitcast(bits, F32)


def _rms_inv(x):
    return lax.rsqrt(jnp.mean(x * x, axis=-1, keepdims=True) + EPS)


def _resident(shape):
    nd = len(shape)
    return pl.BlockSpec(shape, lambda *_: (0,) * nd, pipeline_mode=pl.Buffered(1))


def _params(sem):
    return pltpu.CompilerParams(dimension_semantics=sem, vmem_limit_bytes=VMEM_LIMIT_BYTES)


def _level_ref(bh, w):
    C, d = bh.shape
    n = 2 * w
    if w >= SUBLANES:
        return jnp.concatenate(
            [jnp.broadcast_to(bh[i * n + w - 1:i * n + w, :], (n, d)) for i in range(C // n)], axis=0)
    b3 = bh.reshape(C // SUBLANES, SUBLANES, d)
    pick = lambda s: jnp.broadcast_to(b3[:, s:s + 1, :], b3.shape)
    sub = lax.broadcasted_iota(jnp.int32, b3.shape, 1)
    r = pick(SUBLANES - n + w - 1)
    for s in range(SUBLANES - 2 * n, -1, -n):
        r = jnp.where(sub < s + n, pick(s + w - 1), r)
    return r.reshape(C, d)


def _hg_mixer_kernel(x_ref, g_ref, w_ref, lbl_ref, o_ref, gs_ref, st_ref, q_s, k_s, v_s, b_s, *, layer, heads, chunk):
    tm = x_ref.shape[0]
    C = chunk
    fd = o_ref.shape[1]
    dk = fd // heads
    dv = fd // heads
    n_chunks = tm // C

    @pl.when(pl.program_id(0) == 0)
    def _zero_state():
        st_ref[...] = jnp.zeros_like(st_ref)

    x = x_ref[...]
    h = (x * _rms_inv(x) * g_ref[...]).astype(BF16)
    lg = lbl_ref[...]
    ex = jnp.exp(lg - jnp.max(lg, axis=0, keepdims=True))
    sm = ex / jnp.sum(ex, axis=0, keepdims=True)
    lb = jnp.sum(sm[:layer + 1], axis=0, keepdims=True)
    log_lb = jnp.log(lb) * LOG2E
    log_1m_lb = jnp.log1p(-lb) * LOG2E

    def softplus_neg_abs(d):
        return jnp.log2(1.0 + jnp.exp2(_neg_abs(d)))

    def silu(z):
        return z / (1.0 + jnp.exp(-z))

    row = lax.broadcasted_iota(jnp.int32, (C, C), 0)
    col = lax.broadcasted_iota(jnp.int32, (C, C), 1)
    tri = jnp.where(row >= col, 1.0, 0.0).astype(BF16)
    levels = [1 << i for i in range(int(math.log2(C)))]
    xor = row ^ col
    below = row > col
    level_mask = [below & ((xor >> int(math.log2(w))) == 1) for w in levels]
    diag = row == col

    def project(i, c0):
        rows = slice(i * C, (i + 1) * C)
        hc = h[rows]
        cs = slice(c0, c0 + MXU_COLS)
        wcol = lambda part: w_ref[:, part * fd + c0:part * fd + c0 + MXU_COLS]
        q_s[i, :, cs] = silu(_dot(hc, wcol(0))).astype(BF16)
        zf = _dot(hc, wcol(1)) * LOG2E
        v_s[i, :, cs] = _dot(hc, wcol(2)).astype(BF16)
        c = log_1m_lb[:, cs] + jnp.minimum(zf, 0.0) - softplus_neg_abs(zf)
        a = log_lb[:, cs]
        lf = jnp.maximum(a, c) + softplus_neg_abs(a - c)
        k_s[i, :, cs] = jnp.exp2(c - zf).astype(BF16)
        gs_ref[rows, cs] = silu(_dot(hc, wcol(3))).astype(gs_ref.dtype)
        p0 = lf.astype(BF16)
        r1 = lf - p0.astype(F32)
        p1 = r1.astype(BF16)
        p2 = (r1 - p1.astype(F32)).astype(BF16)
        b_s[i, :, cs] = _dot(tri, p0) + _dot(tri, p1) + _dot(tri, p2)

    decay = lambda expo: jnp.exp2(expo).astype(BF16)

    def recur(i, hh):
        rows = slice(i * C, (i + 1) * C)
        sk = slice(hh * dk, (hh + 1) * dk)
        sv = slice(hh * dv, (hh + 1) * dv)
        bh = b_s[i, :, sk]
        qh = q_s[i, :, sk]
        kh = k_s[i, :, sk]
        vh = v_s[i, :, sv]
        b_last = bh[C - 1:C, :]
        a = jnp.where(diag, _dot_nt(qh, kh), 0.0)
        for w, mk in zip(levels, level_mask):
            e = decay(_neg_abs(bh - _level_ref(bh, w)))
            a = jnp.where(mk, _dot_nt(qh * e, kh * e), a)
        o = _dot(a.astype(BF16), vh)
        st = st_ref[hh]
        o = o + _dot_nt(qh * decay(bh), st.astype(BF16))
        st_ref[hh] = st * jnp.exp2(b_last) + _dot_tn(vh, kh * decay(b_last - bh))
        o_ref[rows, sv] = o.astype(o_ref.dtype)

    col_starts = list(range(0, fd, MXU_COLS))
    heads_per_cols = heads // len(col_starts)
    for c0 in col_starts:
        project(0, c0)
    for i in range(n_chunks):
        for n, c0 in enumerate(col_starts):
            if i + 1 < n_chunks:
                project(i + 1, c0)
            for hh in range(n * heads_per_cols, (n + 1) * heads_per_cols):
                recur(i, hh)


def _hg_mixer(x, g, w, lb_logits, layer, heads, tm, chunk):
    T, D = x.shape
    fd = lb_logits.shape[1]
    assert w.shape[1] == 4 * fd
    row = lambda n: pl.BlockSpec((tm, n), lambda i: (i, 0))
    stage = lambda dt: pltpu.VMEM((tm // chunk, chunk, fd), dt)
    return pl.pallas_call(
        functools.partial(_hg_mixer_kernel, layer=layer, heads=heads, chunk=chunk),
        out_shape=(jax.ShapeDtypeStruct((T, fd), BF16), jax.ShapeDtypeStruct((T, fd), BF16)),
        grid=(T // tm,),
        in_specs=[row(D), _resident((1, D)), _resident(w.shape), _resident(lb_logits.shape)],
        out_specs=(row(fd), row(fd)),
        scratch_shapes=[pltpu.VMEM((heads, fd // heads, fd // heads), F32),
                        stage(BF16), stage(BF16), stage(BF16), stage(F32)],
        compiler_params=_params(("arbitrary",)),
        name="hgrn2_mixer",
    )(x, g, w, lb_logits)


def _ffn_kernel(x_ref, a_ref, *refs, gated, final_norm):
    if gated:
        gs_ref, gn_ref, *refs = refs
    wpre_ref, g_ref, wup_ref, cw_ref, cb_ref, wdn_ref, fg_ref, y_ref, u_ref, act_ref = refs
    tm = x_ref.shape[0]
    ff = wdn_ref.shape[0]
    halo = SUBLANES

    @pl.when(pl.program_id(0) == 0)
    def _zero_halo():
        u_ref[0:halo, :] = jnp.zeros((halo, u_ref.shape[1]), F32)

    a = a_ref[...]
    if gated:
        o = a.astype(F32)
        a = (o * _rms_inv(o) * gn_ref[...] * gs_ref[...].astype(F32)).astype(BF16)
    x = x_ref[...] + _dot(a, wpre_ref[...])
    h = (x * _rms_inv(x) * g_ref[...]).astype(BF16)
    up_cols = 512
    for c0 in range(0, 2 * ff, up_cols):
        u_ref[halo:halo + tm, c0:c0 + up_cols] = _dot(h, wup_ref[:, c0:c0 + up_cols])

    def conv(cs):
        acc = cb_ref[:, cs] + cw_ref[CONV_W - 1:CONV_W, cs] * u_ref[halo:halo + tm, cs]
        for j in range(CONV_W - 1):
            off = halo - (CONV_W - 1) + j
            acc = acc + cw_ref[j:j + 1, cs] * u_ref[off:off + tm, cs]
        return acc

    act_cols = 256
    for c0 in range(0, ff, act_cols):
        gate = conv(slice(c0, c0 + act_cols))
        val = conv(slice(ff + c0, ff + c0 + act_cols))
        act_ref[:, c0:c0 + act_cols] = (gate * jax.nn.sigmoid(gate) * val).astype(BF16)

    u_ref[0:halo, :] = u_ref[tm:tm + halo, :]
    y = x + _dot(act_ref[...], wdn_ref[...])
    if final_norm:
        y = y * _rms_inv(y) * fg_ref[...]
    y_ref[...] = y


def _ffn(x, a, gate, wpre, g, wup, cw, cb, wdn, fg, final_norm, tm):
    T, D = x.shape
    ff = wdn.shape[0]
    row = lambda n: pl.BlockSpec((tm, n), lambda i: (i, 0))
    resident = [wpre, g, wup, cw, cb, wdn, fg]
    operands = [x, a] + ([gate[0], gate[1]] if gate else []) + resident
    in_specs = ([row(D), row(a.shape[1])] + ([row(gate[0].shape[1]), _resident(gate[1].shape)] if gate else [])
                + [_resident(r.shape) for r in resident])
    return pl.pallas_call(
        functools.partial(_ffn_kernel, gated=gate is not None, final_norm=final_norm),
        out_shape=jax.ShapeDtypeStruct((T, D), F32),
        grid=(T // tm,),
        in_specs=in_specs,
        out_specs=row(D),
        scratch_shapes=[pltpu.VMEM((tm + SUBLANES, 2 * ff), F32), pltpu.VMEM((tm, ff), BF16)],
        compiler_params=_params(("arbitrary",)),
        name="conv_ffn",
    )(*operands)


def _qkv_kernel(x_ref, gkv_ref, gq_ref, wk_ref, wv_ref, wq_ref, qT_ref, k_ref, vT_ref, *, scale):
    heads = k_ref.shape[0]
    hd = k_ref.shape[2]
    x = x_ref[...]
    xn = x * _rms_inv(x)
    hkv = (xn * gkv_ref[...]).astype(BF16)
    hq = (xn * gq_ref[...]).astype(BF16)
    k = _dot(hkv, wk_ref[...]).astype(BF16)
    for h in range(heads):
        k_ref[h] = k[:, h * hd:(h + 1) * hd]
    vT_ref[...] = _dot(hkv, wv_ref[...]).T.astype(BF16)
    qT_ref[...] = (_dot(hq, wq_ref[...]) * scale).T.astype(BF16)


def _qkv(x, gkv, gq, wk, wv, wq, heads, scale, tm):
    T, D = x.shape
    hd = wk.shape[1] // heads
    colblk = lambda n: pl.BlockSpec((n, tm), lambda i: (0, i))
    return pl.pallas_call(
        functools.partial(_qkv_kernel, scale=scale),
        out_shape=(jax.ShapeDtypeStruct((wq.shape[1], T), BF16), jax.ShapeDtypeStruct((heads, T, hd), BF16),
                   jax.ShapeDtypeStruct((wv.shape[1], T), BF16)),
        grid=(T // tm,),
        in_specs=[pl.BlockSpec((tm, D), lambda i: (i, 0)), _resident(gkv.shape), _resident(gq.shape),
                  _resident(wk.shape), _resident(wv.shape), _resident(wq.shape)],
        out_specs=(colblk(wq.shape[1]), pl.BlockSpec((heads, tm, hd), lambda i: (0, i, 0)), colblk(wv.shape[1])),
        compiler_params=_params(("parallel",)),
        name="qkv_proj",
    )(x, gkv, gq, wk, wv, wq)


def _attn_kernel(tbl_ref, lam_ref, g_ref, qT_ref, k_ref, vT_ref, o_ref,
                 qp_ref, dbias_ref, pbias_ref, m_ref, acc_ref, *, lambda_init):
    hw, B = qT_ref.shape
    hd = hw // 2
    dv = vT_ref.shape[0]
    h = pl.program_id(0)
    qi = pl.program_id(1)
    nsub = B // LANES

    @pl.when(qi == 0)
    def _build_bias_tiles():
        far = tbl_ref[h * REL_BUCKETS + REL_BUCKETS - 1]
        a = lax.broadcasted_iota(jnp.int32, (LANES, LANES), 0)
        b = lax.broadcasted_iota(jnp.int32, (LANES, LANES), 1)

        def rel_bias(n):
            val = jnp.zeros(n.shape, F32)
            for bucket in range(REL_BUCKETS - 2, -1, -1):
                val = jnp.where(n < T5_BUCKET_LO[bucket + 1], (tbl_ref[h * REL_BUCKETS + bucket] - far) * LOG2E, val)
            return val

        e0 = jnp.where(b >= a, rel_bias(b - a), NEG)
        e1 = rel_bias(LANES + b - a)
        zero = jnp.zeros((LANES, LANES), F32)
        masked = jnp.full((LANES, LANES), NEG, F32)
        for r in range(nsub):
            for c in range(nsub):
                rs, cs = slice(r * LANES, (r + 1) * LANES), slice(c * LANES, (c + 1) * LANES)
                dbias_ref[rs, cs] = masked if c < r else e0 if c == r else e1 if c == r + 1 else zero
        for c in range(nsub):
            pbias_ref[:, c * LANES:(c + 1) * LANES] = e1 if c == 0 else zero

    qT = qT_ref[...].astype(F32)
    half = lax.broadcasted_iota(jnp.int32, qT.shape, 0) < hd
    qp_ref[0] = jnp.where(half, qT, 0.0).astype(BF16)
    qp_ref[1] = jnp.where(half, 0.0, qT).astype(BF16)
    m_ref[...] = jnp.full(m_ref.shape, NEG, F32)
    acc_ref[...] = jnp.zeros(acc_ref.shape, F32)
    ones = jnp.ones((ONES_ROWS, B), BF16)

    def scores(j, kind, c):
        st = pl.multiple_of(j * B, B)
        if kind == "prev":
            cut = B - LANES
            last = pl.multiple_of(st + cut, LANES)
            return jnp.concatenate([_dot(k_ref[pl.ds(st, cut), :], qp_ref[c]),
                                    _dot(k_ref[pl.ds(last, LANES), :], qp_ref[c]) + pbias_ref[...]], axis=0)
        s = _dot(k_ref[pl.ds(st, B), :], qp_ref[c])
        return s + dbias_ref[...] if kind == "diag" else s

    def update(j, sc):
        st = pl.multiple_of(j * B, B)
        vt = jnp.concatenate([vT_ref[:, pl.ds(st, B)], ones], axis=0)
        for c in range(2):
            m_old = m_ref[c]
            m_new = jnp.maximum(m_old, jnp.max(sc[c], axis=0, keepdims=True))
            p = jnp.exp2(sc[c] - m_new).astype(BF16)
            acc_ref[c] = jnp.exp2(m_old - m_new) * acc_ref[c] + _dot(vt, p)
            m_ref[c] = m_new

    def process(tiles):
        qk = lambda t: [scores(tiles[t][0], tiles[t][1], c) for c in range(2)]
        nxt = qk(0)
        for t in range(len(tiles)):
            cur, nxt = nxt, (qk(t + 1) if t + 1 < len(tiles) else None)
            update(tiles[t][0], cur)

    def process_far(tiles):
        qk = lambda t: [scores(tiles[t][0], "far", c) for c in range(2)]
        ref = [m_ref[c] for c in range(2)]
        top = [None, None]
        pv = [None, None]
        nxt = qk(0)
        for t in range(len(tiles)):
            cur, nxt = nxt, (qk(t + 1) if t + 1 < len(tiles) else None)
            st = pl.multiple_of(tiles[t][0] * B, B)
            vt = jnp.concatenate([vT_ref[:, pl.ds(st, B)], ones], axis=0)
            for c in range(2):
                cm = jnp.max(cur[c], axis=0, keepdims=True)
                top[c] = cm if top[c] is None else jnp.maximum(top[c], cm)
                d = _dot(vt, jnp.exp2(cur[c] - ref[c]).astype(BF16))
                pv[c] = d if pv[c] is None else pv[c] + d
        excess = jnp.max(jnp.maximum(top[0] - ref[0], top[1] - ref[1]))
        in_range = excess <= STALE_MAX_MARGIN

        @pl.when(in_range)
        def _commit():
            for c in range(2):
                acc_ref[c] = acc_ref[c] + pv[c]

        @pl.when(jnp.logical_not(in_range))
        def _redo():
            process(tiles)

    @pl.when(qi >= 1)
    def _previous_and_diagonal():
        process([(qi - 1, "prev"), (qi, "diag")])

    @pl.when(qi == 0)
    def _diagonal_only():
        process([(qi, "diag")])

    n_far = jnp.maximum(qi - 1, 0)
    far = lambda first, n: [(first + t, "far") for t in range(n)]

    def far_group(g, carry):
        process_far(far(FAR_GROUP * g, FAR_GROUP))
        return carry

    n_groups = n_far // FAR_GROUP
    lax.fori_loop(0, n_groups, far_group, 0)
    done = n_groups * FAR_GROUP
    size = FAR_GROUP // 2
    while size >= 1:
        @pl.when(((n_far - done) & size) != 0)
        def _leftover(size=size):
            process_far(far(done + ((n_far - done) & ~(2 * size - 1)), size))
        size //= 2

    lam = (jnp.exp(jnp.sum(lam_ref[0:1, :] * lam_ref[1:2, :], axis=-1, keepdims=True))
           - jnp.exp(jnp.sum(lam_ref[2:3, :] * lam_ref[3:4, :], axis=-1, keepdims=True)) + lambda_init)
    o1 = acc_ref[0, 0:dv, :] / acc_ref[0, dv:dv + 1, :]
    o2 = acc_ref[1, 0:dv, :] / acc_ref[1, dv:dv + 1, :]
    oT = o1 - lam * o2
    inv = lax.rsqrt(jnp.mean(oT * oT, axis=0, keepdims=True) + EPS)
    y = oT * inv * g_ref[...] * (1.0 - lambda_init)
    o_ref[...] = y.T.astype(o_ref.dtype)


def _attention(tbl, lam, g, qT, k, vT, lambda_init, block):
    heads, T, hw = k.shape
    dv = vT.shape[0] // heads
    B = block
    smem = pl.BlockSpec(memory_space=pltpu.SMEM)
    return pl.pallas_call(
        functools.partial(_attn_kernel, lambda_init=lambda_init),
        out_shape=jax.ShapeDtypeStruct((T, heads * dv), BF16),
        grid=(heads, T // B),
        in_specs=[smem,
                  pl.BlockSpec(lam.shape, lambda h, i: (0, 0)),
                  pl.BlockSpec(g.shape, lambda h, i: (0, 0)),
                  pl.BlockSpec((hw, B), lambda h, i: (h, i)),
                  pl.BlockSpec((None, T, hw), lambda h, i: (h, 0, 0)),
                  pl.BlockSpec((dv, T), lambda h, i: (h, 0))],
        out_specs=pl.BlockSpec((B, dv), lambda h, i: (i, h)),
        scratch_shapes=[pltpu.VMEM((2, hw, B), BF16), pltpu.VMEM((B, B), F32), pltpu.VMEM((LANES, B), F32),
                        pltpu.VMEM((2, 1, B), F32), pltpu.VMEM((2, dv + ONES_ROWS, B), F32)],
        compiler_params=_params(("arbitrary", "arbitrary")),
        name="diff_attention",
    )(tbl, lam, g, qT, k, vT)


def kernel(x, a_w_in, a_w_out, a_gnorm, a_lb_logits, b_w_q, b_w_o, b_lam_q1, b_lam_k1, b_lam_q2, b_lam_k2,
           b_subln, kv_norm, kv_w, rel_table, norm_mix, norm_ffn, ffn_w_up, ffn_conv_w, ffn_conv_b,
           ffn_w_down, final_norm):
    batch, T, D = x.shape
    depth = norm_mix.shape[0]
    n_a = a_w_in.shape[0]
    qdim = b_w_q.shape[2]
    row = lambda v: v.reshape(1, -1).astype(F32)
    tm = min(512, T)
    chunk = min(HG_CHUNK, T)
    block = min(ATT_BLOCK, T)
    tbl = rel_table.astype(F32).T.reshape(-1)

    outs = []
    for bi in range(batch):
        xs = x[bi]
        qT = k3 = vT = None
        for li in range(depth):
            if li == n_a:
                j = 0
                qT, k3, vT = _qkv(xs, row(kv_norm), row(norm_mix[li]), kv_w[:, :qdim].astype(BF16),
                                  kv_w[:, qdim:].astype(BF16), b_w_q[j].astype(BF16), DA_HEADS,
                                  DA_HEAD_DIM ** -0.5 * LOG2E, tm)
            if li < n_a:
                mix, gs = _hg_mixer(xs, row(norm_mix[li]), a_w_in[li].astype(BF16), a_lb_logits.astype(F32), li,
                                    HG_HEADS, tm, chunk)
                gate, w_mix = (gs, row(a_gnorm[li])), a_w_out[li].astype(BF16)
            else:
                j = li - n_a
                if j > 0:
                    qT, _, _ = _qkv(xs, row(kv_norm), row(norm_mix[li]), kv_w[:, :qdim].astype(BF16),
                                    kv_w[:, qdim:].astype(BF16), b_w_q[j].astype(BF16), DA_HEADS,
                                    DA_HEAD_DIM ** -0.5 * LOG2E, tm)
                lambda_init = 0.8 - 0.6 * math.exp(-0.3 * li)
                lam = jnp.stack([b_lam_q1[j], b_lam_k1[j], b_lam_q2[j], b_lam_k2[j]]).astype(F32)
                mix = _attention(tbl, lam, b_subln[j].reshape(-1, 1).astype(F32), qT, k3, vT, lambda_init, block)
                gate, w_mix = None, b_w_o[j].astype(BF16)
            last = li == depth - 1
            xs = _ffn(xs, mix, gate, w_mix, row(norm_ffn[li]), ffn_w_up[li].astype(BF16), ffn_conv_w[li].astype(F32),
                      row(ffn_conv_b[li]), ffn_w_down[li].astype(BF16), row(final_norm), last, tm)
        outs.append(xs)
    return jnp.stack(outs).astype(x.dtype)
```

```python
import functools
import math

import jax
import jax.numpy as jnp
from jax import lax
from jax.experimental import pallas as pl
from jax.experimental.pallas import tpu as pltpu

F32 = jnp.float32
BF16 = jnp.bfloat16
EPS = 1e-6

LANES = 128
SUBLANES = 8
MXU_COLS = 256
VMEM_LIMIT_BYTES = 56 * 2**20

HG_HEADS = 8
DA_HEADS = 8
DA_HEAD_DIM = 64
REL_BUCKETS = 32
REL_MAX_DIST = 128
CONV_W = 3
NEG = -1e30
LOG2E = math.log2(math.e)
ONES_ROWS = 2 * SUBLANES

HG_CHUNK = 128
ATT_BLOCK = 512
FAR_GROUP = 8
STALE_MAX_MARGIN = 64.0


def _t5_bucket_lower_bounds():
    max_exact = REL_BUCKETS // 2
    lo = list(range(max_exact))
    bucket_of = lambda n: min(
        max_exact + int(math.log(n / max_exact) / math.log(REL_MAX_DIST / max_exact) * (REL_BUCKETS - max_exact)),
        REL_BUCKETS - 1)
    n = max_exact
    for b in range(max_exact, REL_BUCKETS):
        while bucket_of(n) < b:
            n += 1
        lo.append(n)
    return tuple(lo)


T5_BUCKET_LO = _t5_bucket_lower_bounds()


def _dot(a, b):
    return jnp.dot(a, b, preferred_element_type=F32)


def _dot_nt(a, b):
    return lax.dot_general(a, b, (((1,), (1,)), ((), ())), preferred_element_type=F32)


def _dot_tn(a, b):
    return lax.dot_general(a, b, (((0,), (0,)), ((), ())), preferred_element_type=F32)


def _neg_abs(x):
    bits = pltpu.bitcast(x, jnp.uint32) | jnp.uint32(0x80000000)
    return pltpu.bitcast(bits, F32)


def _rms_inv(x):
    return lax.rsqrt(jnp.mean(x * x, axis=-1, keepdims=True) + EPS)


def _resident(shape):
    nd = len(shape)
    return pl.BlockSpec(shape, lambda *_: (0,) * nd, pipeline_mode=pl.Buffered(1))


def _params(sem):
    return pltpu.CompilerParams(dimension_semantics=sem, vmem_limit_bytes=VMEM_LIMIT_BYTES)


def _level_ref(bh, w):
    C, d = bh.shape
    n = 2 * w
    if w >= SUBLANES:
        return jnp.concatenate(
            [jnp.broadcast_to(bh[i * n + w - 1:i * n + w, :], (n, d)) for i in range(C // n)], axis=0)
    b3 = bh.reshape(C // SUBLANES, SUBLANES, d)
    pick = lambda s: jnp.broadcast_to(b3[:, s:s + 1, :], b3.shape)
    sub = lax.broadcasted_iota(jnp.int32, b3.shape, 1)
    r = pick(SUBLANES - n + w - 1)
    for s in range(SUBLANES - 2 * n, -1, -n):
        r = jnp.where(sub < s + n, pick(s + w - 1), r)
    return r.reshape(C, d)


def _hg_mixer_kernel(x_ref, g_ref, w_ref, lbl_ref, o_ref, gs_ref, st_ref, q_s, k_s, v_s, b_s, *, layer, heads, chunk):
    tm = x_ref.shape[0]
    C = chunk
    fd = o_ref.shape[1]
    dk = fd // heads
    dv = fd // heads
    n_chunks = tm // C

    @pl.when(pl.program_id(0) == 0)
    def _zero_state():
        st_ref[...] = jnp.zeros_like(st_ref)

    x = x_ref[...]
    h = (x * _rms_inv(x) * g_ref[...]).astype(BF16)
    lg = lbl_ref[...]
    ex = jnp.exp(lg - jnp.max(lg, axis=0, keepdims=True))
    sm = ex / jnp.sum(ex, axis=0, keepdims=True)
    lb = jnp.sum(sm[:layer + 1], axis=0, keepdims=True)
    log_lb = jnp.log(lb) * LOG2E
    log_1m_lb = jnp.log1p(-lb) * LOG2E

    def softplus_neg_abs(d):
        return jnp.log2(1.0 + jnp.exp2(_neg_abs(d)))

    def silu(z):
        return z / (1.0 + jnp.exp(-z))

    row = lax.broadcasted_iota(jnp.int32, (C, C), 0)
    col = lax.broadcasted_iota(jnp.int32, (C, C), 1)
    tri = jnp.where(row >= col, 1.0, 0.0).astype(BF16)
    levels = [1 << i for i in range(int(math.log2(C)))]
    xor = row ^ col
    below = row > col
    level_mask = [below & ((xor >> int(math.log2(w))) == 1) for w in levels]
    diag = row == col

    def project(i, c0):
        rows = slice(i * C, (i + 1) * C)
        hc = h[rows]
        cs = slice(c0, c0 + MXU_COLS)
        wcol = lambda part: w_ref[:, part * fd + c0:part * fd + c0 + MXU_COLS]
        q_s[i, :, cs] = silu(_dot(hc, wcol(0))).astype(BF16)
        zf = _dot(hc, wcol(1)) * LOG2E
        v_s[i, :, cs] = _dot(hc, wcol(2)).astype(BF16)
        c = log_1m_lb[:, cs] + jnp.minimum(zf, 0.0) - softplus_neg_abs(zf)
        a = log_lb[:, cs]
        lf = jnp.maximum(a, c) + softplus_neg_abs(a - c)
        k_s[i, :, cs] = jnp.exp2(c - zf).astype(BF16)
        gs_ref[rows, cs] = silu(_dot(hc, wcol(3))).astype(gs_ref.dtype)
        p0 = lf.astype(BF16)
        r1 = lf - p0.astype(F32)
        p1 = r1.astype(BF16)
        p2 = (r1 - p1.astype(F32)).astype(BF16)
        b_s[i, :, cs] = _dot(tri, p0) + _dot(tri, p1) + _dot(tri, p2)

    decay = lambda expo: jnp.exp2(expo).astype(BF16)

    def recur(i, hh):
        rows = slice(i * C, (i + 1) * C)
        sk = slice(hh * dk, (hh + 1) * dk)
        sv = slice(hh * dv, (hh + 1) * dv)
        bh = b_s[i, :, sk]
        qh = q_s[i, :, sk]
        kh = k_s[i, :, sk]
        vh = v_s[i, :, sv]
        b_last = bh[C - 1:C, :]
        a = jnp.where(diag, _dot_nt(qh, kh), 0.0)
        for w, mk in zip(levels, level_mask):
            e = decay(_neg_abs(bh - _level_ref(bh, w)))
            a = jnp.where(mk, _dot_nt(qh * e, kh * e), a)
        o = _dot(a.astype(BF16), vh)
        st = st_ref[hh]
        o = o + _dot_nt(qh * decay(bh), st.astype(BF16))
        st_ref[hh] = st * jnp.exp2(b_last) + _dot_tn(vh, kh * decay(b_last - bh))
        o_ref[rows, sv] = o.astype(o_ref.dtype)

    col_starts = list(range(0, fd, MXU_COLS))
    heads_per_cols = heads // len(col_starts)
    for c0 in col_starts:
        project(0, c0)
    for i in range(n_chunks):
        for n, c0 in enumerate(col_starts):
            if i + 1 < n_chunks:
                project(i + 1, c0)
            for hh in range(n * heads_per_cols, (n + 1) * heads_per_cols):
                recur(i, hh)


def _hg_mixer(x, g, w, lb_logits, layer, heads, tm, chunk):
    T, D = x.shape
    fd = lb_logits.shape[1]
    assert w.shape[1] == 4 * fd
    row = lambda n: pl.BlockSpec((tm, n), lambda i: (i, 0))
    stage = lambda dt: pltpu.VMEM((tm // chunk, chunk, fd), dt)
    return pl.pallas_call(
        functools.partial(_hg_mixer_kernel, layer=layer, heads=heads, chunk=chunk),
        out_shape=(jax.ShapeDtypeStruct((T, fd), BF16), jax.ShapeDtypeStruct((T, fd), BF16)),
        grid=(T // tm,),
        in_specs=[row(D), _resident((1, D)), _resident(w.shape), _resident(lb_logits.shape)],
        out_specs=(row(fd), row(fd)),
        scratch_shapes=[pltpu.VMEM((heads, fd // heads, fd // heads), F32),
                        stage(BF16), stage(BF16), stage(BF16), stage(F32)],
        compiler_params=_params(("arbitrary",)),
        name="hgrn2_mixer",
    )(x, g, w, lb_logits)


def _ffn_kernel(x_ref, a_ref, *refs, gated, final_norm):
    if gated:
        gs_ref, gn_ref, *refs = refs
    wpre_ref, g_ref, wup_ref, cw_ref, cb_ref, wdn_ref, fg_ref, y_ref, u_ref, act_ref = refs
    tm = x_ref.shape[0]
    ff = wdn_ref.shape[0]
    halo = SUBLANES

    @pl.when(pl.program_id(0) == 0)
    def _zero_halo():
        u_ref[0:halo, :] = jnp.zeros((halo, u_ref.shape[1]), F32)

    a = a_ref[...]
    if gated:
        o = a.astype(F32)
        a = (o * _rms_inv(o) * gn_ref[...] * gs_ref[...].astype(F32)).astype(BF16)
    x = x_ref[...] + _dot(a, wpre_ref[...])
    h = (x * _rms_inv(x) * g_ref[...]).astype(BF16)
    up_cols = 512
    for c0 in range(0, 2 * ff, up_cols):
        u_ref[halo:halo + tm, c0:c0 + up_cols] = _dot(h, wup_ref[:, c0:c0 + up_cols])

    def conv(cs):
        acc = cb_ref[:, cs] + cw_ref[CONV_W - 1:CONV_W, cs] * u_ref[halo:halo + tm, cs]
        for j in range(CONV_W - 1):
            off = halo - (CONV_W - 1) + j
            acc = acc + cw_ref[j:j + 1, cs] * u_ref[off:off + tm, cs]
        return acc

    act_cols = 256
    for c0 in range(0, ff, act_cols):
        gate = conv(slice(c0, c0 + act_cols))
        val = conv(slice(ff + c0, ff + c0 + act_cols))
        act_ref[:, c0:c0 + act_cols] = (gate * jax.nn.sigmoid(gate) * val).astype(BF16)

    u_ref[0:halo, :] = u_ref[tm:tm + halo, :]
    y = x + _dot(act_ref[...], wdn_ref[...])
    if final_norm:
        y = y * _rms_inv(y) * fg_ref[...]
    y_ref[...] = y


def _ffn(x, a, gate, wpre, g, wup, cw, cb, wdn, fg, final_norm, tm):
    T, D = x.shape
    ff = wdn.shape[0]
    row = lambda n: pl.BlockSpec((tm, n), lambda i: (i, 0))
    resident = [wpre, g, wup, cw, cb, wdn, fg]
    operands = [x, a] + ([gate[0], gate[1]] if gate else []) + resident
    in_specs = ([row(D), row(a.shape[1])] + ([row(gate[0].shape[1]), _resident(gate[1].shape)] if gate else [])
                + [_resident(r.shape) for r in resident])
    return pl.pallas_call(
        functools.partial(_ffn_kernel, gated=gate is not None, final_norm=final_norm),
        out_shape=jax.ShapeDtypeStruct((T, D), F32),
        grid=(T // tm,),
        in_specs=in_specs,
        out_specs=row(D),
        scratch_shapes=[pltpu.VMEM((tm + SUBLANES, 2 * ff), F32), pltpu.VMEM((tm, ff), BF16)],
        compiler_params=_params(("arbitrary",)),
        name="conv_ffn",
    )(*operands)


def _qkv_kernel(x_ref, gkv_ref, gq_ref, wk_ref, wv_ref, wq_ref, qT_ref, k_ref, vT_ref, *, scale):
    heads = k_ref.shape[0]
    hd = k_ref.shape[2]
    x = x_ref[...]
    xn = x * _rms_inv(x)
    hkv = (xn * gkv_ref[...]).astype(BF16)
    hq = (xn * gq_ref[...]).astype(BF16)
    k = _dot(hkv, wk_ref[...]).astype(BF16)
    for h in range(heads):
        k_ref[h] = k[:, h * hd:(h + 1) * hd]
    vT_ref[...] = _dot(hkv, wv_ref[...]).T.astype(BF16)
    qT_ref[...] = (_dot(hq, wq_ref[...]) * scale).T.astype(BF16)


def _qkv(x, gkv, gq, wk, wv, wq, heads, scale, tm):
    T, D = x.shape
    hd = wk.shape[1] // heads
    colblk = lambda n: pl.BlockSpec((n, tm), lambda i: (0, i))
    return pl.pallas_call(
        functools.partial(_qkv_kernel, scale=scale),
        out_shape=(jax.ShapeDtypeStruct((wq.shape[1], T), BF16), jax.ShapeDtypeStruct((heads, T, hd), BF16),
                   jax.ShapeDtypeStruct((wv.shape[1], T), BF16)),
        grid=(T // tm,),
        in_specs=[pl.BlockSpec((tm, D), lambda i: (i, 0)), _resident(gkv.shape), _resident(gq.shape),
                  _resident(wk.shape), _resident(wv.shape), _resident(wq.shape)],
        out_specs=(colblk(wq.shape[1]), pl.BlockSpec((heads, tm, hd), lambda i: (0, i, 0)), colblk(wv.shape[1])),
        compiler_params=_params(("parallel",)),
        name="qkv_proj",
    )(x, gkv, gq, wk, wv, wq)


def _attn_kernel(tbl_ref, lam_ref, g_ref, qT_ref, k_ref, vT_ref, o_ref,
                 qp_ref, dbias_ref, pbias_ref, m_ref, acc_ref, *, lambda_init):
    hw, B = qT_ref.shape
    hd = hw // 2
    dv = vT_ref.shape[0]
    h = pl.program_id(0)
    qi = pl.program_id(1)
    nsub = B // LANES

    @pl.when(qi == 0)
    def _build_bias_tiles():
        far = tbl_ref[h * REL_BUCKETS + REL_BUCKETS - 1]
        a = lax.broadcasted_iota(jnp.int32, (LANES, LANES), 0)
        b = lax.broadcasted_iota(jnp.int32, (LANES, LANES), 1)

        def rel_bias(n):
            val = jnp.zeros(n.shape, F32)
            for bucket in range(REL_BUCKETS - 2, -1, -1):
                val = jnp.where(n < T5_BUCKET_LO[bucket + 1], (tbl_ref[h * REL_BUCKETS + bucket] - far) * LOG2E, val)
            return val

        e0 = jnp.where(b >= a, rel_bias(b - a), NEG)
        e1 = rel_bias(LANES + b - a)
        zero = jnp.zeros((LANES, LANES), F32)
        masked = jnp.full((LANES, LANES), NEG, F32)
        for r in range(nsub):
            for c in range(nsub):
                rs, cs = slice(r * LANES, (r + 1) * LANES), slice(c * LANES, (c + 1) * LANES)
                dbias_ref[rs, cs] = masked if c < r else e0 if c == r else e1 if c == r + 1 else zero
        for c in range(nsub):
            pbias_ref[:, c * LANES:(c + 1) * LANES] = e1 if c == 0 else zero

    qT = qT_ref[...].astype(F32)
    half = lax.broadcasted_iota(jnp.int32, qT.shape, 0) < hd
    qp_ref[0] = jnp.where(half, qT, 0.0).astype(BF16)
    qp_ref[1] = jnp.where(half, 0.0, qT).astype(BF16)
    m_ref[...] = jnp.full(m_ref.shape, NEG, F32)
    acc_ref[...] = jnp.zeros(acc_ref.shape, F32)
    ones = jnp.ones((ONES_ROWS, B), BF16)

    def scores(j, kind, c):
        st = pl.multiple_of(j * B, B)
        if kind == "prev":
            cut = B - LANES
            last = pl.multiple_of(st + cut, LANES)
            return jnp.concatenate([_dot(k_ref[pl.ds(st, cut), :], qp_ref[c]),
                                    _dot(k_ref[pl.ds(last, LANES), :], qp_ref[c]) + pbias_ref[...]], axis=0)
        s = _dot(k_ref[pl.ds(st, B), :], qp_ref[c])
        return s + dbias_ref[...] if kind == "diag" else s

    def update(j, sc):
        st = pl.multiple_of(j * B, B)
        vt = jnp.concatenate([vT_ref[:, pl.ds(st, B)], ones], axis=0)
        for c in range(2):
            m_old = m_ref[c]
            m_new = jnp.maximum(m_old, jnp.max(sc[c], axis=0, keepdims=True))
            p = jnp.exp2(sc[c] - m_new).astype(BF16)
            acc_ref[c] = jnp.exp2(m_old - m_new) * acc_ref[c] + _dot(vt, p)
            m_ref[c] = m_new

    def process(tiles):
        qk = lambda t: [scores(tiles[t][0], tiles[t][1], c) for c in range(2)]
        nxt = qk(0)
        for t in range(len(tiles)):
            cur, nxt = nxt, (qk(t + 1) if t + 1 < len(tiles) else None)
            update(tiles[t][0], cur)

    def process_far(tiles):
        qk = lambda t: [scores(tiles[t][0], "far", c) for c in range(2)]
        ref = [m_ref[c] for c in range(2)]
        top = [None, None]
        ps = [[], []]
        vts = []
        nxt = qk(0)
        for t in range(len(tiles)):
            cur, nxt = nxt, (qk(t + 1) if t + 1 < len(tiles) else None)
            st = pl.multiple_of(tiles[t][0] * B, B)
            vts.append(jnp.concatenate([vT_ref[:, pl.ds(st, B)], ones], axis=0))
            for c in range(2):
                cm = jnp.max(cur[c], axis=0, keepdims=True)
                top[c] = cm if top[c] is None else jnp.maximum(top[c], cm)
                ps[c].append(jnp.exp2(cur[c] - ref[c]).astype(BF16))
        vt_all = jnp.concatenate(vts, axis=1)
        pv = [_dot(vt_all, jnp.concatenate(ps[c], axis=0)) for c in range(2)]
        excess = jnp.max(jnp.maximum(top[0] - ref[0], top[1] - ref[1]))
        in_range = excess <= STALE_MAX_MARGIN

        @pl.when(in_range)
        def _commit():
            for c in range(2):
                acc_ref[c] = acc_ref[c] + pv[c]

        @pl.when(jnp.logical_not(in_range))
        def _redo():
            process(tiles)

    @pl.when(qi >= 1)
    def _previous_and_diagonal():
        process([(qi - 1, "prev"), (qi, "diag")])

    @pl.when(qi == 0)
    def _diagonal_only():
        process([(qi, "diag")])

    n_far = jnp.maximum(qi - 1, 0)
    far = lambda first, n: [(first + t, "far") for t in range(n)]

    def far_group(g, carry):
        process_far(far(FAR_GROUP * g, FAR_GROUP))
        return carry

    n_groups = n_far // FAR_GROUP
    lax.fori_loop(0, n_groups, far_group, 0)
    done = n_groups * FAR_GROUP
    size = FAR_GROUP // 2
    while size >= 1:
        @pl.when(((n_far - done) & size) != 0)
        def _leftover(size=size):
            process_far(far(done + ((n_far - done) & ~(2 * size - 1)), size))
        size //= 2

    lam = (jnp.exp(jnp.sum(lam_ref[0:1, :] * lam_ref[1:2, :], axis=-1, keepdims=True))
           - jnp.exp(jnp.sum(lam_ref[2:3, :] * lam_ref[3:4, :], axis=-1, keepdims=True)) + lambda_init)
    o1 = acc_ref[0, 0:dv, :] / acc_ref[0, dv:dv + 1, :]
    o2 = acc_ref[1, 0:dv, :] / acc_ref[1, dv:dv + 1, :]
    oT = o1 - lam * o2
    inv = lax.rsqrt(jnp.mean(oT * oT, axis=0, keepdims=True) + EPS)
    y = oT * inv * g_ref[...] * (1.0 - lambda_init)
    o_ref[...] = y.T.astype(o_ref.dtype)


def _attention(tbl, lam, g, qT, k, vT, lambda_init, block):
    heads, T, hw = k.shape
    dv = vT.shape[0] // heads
    B = block
    smem = pl.BlockSpec(memory_space=pltpu.SMEM)
    return pl.pallas_call(
        functools.partial(_attn_kernel, lambda_init=lambda_init),
        out_shape=jax.ShapeDtypeStruct((T, heads * dv), BF16),
        grid=(heads, T // B),
        in_specs=[smem,
                  pl.BlockSpec(lam.shape, lambda h, i: (0, 0)),
                  pl.BlockSpec(g.shape, lambda h, i: (0, 0)),
                  pl.BlockSpec((hw, B), lambda h, i: (h, i)),
                  pl.BlockSpec((None, T, hw), lambda h, i: (h, 0, 0)),
                  pl.BlockSpec((dv, T), lambda h, i: (h, 0))],
        out_specs=pl.BlockSpec((B, dv), lambda h, i: (i, h)),
        scratch_shapes=[pltpu.VMEM((2, hw, B), BF16), pltpu.VMEM((B, B), F32), pltpu.VMEM((LANES, B), F32),
                        pltpu.VMEM((2, 1, B), F32), pltpu.VMEM((2, dv + ONES_ROWS, B), F32)],
        compiler_params=_params(("arbitrary", "arbitrary")),
        name="diff_attention",
    )(tbl, lam, g, qT, k, vT)


def kernel(x, a_w_in, a_w_out, a_gnorm, a_lb_logits, b_w_q, b_w_o, b_lam_q1, b_lam_k1, b_lam_q2, b_lam_k2,
           b_subln, kv_norm, kv_w, rel_table, norm_mix, norm_ffn, ffn_w_up, ffn_conv_w, ffn_conv_b,
           ffn_w_down, final_norm):
    batch, T, D = x.shape
    depth = norm_mix.shape[0]
    n_a = a_w_in.shape[0]
    qdim = b_w_q.shape[2]
    row = lambda v: v.reshape(1, -1).astype(F32)
    tm = min(512, T)
    chunk = min(HG_CHUNK, T)
    block = min(ATT_BLOCK, T)
    tbl = rel_table.astype(F32).T.reshape(-1)

    outs = []
    for bi in range(batch):
        xs = x[bi]
        qT = k3 = vT = None
        for li in range(depth):
            if li == n_a:
                j = 0
                qT, k3, vT = _qkv(xs, row(kv_norm), row(norm_mix[li]), kv_w[:, :qdim].astype(BF16),
                                  kv_w[:, qdim:].astype(BF16), b_w_q[j].astype(BF16), DA_HEADS,
                                  DA_HEAD_DIM ** -0.5 * LOG2E, tm)
            if li < n_a:
                mix, gs = _hg_mixer(xs, row(norm_mix[li]), a_w_in[li].astype(BF16), a_lb_logits.astype(F32), li,
                                    HG_HEADS, tm, chunk)
                gate, w_mix = (gs, row(a_gnorm[li])), a_w_out[li].astype(BF16)
            else:
                j = li - n_a
                if j > 0:
                    qT, _, _ = _qkv(xs, row(kv_norm), row(norm_mix[li]), kv_w[:, :qdim].astype(BF16),
                                    kv_w[:, qdim:].astype(BF16), b_w_q[j].astype(BF16), DA_HEADS,
                                    DA_HEAD_DIM ** -0.5 * LOG2E, tm)
                lambda_init = 0.8 - 0.6 * math.exp(-0.3 * li)
                lam = jnp.stack([b_lam_q1[j], b_lam_k1[j], b_lam_q2[j], b_lam_k2[j]]).astype(F32)
                mix = _attention(tbl, lam, b_subln[j].reshape(-1, 1).astype(F32), qT, k3, vT, lambda_init, block)
                gate, w_mix = None, b_w_o[j].astype(BF16)
            last = li == depth - 1
            xs = _ffn(xs, mix, gate, w_mix, row(norm_ffn[li]), ffn_w_up[li].astype(BF16), ffn_conv_w[li].astype(F32),
                      row(ffn_conv_b[li]), ffn_w_down[li].astype(BF16), row(final_norm), last, tm)
        outs.append(xs)
    return jnp.stack(outs).astype(x.dtype)
```

```python
import functools
import math

import jax
import jax.numpy as jnp
from jax import lax
from jax.experimental import pallas as pl
from jax.experimental.pallas import tpu as pltpu

F32 = jnp.float32
BF16 = jnp.bfloat16
EPS = 1e-6

LANES = 128
SUBLANES = 8
MXU_COLS = 256
VMEM_LIMIT_BYTES = 56 * 2**20

HG_HEADS = 8
DA_HEADS = 8
DA_HEAD_DIM = 64
REL_BUCKETS = 32
REL_MAX_DIST = 128
CONV_W = 3
NEG = -1e30
LOG2E = math.log2(math.e)
ONES_ROWS = 2 * SUBLANES

HG_CHUNK = 128
ATT_BLOCK = 512
FAR_GROUP = 8
STALE_MAX_MARGIN = 64.0


def _t5_bucket_lower_bounds():
    max_exact = REL_BUCKETS // 2
    lo = list(range(max_exact))
    bucket_of = lambda n: min(
        max_exact + int(math.log(n / max_exact) / math.log(REL_MAX_DIST / max_exact) * (REL_BUCKETS - max_exact)),
        REL_BUCKETS - 1)
    n = max_exact
    for b in range(max_exact, REL_BUCKETS):
        while bucket_of(n) < b:
            n += 1
        lo.append(n)
    return tuple(lo)


T5_BUCKET_LO = _t5_bucket_lower_bounds()


def _dot(a, b):
    return jnp.dot(a, b, preferred_element_type=F32)


def _dot_nt(a, b):
    return lax.dot_general(a, b, (((1,), (1,)), ((), ())), preferred_element_type=F32)


def _dot_tn(a, b):
    return lax.dot_general(a, b, (((0,), (0,)), ((), ())), preferred_element_type=F32)


def _neg_abs(x):
    bits = pltpu.bitcast(x, jnp.uint32) | jnp.uint32(0x80000000)
    return pltpu.bitcast(bits, F32)


def _rms_inv(x):
    return lax.rsqrt(jnp.mean(x * x, axis=-1, keepdims=True) + EPS)


def _resident(shape):
    nd = len(shape)
    return pl.BlockSpec(shape, lambda *_: (0,) * nd, pipeline_mode=pl.Buffered(1))


def _params(sem):
    return pltpu.CompilerParams(dimension_semantics=sem, vmem_limit_bytes=VMEM_LIMIT_BYTES)


def _level_ref(bh, w):
    C, d = bh.shape
    n = 2 * w
    if w >= SUBLANES:
        return jnp.concatenate(
            [jnp.broadcast_to(bh[i * n + w - 1:i * n + w, :], (n, d)) for i in range(C // n)], axis=0)
    b3 = bh.reshape(C // SUBLANES, SUBLANES, d)
    pick = lambda s: jnp.broadcast_to(b3[:, s:s + 1, :], b3.shape)
    sub = lax.broadcasted_iota(jnp.int32, b3.shape, 1)
    r = pick(SUBLANES - n + w - 1)
    for s in range(SUBLANES - 2 * n, -1, -n):
        r = jnp.where(sub < s + n, pick(s + w - 1), r)
    return r.reshape(C, d)


def _hg_mixer_kernel(x_ref, g_ref, w_ref, lbl_ref, o_ref, gs_ref, st_ref, q_s, k_s, v_s, b_s, *, layer, heads, chunk):
    tm = x_ref.shape[0]
    C = chunk
    fd = o_ref.shape[1]
    dk = fd // heads
    dv = fd // heads
    n_chunks = tm // C

    @pl.when(pl.program_id(0) == 0)
    def _zero_state():
        st_ref[...] = jnp.zeros_like(st_ref)

    x = x_ref[...]
    h = (x * _rms_inv(x) * g_ref[...]).astype(BF16)
    lg = lbl_ref[...]
    ex = jnp.exp(lg - jnp.max(lg, axis=0, keepdims=True))
    sm = ex / jnp.sum(ex, axis=0, keepdims=True)
    lb = jnp.sum(sm[:layer + 1], axis=0, keepdims=True)
    log_lb = jnp.log(lb) * LOG2E
    log_1m_lb = jnp.log1p(-lb) * LOG2E

    def softplus_neg_abs(d):
        return jnp.log2(1.0 + jnp.exp2(_neg_abs(d)))

    def silu(z):
        return z / (1.0 + jnp.exp(-z))

    row = lax.broadcasted_iota(jnp.int32, (C, C), 0)
    col = lax.broadcasted_iota(jnp.int32, (C, C), 1)
    tri = jnp.where(row >= col, 1.0, 0.0).astype(BF16)
    tri3 = jnp.concatenate([tri, tri, tri], axis=1)
    levels = [1 << i for i in range(int(math.log2(C)))]
    xor = row ^ col
    below = row > col
    level_mask = [below & ((xor >> int(math.log2(w))) == 1) for w in levels]
    diag = row == col

    def project(i, c0):
        rows = slice(i * C, (i + 1) * C)
        hc = h[rows]
        cs = slice(c0, c0 + MXU_COLS)
        wcol = lambda part: w_ref[:, part * fd + c0:part * fd + c0 + MXU_COLS]
        q_s[i, :, cs] = silu(_dot(hc, wcol(0))).astype(BF16)
        zf = _dot(hc, wcol(1)) * LOG2E
        v_s[i, :, cs] = _dot(hc, wcol(2)).astype(BF16)
        c = log_1m_lb[:, cs] + jnp.minimum(zf, 0.0) - softplus_neg_abs(zf)
        a = log_lb[:, cs]
        lf = jnp.maximum(a, c) + softplus_neg_abs(a - c)
        k_s[i, :, cs] = jnp.exp2(c - zf).astype(BF16)
        gs_ref[rows, cs] = silu(_dot(hc, wcol(3))).astype(gs_ref.dtype)
        p0 = lf.astype(BF16)
        r1 = lf - p0.astype(F32)
        p1 = r1.astype(BF16)
        p2 = (r1 - p1.astype(F32)).astype(BF16)
        b_s[i, :, cs] = _dot(tri3, jnp.concatenate([p0, p1, p2], axis=0))

    decay = lambda expo: jnp.exp2(expo).astype(BF16)

    def recur(i, hh):
        rows = slice(i * C, (i + 1) * C)
        sk = slice(hh * dk, (hh + 1) * dk)
        sv = slice(hh * dv, (hh + 1) * dv)
        bh = b_s[i, :, sk]
        qh = q_s[i, :, sk]
        kh = k_s[i, :, sk]
        vh = v_s[i, :, sv]
        b_last = bh[C - 1:C, :]
        a = jnp.where(diag, _dot_nt(qh, kh), 0.0)
        for w, mk in zip(levels, level_mask):
            e = decay(_neg_abs(bh - _level_ref(bh, w)))
            a = jnp.where(mk, _dot_nt(qh * e, kh * e), a)
        o = _dot(a.astype(BF16), vh)
        st = st_ref[hh]
        o = o + _dot_nt(qh * decay(bh), st.astype(BF16))
        st_ref[hh] = st * jnp.exp2(b_last) + _dot_tn(vh, kh * decay(b_last - bh))
        o_ref[rows, sv] = o.astype(o_ref.dtype)

    col_starts = list(range(0, fd, MXU_COLS))
    heads_per_cols = heads // len(col_starts)
    for c0 in col_starts:
        project(0, c0)
    for i in range(n_chunks):
        for n, c0 in enumerate(col_starts):
            if i + 1 < n_chunks:
                project(i + 1, c0)
            for hh in range(n * heads_per_cols, (n + 1) * heads_per_cols):
                recur(i, hh)


def _hg_mixer(x, g, w, lb_logits, layer, heads, tm, chunk):
    T, D = x.shape
    fd = lb_logits.shape[1]
    assert w.shape[1] == 4 * fd
    row = lambda n: pl.BlockSpec((tm, n), lambda i: (i, 0))
    stage = lambda dt: pltpu.VMEM((tm // chunk, chunk, fd), dt)
    return pl.pallas_call(
        functools.partial(_hg_mixer_kernel, layer=layer, heads=heads, chunk=chunk),
        out_shape=(jax.ShapeDtypeStruct((T, fd), BF16), jax.ShapeDtypeStruct((T, fd), BF16)),
        grid=(T // tm,),
        in_specs=[row(D), _resident((1, D)), _resident(w.shape), _resident(lb_logits.shape)],
        out_specs=(row(fd), row(fd)),
        scratch_shapes=[pltpu.VMEM((heads, fd // heads, fd // heads), F32),
                        stage(BF16), stage(BF16), stage(BF16), stage(F32)],
        compiler_params=_params(("arbitrary",)),
        name="hgrn2_mixer",
    )(x, g, w, lb_logits)


def _ffn_kernel(x_ref, a_ref, *refs, gated, final_norm):
    if gated:
        gs_ref, gn_ref, *refs = refs
    wpre_ref, g_ref, wup_ref, cw_ref, cb_ref, wdn_ref, fg_ref, y_ref, u_ref, act_ref = refs
    tm = x_ref.shape[0]
    ff = wdn_ref.shape[0]
    halo = SUBLANES

    @pl.when(pl.program_id(0) == 0)
    def _zero_halo():
        u_ref[0:halo, :] = jnp.zeros((halo, u_ref.shape[1]), F32)

    a = a_ref[...]
    if gated:
        o = a.astype(F32)
        a = (o * _rms_inv(o) * gn_ref[...] * gs_ref[...].astype(F32)).astype(BF16)
    x = x_ref[...] + _dot(a, wpre_ref[...])
    h = (x * _rms_inv(x) * g_ref[...]).astype(BF16)
    up_cols = 512
    for c0 in range(0, 2 * ff, up_cols):
        u_ref[halo:halo + tm, c0:c0 + up_cols] = _dot(h, wup_ref[:, c0:c0 + up_cols])

    def conv(cs):
        acc = cb_ref[:, cs] + cw_ref[CONV_W - 1:CONV_W, cs] * u_ref[halo:halo + tm, cs]
        for j in range(CONV_W - 1):
            off = halo - (CONV_W - 1) + j
            acc = acc + cw_ref[j:j + 1, cs] * u_ref[off:off + tm, cs]
        return acc

    act_cols = 256
    for c0 in range(0, ff, act_cols):
        gate = conv(slice(c0, c0 + act_cols))
        val = conv(slice(ff + c0, ff + c0 + act_cols))
        act_ref[:, c0:c0 + act_cols] = (gate * jax.nn.sigmoid(gate) * val).astype(BF16)

    u_ref[0:halo, :] = u_ref[tm:tm + halo, :]
    y = x + _dot(act_ref[...], wdn_ref[...])
    if final_norm:
        y = y * _rms_inv(y) * fg_ref[...]
    y_ref[...] = y


def _ffn(x, a, gate, wpre, g, wup, cw, cb, wdn, fg, final_norm, tm):
    T, D = x.shape
    ff = wdn.shape[0]
    row = lambda n: pl.BlockSpec((tm, n), lambda i: (i, 0))
    resident = [wpre, g, wup, cw, cb, wdn, fg]
    operands = [x, a] + ([gate[0], gate[1]] if gate else []) + resident
    in_specs = ([row(D), row(a.shape[1])] + ([row(gate[0].shape[1]), _resident(gate[1].shape)] if gate else [])
                + [_resident(r.shape) for r in resident])
    return pl.pallas_call(
        functools.partial(_ffn_kernel, gated=gate is not None, final_norm=final_norm),
        out_shape=jax.ShapeDtypeStruct((T, D), F32),
        grid=(T // tm,),
        in_specs=in_specs,
        out_specs=row(D),
        scratch_shapes=[pltpu.VMEM((tm + SUBLANES, 2 * ff), F32), pltpu.VMEM((tm, ff), BF16)],
        compiler_params=_params(("arbitrary",)),
        name="conv_ffn",
    )(*operands)


def _qkv_kernel(x_ref, gkv_ref, gq_ref, wk_ref, wv_ref, wq_ref, qT_ref, k_ref, vT_ref, *, scale):
    heads = k_ref.shape[0]
    hd = k_ref.shape[2]
    x = x_ref[...]
    xn = x * _rms_inv(x)
    hkv = (xn * gkv_ref[...]).astype(BF16)
    hq = (xn * gq_ref[...]).astype(BF16)
    k = _dot(hkv, wk_ref[...]).astype(BF16)
    for h in range(heads):
        k_ref[h] = k[:, h * hd:(h + 1) * hd]
    vT_ref[...] = _dot(hkv, wv_ref[...]).T.astype(BF16)
    qT_ref[...] = (_dot(hq, wq_ref[...]) * scale).T.astype(BF16)


def _qkv(x, gkv, gq, wk, wv, wq, heads, scale, tm):
    T, D = x.shape
    hd = wk.shape[1] // heads
    colblk = lambda n: pl.BlockSpec((n, tm), lambda i: (0, i))
    return pl.pallas_call(
        functools.partial(_qkv_kernel, scale=scale),
        out_shape=(jax.ShapeDtypeStruct((wq.shape[1], T), BF16), jax.ShapeDtypeStruct((heads, T, hd), BF16),
                   jax.ShapeDtypeStruct((wv.shape[1], T), BF16)),
        grid=(T // tm,),
        in_specs=[pl.BlockSpec((tm, D), lambda i: (i, 0)), _resident(gkv.shape), _resident(gq.shape),
                  _resident(wk.shape), _resident(wv.shape), _resident(wq.shape)],
        out_specs=(colblk(wq.shape[1]), pl.BlockSpec((heads, tm, hd), lambda i: (0, i, 0)), colblk(wv.shape[1])),
        compiler_params=_params(("parallel",)),
        name="qkv_proj",
    )(x, gkv, gq, wk, wv, wq)


def _attn_kernel(tbl_ref, lam_ref, g_ref, qT_ref, k_ref, vT_ref, o_ref,
                 qp_ref, dbias_ref, pbias_ref, m_ref, excess_ref, acc_ref, *, lambda_init):
    hw, B = qT_ref.shape
    hd = hw // 2
    dv = vT_ref.shape[0]
    h = pl.program_id(0)
    qi = pl.program_id(1)
    nsub = B // LANES

    @pl.when(qi == 0)
    def _build_bias_tiles():
        far = tbl_ref[h * REL_BUCKETS + REL_BUCKETS - 1]
        a = lax.broadcasted_iota(jnp.int32, (LANES, LANES), 0)
        b = lax.broadcasted_iota(jnp.int32, (LANES, LANES), 1)

        def rel_bias(n):
            val = jnp.zeros(n.shape, F32)
            for bucket in range(REL_BUCKETS - 2, -1, -1):
                val = jnp.where(n < T5_BUCKET_LO[bucket + 1], (tbl_ref[h * REL_BUCKETS + bucket] - far) * LOG2E, val)
            return val

        e0 = jnp.where(b >= a, rel_bias(b - a), NEG)
        e1 = rel_bias(LANES + b - a)
        zero = jnp.zeros((LANES, LANES), F32)
        masked = jnp.full((LANES, LANES), NEG, F32)
        for r in range(nsub):
            for c in range(nsub):
                rs, cs = slice(r * LANES, (r + 1) * LANES), slice(c * LANES, (c + 1) * LANES)
                dbias_ref[rs, cs] = masked if c < r else e0 if c == r else e1 if c == r + 1 else zero
        for c in range(nsub):
            pbias_ref[:, c * LANES:(c + 1) * LANES] = e1 if c == 0 else zero

    qT = qT_ref[...].astype(F32)
    half = lax.broadcasted_iota(jnp.int32, qT.shape, 0) < hd
    qp_ref[0] = jnp.where(half, qT, 0.0).astype(BF16)
    qp_ref[1] = jnp.where(half, 0.0, qT).astype(BF16)
    m_ref[...] = jnp.full(m_ref.shape, NEG, F32)
    acc_ref[...] = jnp.zeros(acc_ref.shape, F32)
    ones = jnp.ones((ONES_ROWS, B), BF16)

    def scores(j, kind, c):
        st = pl.multiple_of(j * B, B)
        if kind == "prev":
            cut = B - LANES
            last = pl.multiple_of(st + cut, LANES)
            return jnp.concatenate([_dot(k_ref[pl.ds(st, cut), :], qp_ref[c]),
                                    _dot(k_ref[pl.ds(last, LANES), :], qp_ref[c]) + pbias_ref[...]], axis=0)
        s = _dot(k_ref[pl.ds(st, B), :], qp_ref[c])
        return s + dbias_ref[...] if kind == "diag" else s

    def update(j, sc):
        st = pl.multiple_of(j * B, B)
        vt = jnp.concatenate([vT_ref[:, pl.ds(st, B)], ones], axis=0)
        for c in range(2):
            m_old = m_ref[c]
            m_new = jnp.maximum(m_old, jnp.max(sc[c], axis=0, keepdims=True))
            p = jnp.exp2(sc[c] - m_new).astype(BF16)
            acc_ref[c] = jnp.exp2(m_old - m_new) * acc_ref[c] + _dot(vt, p)
            m_ref[c] = m_new

    def process(tiles):
        qk = lambda t: [scores(tiles[t][0], tiles[t][1], c) for c in range(2)]
        nxt = qk(0)
        for t in range(len(tiles)):
            cur, nxt = nxt, (qk(t + 1) if t + 1 < len(tiles) else None)
            update(tiles[t][0], cur)

    def process_far(tiles):
        qk = lambda t: [scores(tiles[t][0], "far", c) for c in range(2)]
        ref = [m_ref[c] for c in range(2)]
        top = [None, None]
        ps = [[], []]
        vts = []
        nxt = qk(0)
        for t in range(len(tiles)):
            cur, nxt = nxt, (qk(t + 1) if t + 1 < len(tiles) else None)
            st = pl.multiple_of(tiles[t][0] * B, B)
            vts.append(jnp.concatenate([vT_ref[:, pl.ds(st, B)], ones], axis=0))
            for c in range(2):
                cm = jnp.max(cur[c], axis=0, keepdims=True)
                top[c] = cm if top[c] is None else jnp.maximum(top[c], cm)
                ps[c].append(jnp.exp2(cur[c] - ref[c]).astype(BF16))
        vt_all = jnp.concatenate(vts, axis=1)
        for c in range(2):
            acc_ref[c] = acc_ref[c] + _dot(vt_all, jnp.concatenate(ps[c], axis=0))
        excess_ref[...] = jnp.maximum(excess_ref[...], jnp.maximum(top[0] - ref[0], top[1] - ref[1]))

    def causal_edge_tiles():
        @pl.when(qi >= 1)
        def _previous_and_diagonal():
            process([(qi - 1, "prev"), (qi, "diag")])

        @pl.when(qi == 0)
        def _diagonal_only():
            process([(qi, "diag")])

    causal_edge_tiles()
    excess_ref[...] = jnp.zeros(excess_ref.shape, F32)

    n_far = jnp.maximum(qi - 1, 0)
    far = lambda first, n: [(first + t, "far") for t in range(n)]

    def far_group(g, carry):
        process_far(far(FAR_GROUP * g, FAR_GROUP))
        return carry

    n_groups = n_far // FAR_GROUP
    lax.fori_loop(0, n_groups, far_group, 0)
    done = n_groups * FAR_GROUP
    size = FAR_GROUP // 2
    while size >= 1:
        @pl.when(((n_far - done) & size) != 0)
        def _leftover(size=size):
            process_far(far(done + ((n_far - done) & ~(2 * size - 1)), size))
        size //= 2

    @pl.when(jnp.max(excess_ref[...]) > STALE_MAX_MARGIN)
    def _redo_with_running_max():
        m_ref[...] = jnp.full(m_ref.shape, NEG, F32)
        acc_ref[...] = jnp.zeros(acc_ref.shape, F32)
        causal_edge_tiles()

        def one_far_tile(j, carry):
            process([(j, "far")])
            return carry

        lax.fori_loop(0, n_far, one_far_tile, 0)

    lam = (jnp.exp(jnp.sum(lam_ref[0:1, :] * lam_ref[1:2, :], axis=-1, keepdims=True))
           - jnp.exp(jnp.sum(lam_ref[2:3, :] * lam_ref[3:4, :], axis=-1, keepdims=True)) + lambda_init)
    o1 = acc_ref[0, 0:dv, :] / acc_ref[0, dv:dv + 1, :]
    o2 = acc_ref[1, 0:dv, :] / acc_ref[1, dv:dv + 1, :]
    oT = o1 - lam * o2
    inv = lax.rsqrt(jnp.mean(oT * oT, axis=0, keepdims=True) + EPS)
    y = oT * inv * g_ref[...] * (1.0 - lambda_init)
    o_ref[...] = y.T.astype(o_ref.dtype)


def _attention(tbl, lam, g, qT, k, vT, lambda_init, block):
    heads, T, hw = k.shape
    dv = vT.shape[0] // heads
    B = block
    smem = pl.BlockSpec(memory_space=pltpu.SMEM)
    return pl.pallas_call(
        functools.partial(_attn_kernel, lambda_init=lambda_init),
        out_shape=jax.ShapeDtypeStruct((T, heads * dv), BF16),
        grid=(heads, T // B),
        in_specs=[smem,
                  pl.BlockSpec(lam.shape, lambda h, i: (0, 0)),
                  pl.BlockSpec(g.shape, lambda h, i: (0, 0)),
                  pl.BlockSpec((hw, B), lambda h, i: (h, i)),
                  pl.BlockSpec((None, T, hw), lambda h, i: (h, 0, 0)),
                  pl.BlockSpec((dv, T), lambda h, i: (h, 0))],
        out_specs=pl.BlockSpec((B, dv), lambda h, i: (i, h)),
        scratch_shapes=[pltpu.VMEM((2, hw, B), BF16), pltpu.VMEM((B, B), F32), pltpu.VMEM((LANES, B), F32),
                        pltpu.VMEM((2, 1, B), F32), pltpu.VMEM((1, B), F32), pltpu.VMEM((2, dv + ONES_ROWS, B), F32)],
        compiler_params=_params(("arbitrary", "arbitrary")),
        name="diff_attention",
    )(tbl, lam, g, qT, k, vT)


def kernel(x, a_w_in, a_w_out, a_gnorm, a_lb_logits, b_w_q, b_w_o, b_lam_q1, b_lam_k1, b_lam_q2, b_lam_k2,
           b_subln, kv_norm, kv_w, rel_table, norm_mix, norm_ffn, ffn_w_up, ffn_conv_w, ffn_conv_b,
           ffn_w_down, final_norm):
    batch, T, D = x.shape
    depth = norm_mix.shape[0]
    n_a = a_w_in.shape[0]
    qdim = b_w_q.shape[2]
    row = lambda v: v.reshape(1, -1).astype(F32)
    tm = min(512, T)
    chunk = min(HG_CHUNK, T)
    block = min(ATT_BLOCK, T)
    tbl = rel_table.astype(F32).T.reshape(-1)

    outs = []
    for bi in range(batch):
        xs = x[bi]
        qT = k3 = vT = None
        for li in range(depth):
            if li == n_a:
                j = 0
                qT, k3, vT = _qkv(xs, row(kv_norm), row(norm_mix[li]), kv_w[:, :qdim].astype(BF16),
                                  kv_w[:, qdim:].astype(BF16), b_w_q[j].astype(BF16), DA_HEADS,
                                  DA_HEAD_DIM ** -0.5 * LOG2E, tm)
            if li < n_a:
                mix, gs = _hg_mixer(xs, row(norm_mix[li]), a_w_in[li].astype(BF16), a_lb_logits.astype(F32), li,
                                    HG_HEADS, tm, chunk)
                gate, w_mix = (gs, row(a_gnorm[li])), a_w_out[li].astype(BF16)
            else:
                j = li - n_a
                if j > 0:
                    qT, _, _ = _qkv(xs, row(kv_norm), row(norm_mix[li]), kv_w[:, :qdim].astype(BF16),
                                    kv_w[:, qdim:].astype(BF16), b_w_q[j].astype(BF16), DA_HEADS,
                                    DA_HEAD_DIM ** -0.5 * LOG2E, tm)
                lambda_init = 0.8 - 0.6 * math.exp(-0.3 * li)
                lam = jnp.stack([b_lam_q1[j], b_lam_k1[j], b_lam_q2[j], b_lam_k2[j]]).astype(F32)
                mix = _attention(tbl, lam, b_subln[j].reshape(-1, 1).astype(F32), qT, k3, vT, lambda_init, block)
                gate, w_mix = None, b_w_o[j].astype(BF16)
            last = li == depth - 1
            xs = _ffn(xs, mix, gate, w_mix, row(norm_ffn[li]), ffn_w_up[li].astype(BF16), ffn_conv_w[li].astype(F32),
                      row(ffn_conv_b[li]), ffn_w_down[li].astype(BF16), row(final_norm), last, tm)
        outs.append(xs)
    return jnp.stack(outs).astype(x.dtype)
```

```python
import functools
import math

import jax
import jax.numpy as jnp
from jax import lax
from jax.experimental import pallas as pl
from jax.experimental.pallas import tpu as pltpu

F32 = jnp.float32
BF16 = jnp.bfloat16
EPS = 1e-6

LANES = 128
SUBLANES = 8
MXU_COLS = 256
VMEM_LIMIT_BYTES = 56 * 2**20

HG_HEADS = 8
DA_HEADS = 8
DA_HEAD_DIM = 64
REL_BUCKETS = 32
REL_MAX_DIST = 128
CONV_W = 3
NEG = -1e30
LOG2E = math.log2(math.e)
ONES_ROWS = 2 * SUBLANES

HG_CHUNK = 128
HG_PROJ_CHUNKS = 2
ATT_BLOCK = 512
FAR_GROUP = 8
STALE_MAX_MARGIN = 64.0


def _t5_bucket_lower_bounds():
    max_exact = REL_BUCKETS // 2
    lo = list(range(max_exact))
    bucket_of = lambda n: min(
        max_exact + int(math.log(n / max_exact) / math.log(REL_MAX_DIST / max_exact) * (REL_BUCKETS - max_exact)),
        REL_BUCKETS - 1)
    n = max_exact
    for b in range(max_exact, REL_BUCKETS):
        while bucket_of(n) < b:
            n += 1
        lo.append(n)
    return tuple(lo)


T5_BUCKET_LO = _t5_bucket_lower_bounds()


def _dot(a, b):
    return jnp.dot(a, b, preferred_element_type=F32)


def _dot_nt(a, b):
    return lax.dot_general(a, b, (((1,), (1,)), ((), ())), preferred_element_type=F32)


def _dot_tn(a, b):
    return lax.dot_general(a, b, (((0,), (0,)), ((), ())), preferred_element_type=F32)


def _neg_abs(x):
    bits = pltpu.bitcast(x, jnp.uint32) | jnp.uint32(0x80000000)
    return pltpu.bitcast(bits, F32)


def _rms_inv(x):
    return lax.rsqrt(jnp.mean(x * x, axis=-1, keepdims=True) + EPS)


def _resident(shape):
    nd = len(shape)
    return pl.BlockSpec(shape, lambda *_: (0,) * nd, pipeline_mode=pl.Buffered(1))


def _params(sem):
    return pltpu.CompilerParams(dimension_semantics=sem, vmem_limit_bytes=VMEM_LIMIT_BYTES)


def _level_ref(bh, w):
    C, d = bh.shape
    n = 2 * w
    if w >= SUBLANES:
        return jnp.concatenate(
            [jnp.broadcast_to(bh[i * n + w - 1:i * n + w, :], (n, d)) for i in range(C // n)], axis=0)
    b3 = bh.reshape(C // SUBLANES, SUBLANES, d)
    pick = lambda s: jnp.broadcast_to(b3[:, s:s + 1, :], b3.shape)
    sub = lax.broadcasted_iota(jnp.int32, b3.shape, 1)
    r = pick(SUBLANES - n + w - 1)
    for s in range(SUBLANES - 2 * n, -1, -n):
        r = jnp.where(sub < s + n, pick(s + w - 1), r)
    return r.reshape(C, d)


def _hg_mixer_kernel(x_ref, g_ref, w_ref, lbl_ref, o_ref, gs_ref, st_ref, q_s, k_s, v_s, b_s, *, layer, heads, chunk):
    tm = x_ref.shape[0]
    C = chunk
    fd = o_ref.shape[1]
    dk = fd // heads
    dv = fd // heads
    n_chunks = tm // C

    @pl.when(pl.program_id(0) == 0)
    def _zero_state():
        st_ref[...] = jnp.zeros_like(st_ref)

    x = x_ref[...]
    h = (x * _rms_inv(x) * g_ref[...]).astype(BF16)
    lg = lbl_ref[...]
    ex = jnp.exp(lg - jnp.max(lg, axis=0, keepdims=True))
    sm = ex / jnp.sum(ex, axis=0, keepdims=True)
    lb = jnp.sum(sm[:layer + 1], axis=0, keepdims=True)
    log_lb = jnp.log(lb) * LOG2E
    log_1m_lb = jnp.log1p(-lb) * LOG2E

    def softplus_neg_abs(d):
        return jnp.log2(1.0 + jnp.exp2(_neg_abs(d)))

    def silu(z):
        return z / (1.0 + jnp.exp(-z))

    row = lax.broadcasted_iota(jnp.int32, (C, C), 0)
    col = lax.broadcasted_iota(jnp.int32, (C, C), 1)
    tri = jnp.where(row >= col, 1.0, 0.0).astype(BF16)
    tri3 = jnp.concatenate([tri, tri, tri], axis=1)
    levels = [1 << i for i in range(int(math.log2(C)))]
    xor = row ^ col
    below = row > col
    level_mask = [below & ((xor >> int(math.log2(w))) == 1) for w in levels]
    diag = row == col

    G = HG_PROJ_CHUNKS

    def project(i, c0):
        rows = slice(i * C, (i + G) * C)
        hc = h[rows]
        cs = slice(c0, c0 + MXU_COLS)
        wcol = lambda part: w_ref[:, part * fd + c0:part * fd + c0 + MXU_COLS]
        chunk_rows = [(i + n, slice(n * C, (n + 1) * C)) for n in range(G)]

        def stage(buf, val):
            for n, r in chunk_rows:
                buf[n, :, cs] = val[r]

        stage(q_s, silu(_dot(hc, wcol(0))).astype(BF16))
        zf = _dot(hc, wcol(1)) * LOG2E
        stage(v_s, _dot(hc, wcol(2)).astype(BF16))
        c = log_1m_lb[:, cs] + jnp.minimum(zf, 0.0) - softplus_neg_abs(zf)
        a = log_lb[:, cs]
        lf = jnp.maximum(a, c) + softplus_neg_abs(a - c)
        stage(k_s, jnp.exp2(c - zf).astype(BF16))
        gs_ref[rows, cs] = silu(_dot(hc, wcol(3))).astype(gs_ref.dtype)
        p0 = lf.astype(BF16)
        r1 = lf - p0.astype(F32)
        p1 = r1.astype(BF16)
        p2 = (r1 - p1.astype(F32)).astype(BF16)
        for n, r in chunk_rows:
            b_s[n, :, cs] = _dot(tri3, jnp.concatenate([p0[r], p1[r], p2[r]], axis=0))

    decay = lambda expo: jnp.exp2(expo).astype(BF16)

    def recur(i, hh):
        rows = slice(i * C, (i + 1) * C)
        sk = slice(hh * dk, (hh + 1) * dk)
        sv = slice(hh * dv, (hh + 1) * dv)
        bh = b_s[i, :, sk]
        qh = q_s[i, :, sk]
        kh = k_s[i, :, sk]
        vh = v_s[i, :, sv]
        b_last = bh[C - 1:C, :]
        a = jnp.where(diag, _dot_nt(qh, kh), 0.0)
        for w, mk in zip(levels, level_mask):
            e = decay(_neg_abs(bh - _level_ref(bh, w)))
            a = jnp.where(mk, _dot_nt(qh * e, kh * e), a)
        o = _dot(a.astype(BF16), vh)
        st = st_ref[hh]
        o = o + _dot_nt(qh * decay(bh), st.astype(BF16))
        st_ref[hh] = st * jnp.exp2(b_last) + _dot_tn(vh, kh * decay(b_last - bh))
        o_ref[rows, sv] = o.astype(o_ref.dtype)

    col_starts = list(range(0, fd, MXU_COLS))
    per_cols = G * heads // len(col_starts)
    for c0 in col_starts:
        project(0, c0)
    for i in range(0, n_chunks, G):
        units = [(i + n, hh) for n in range(G) for hh in range(heads)]
        for n, c0 in enumerate(col_starts):
            if i + G < n_chunks:
                project(i + G, c0)
            for ci, hh in units[n * per_cols:(n + 1) * per_cols]:
                recur(ci, hh)


def _hg_mixer(x, g, w, lb_logits, layer, heads, tm, chunk):
    T, D = x.shape
    fd = lb_logits.shape[1]
    assert w.shape[1] == 4 * fd
    row = lambda n: pl.BlockSpec((tm, n), lambda i: (i, 0))
    stage = lambda dt: pltpu.VMEM((tm // chunk, chunk, fd), dt)
    return pl.pallas_call(
        functools.partial(_hg_mixer_kernel, layer=layer, heads=heads, chunk=chunk),
        out_shape=(jax.ShapeDtypeStruct((T, fd), BF16), jax.ShapeDtypeStruct((T, fd), BF16)),
        grid=(T // tm,),
        in_specs=[row(D), _resident((1, D)), _resident(w.shape), _resident(lb_logits.shape)],
        out_specs=(row(fd), row(fd)),
        scratch_shapes=[pltpu.VMEM((heads, fd // heads, fd // heads), F32),
                        stage(BF16), stage(BF16), stage(BF16), stage(F32)],
        compiler_params=_params(("arbitrary",)),
        name="hgrn2_mixer",
    )(x, g, w, lb_logits)


def _ffn_kernel(x_ref, a_ref, *refs, gated, final_norm):
    if gated:
        gs_ref, gn_ref, *refs = refs
    wpre_ref, g_ref, wup_ref, cw_ref, cb_ref, wdn_ref, fg_ref, y_ref, u_ref, act_ref = refs
    tm = x_ref.shape[0]
    ff = wdn_ref.shape[0]
    halo = SUBLANES

    @pl.when(pl.program_id(0) == 0)
    def _zero_halo():
        u_ref[0:halo, :] = jnp.zeros((halo, u_ref.shape[1]), F32)

    halves = [slice(r0, r0 + tm // 2) for r0 in range(0, tm, tm // 2)]
    xs, hs = [], []
    for rows in halves:
        a = a_ref[rows, :]
        if gated:
            o = a.astype(F32)
            a = (o * _rms_inv(o) * gn_ref[...] * gs_ref[rows, :].astype(F32)).astype(BF16)
        x = x_ref[rows, :] + _dot(a, wpre_ref[...])
        xs.append(x)
        hs.append((x * _rms_inv(x) * g_ref[...]).astype(BF16))
    h = jnp.concatenate(hs, axis=0)
    up_cols = 512
    for c0 in range(0, 2 * ff, up_cols):
        u_ref[halo:halo + tm, c0:c0 + up_cols] = _dot(h, wup_ref[:, c0:c0 + up_cols])

    def conv(cs):
        acc = cb_ref[:, cs] + cw_ref[CONV_W - 1:CONV_W, cs] * u_ref[halo:halo + tm, cs]
        for j in range(CONV_W - 1):
            off = halo - (CONV_W - 1) + j
            acc = acc + cw_ref[j:j + 1, cs] * u_ref[off:off + tm, cs]
        return acc

    act_cols = 256
    for c0 in range(0, ff, act_cols):
        gate = conv(slice(c0, c0 + act_cols))
        val = conv(slice(ff + c0, ff + c0 + act_cols))
        act_ref[:, c0:c0 + act_cols] = (gate * jax.nn.sigmoid(gate) * val).astype(BF16)

    u_ref[0:halo, :] = u_ref[tm:tm + halo, :]
    for rows, x in zip(halves, xs):
        y = x + _dot(act_ref[rows, :], wdn_ref[...])
        if final_norm:
            y = y * _rms_inv(y) * fg_ref[...]
        y_ref[rows, :] = y


def _ffn(x, a, gate, wpre, g, wup, cw, cb, wdn, fg, final_norm, tm):
    T, D = x.shape
    ff = wdn.shape[0]
    row = lambda n: pl.BlockSpec((tm, n), lambda i: (i, 0))
    resident = [wpre, g, wup, cw, cb, wdn, fg]
    operands = [x, a] + ([gate[0], gate[1]] if gate else []) + resident
    in_specs = ([row(D), row(a.shape[1])] + ([row(gate[0].shape[1]), _resident(gate[1].shape)] if gate else [])
                + [_resident(r.shape) for r in resident])
    return pl.pallas_call(
        functools.partial(_ffn_kernel, gated=gate is not None, final_norm=final_norm),
        out_shape=jax.ShapeDtypeStruct((T, D), F32),
        grid=(T // tm,),
        in_specs=in_specs,
        out_specs=row(D),
        scratch_shapes=[pltpu.VMEM((tm + SUBLANES, 2 * ff), F32), pltpu.VMEM((tm, ff), BF16)],
        compiler_params=_params(("arbitrary",)),
        name="conv_ffn",
    )(*operands)


def _qkv_kernel(x_ref, gkv_ref, gq_ref, wk_ref, wv_ref, wq_ref, qT_ref, k_ref, vT_ref, *, scale):
    heads = k_ref.shape[0]
    hd = k_ref.shape[2]
    x = x_ref[...]
    xn = x * _rms_inv(x)
    hkv = (xn * gkv_ref[...]).astype(BF16)
    hq = (xn * gq_ref[...]).astype(BF16)
    k = _dot(hkv, wk_ref[...]).astype(BF16)
    for h in range(heads):
        k_ref[h] = k[:, h * hd:(h + 1) * hd]
    vT_ref[...] = _dot(hkv, wv_ref[...]).T.astype(BF16)
    qT_ref[...] = (_dot(hq, wq_ref[...]) * scale).T.astype(BF16)


def _qkv(x, gkv, gq, wk, wv, wq, heads, scale, tm):
    T, D = x.shape
    hd = wk.shape[1] // heads
    colblk = lambda n: pl.BlockSpec((n, tm), lambda i: (0, i))
    return pl.pallas_call(
        functools.partial(_qkv_kernel, scale=scale),
        out_shape=(jax.ShapeDtypeStruct((wq.shape[1], T), BF16), jax.ShapeDtypeStruct((heads, T, hd), BF16),
                   jax.ShapeDtypeStruct((wv.shape[1], T), BF16)),
        grid=(T // tm,),
        in_specs=[pl.BlockSpec((tm, D), lambda i: (i, 0)), _resident(gkv.shape), _resident(gq.shape),
                  _resident(wk.shape), _resident(wv.shape), _resident(wq.shape)],
        out_specs=(colblk(wq.shape[1]), pl.BlockSpec((heads, tm, hd), lambda i: (0, i, 0)), colblk(wv.shape[1])),
        compiler_params=_params(("parallel",)),
        name="qkv_proj",
    )(x, gkv, gq, wk, wv, wq)


def _attn_kernel(tbl_ref, lam_ref, g_ref, qT_ref, k_ref, vT_ref, o_ref,
                 qp_ref, dbias_ref, pbias_ref, m_ref, excess_ref, acc_ref, *, lambda_init):
    hw, B = qT_ref.shape
    hd = hw // 2
    dv = vT_ref.shape[0]
    h = pl.program_id(0)
    qi = pl.program_id(1)
    nsub = B // LANES

    @pl.when(qi == 0)
    def _build_bias_tiles():
        far = tbl_ref[h * REL_BUCKETS + REL_BUCKETS - 1]
        a = lax.broadcasted_iota(jnp.int32, (LANES, LANES), 0)
        b = lax.broadcasted_iota(jnp.int32, (LANES, LANES), 1)

        def rel_bias(n):
            val = jnp.zeros(n.shape, F32)
            for bucket in range(REL_BUCKETS - 2, -1, -1):
                val = jnp.where(n < T5_BUCKET_LO[bucket + 1], (tbl_ref[h * REL_BUCKETS + bucket] - far) * LOG2E, val)
            return val

        e0 = jnp.where(b >= a, rel_bias(b - a), NEG)
        e1 = rel_bias(LANES + b - a)
        zero = jnp.zeros((LANES, LANES), F32)
        masked = jnp.full((LANES, LANES), NEG, F32)
        for r in range(nsub):
            for c in range(nsub):
                rs, cs = slice(r * LANES, (r + 1) * LANES), slice(c * LANES, (c + 1) * LANES)
                dbias_ref[rs, cs] = masked if c < r else e0 if c == r else e1 if c == r + 1 else zero
        for c in range(nsub):
            pbias_ref[:, c * LANES:(c + 1) * LANES] = e1 if c == 0 else zero

    qT = qT_ref[...].astype(F32)
    half = lax.broadcasted_iota(jnp.int32, qT.shape, 0) < hd
    qp_ref[0] = jnp.where(half, qT, 0.0).astype(BF16)
    qp_ref[1] = jnp.where(half, 0.0, qT).astype(BF16)
    m_ref[...] = jnp.full(m_ref.shape, NEG, F32)
    acc_ref[...] = jnp.zeros(acc_ref.shape, F32)
    ones = jnp.ones((ONES_ROWS, B), BF16)

    def scores(j, kind, c):
        st = pl.multiple_of(j * B, B)
        if kind == "prev":
            cut = B - LANES
            last = pl.multiple_of(st + cut, LANES)
            return jnp.concatenate([_dot(k_ref[pl.ds(st, cut), :], qp_ref[c]),
                                    _dot(k_ref[pl.ds(last, LANES), :], qp_ref[c]) + pbias_ref[...]], axis=0)
        s = _dot(k_ref[pl.ds(st, B), :], qp_ref[c])
        return s + dbias_ref[...] if kind == "diag" else s

    def update(j, sc):
        st = pl.multiple_of(j * B, B)
        vt = jnp.concatenate([vT_ref[:, pl.ds(st, B)], ones], axis=0)
        for c in range(2):
            m_old = m_ref[c]
            m_new = jnp.maximum(m_old, jnp.max(sc[c], axis=0, keepdims=True))
            p = jnp.exp2(sc[c] - m_new).astype(BF16)
            acc_ref[c] = jnp.exp2(m_old - m_new) * acc_ref[c] + _dot(vt, p)
            m_ref[c] = m_new

    def process(tiles):
        qk = lambda t: [scores(tiles[t][0], tiles[t][1], c) for c in range(2)]
        nxt = qk(0)
        for t in range(len(tiles)):
            cur, nxt = nxt, (qk(t + 1) if t + 1 < len(tiles) else None)
            update(tiles[t][0], cur)

    def process_far(tiles):
        qk = lambda t: [scores(tiles[t][0], "far", c) for c in range(2)]
        ref = [m_ref[c] for c in range(2)]
        top = [None, None]
        ps = [[], []]
        vts = []
        nxt = qk(0)
        for t in range(len(tiles)):
            cur, nxt = nxt, (qk(t + 1) if t + 1 < len(tiles) else None)
            st = pl.multiple_of(tiles[t][0] * B, B)
            vts.append(jnp.concatenate([vT_ref[:, pl.ds(st, B)], ones], axis=0))
            for c in range(2):
                cm = jnp.max(cur[c], axis=0, keepdims=True)
                top[c] = cm if top[c] is None else jnp.maximum(top[c], cm)
                ps[c].append(jnp.exp2(cur[c] - ref[c]).astype(BF16))
        vt_all = jnp.concatenate(vts, axis=1)
        for c in range(2):
            acc_ref[c] = acc_ref[c] + _dot(vt_all, jnp.concatenate(ps[c], axis=0))
        excess_ref[...] = jnp.maximum(excess_ref[...], jnp.maximum(top[0] - ref[0], top[1] - ref[1]))

    def causal_edge_tiles():
        @pl.when(qi >= 1)
        def _previous_and_diagonal():
            process([(qi - 1, "prev"), (qi, "diag")])

        @pl.when(qi == 0)
        def _diagonal_only():
            process([(qi, "diag")])

    causal_edge_tiles()
    excess_ref[...] = jnp.zeros(excess_ref.shape, F32)

    n_far = jnp.maximum(qi - 1, 0)
    far = lambda first, n: [(first + t, "far") for t in range(n)]

    def far_group(g, carry):
        process_far(far(FAR_GROUP * g, FAR_GROUP))
        return carry

    n_groups = n_far // FAR_GROUP
    lax.fori_loop(0, n_groups, far_group, 0)
    done = n_groups * FAR_GROUP
    size = FAR_GROUP // 2
    while size >= 1:
        @pl.when(((n_far - done) & size) != 0)
        def _leftover(size=size):
            process_far(far(done + ((n_far - done) & ~(2 * size - 1)), size))
        size //= 2

    @pl.when(jnp.max(excess_ref[...]) > STALE_MAX_MARGIN)
    def _redo_with_running_max():
        m_ref[...] = jnp.full(m_ref.shape, NEG, F32)
        acc_ref[...] = jnp.zeros(acc_ref.shape, F32)
        causal_edge_tiles()

        def one_far_tile(j, carry):
            process([(j, "far")])
            return carry

        lax.fori_loop(0, n_far, one_far_tile, 0)

    lam = (jnp.exp(jnp.sum(lam_ref[0:1, :] * lam_ref[1:2, :], axis=-1, keepdims=True))
           - jnp.exp(jnp.sum(lam_ref[2:3, :] * lam_ref[3:4, :], axis=-1, keepdims=True)) + lambda_init)
    o1 = acc_ref[0, 0:dv, :] / acc_ref[0, dv:dv + 1, :]
    o2 = acc_ref[1, 0:dv, :] / acc_ref[1, dv:dv + 1, :]
    oT = o1 - lam * o2
    inv = lax.rsqrt(jnp.mean(oT * oT, axis=0, keepdims=True) + EPS)
    y = oT * inv * g_ref[...] * (1.0 - lambda_init)
    o_ref[...] = y.T.astype(o_ref.dtype)


def _attention(tbl, lam, g, qT, k, vT, lambda_init, block):
    heads, T, hw = k.shape
    dv = vT.shape[0] // heads
    B = block
    smem = pl.BlockSpec(memory_space=pltpu.SMEM)
    return pl.pallas_call(
        functools.partial(_attn_kernel, lambda_init=lambda_init),
        out_shape=jax.ShapeDtypeStruct((T, heads * dv), BF16),
        grid=(heads, T // B),
        in_specs=[smem,
                  pl.BlockSpec(lam.shape, lambda h, i: (0, 0)),
                  pl.BlockSpec(g.shape, lambda h, i: (0, 0)),
                  pl.BlockSpec((hw, B), lambda h, i: (h, i)),
                  pl.BlockSpec((None, T, hw), lambda h, i: (h, 0, 0)),
                  pl.BlockSpec((dv, T), lambda h, i: (h, 0))],
        out_specs=pl.BlockSpec((B, dv), lambda h, i: (i, h)),
        scratch_shapes=[pltpu.VMEM((2, hw, B), BF16), pltpu.VMEM((B, B), F32), pltpu.VMEM((LANES, B), F32),
                        pltpu.VMEM((2, 1, B), F32), pltpu.VMEM((1, B), F32), pltpu.VMEM((2, dv + ONES_ROWS, B), F32)],
        compiler_params=_params(("arbitrary", "arbitrary")),
        name="diff_attention",
    )(tbl, lam, g, qT, k, vT)


def kernel(x, a_w_in, a_w_out, a_gnorm, a_lb_logits, b_w_q, b_w_o, b_lam_q1, b_lam_k1, b_lam_q2, b_lam_k2,
           b_subln, kv_norm, kv_w, rel_table, norm_mix, norm_ffn, ffn_w_up, ffn_conv_w, ffn_conv_b,
           ffn_w_down, final_norm):
    batch, T, D = x.shape
    depth = norm_mix.shape[0]
    n_a = a_w_in.shape[0]
    qdim = b_w_q.shape[2]
    row = lambda v: v.reshape(1, -1).astype(F32)
    tm = min(512, T)
    chunk = min(HG_CHUNK, T)
    block = min(ATT_BLOCK, T)
    tbl = rel_table.astype(F32).T.reshape(-1)

    outs = []
    for bi in range(batch):
        xs = x[bi]
        qT = k3 = vT = None
        for li in range(depth):
            if li == n_a:
                j = 0
                qT, k3, vT = _qkv(xs, row(kv_norm), row(norm_mix[li]), kv_w[:, :qdim].astype(BF16),
                                  kv_w[:, qdim:].astype(BF16), b_w_q[j].astype(BF16), DA_HEADS,
                                  DA_HEAD_DIM ** -0.5 * LOG2E, tm)
            if li < n_a:
                mix, gs = _hg_mixer(xs, row(norm_mix[li]), a_w_in[li].astype(BF16), a_lb_logits.astype(F32), li,
                                    HG_HEADS, tm, chunk)
                gate, w_mix = (gs, row(a_gnorm[li])), a_w_out[li].astype(BF16)
            else:
                j = li - n_a
                if j > 0:
                    qT, _, _ = _qkv(xs, row(kv_norm), row(norm_mix[li]), kv_w[:, :qdim].astype(BF16),
                                    kv_w[:, qdim:].astype(BF16), b_w_q[j].astype(BF16), DA_HEADS,
                                    DA_HEAD_DIM ** -0.5 * LOG2E, tm)
                lambda_init = 0.8 - 0.6 * math.exp(-0.3 * li)
                lam = jnp.stack([b_lam_q1[j], b_lam_k1[j], b_lam_q2[j], b_lam_k2[j]]).astype(F32)
                mix = _attention(tbl, lam, b_subln[j].reshape(-1, 1).astype(F32), qT, k3, vT, lambda_init, block)
                gate, w_mix = None, b_w_o[j].astype(BF16)
            last = li == depth - 1
            xs = _ffn(xs, mix, gate, w_mix, row(norm_ffn[li]), ffn_w_up[li].astype(BF16), ffn_conv_w[li].astype(F32),
                      row(ffn_conv_b[li]), ffn_w_down[li].astype(BF16), row(final_norm), last, tm)
        outs.append(xs)
    return jnp.stack(outs).astype(x.dtype)
```

```python
import functools
import math

import jax
import jax.numpy as jnp
from jax import lax
from jax.experimental import pallas as pl
from jax.experimental.pallas import tpu as pltpu

F32 = jnp.float32
BF16 = jnp.bfloat16
EPS = 1e-6

LANES = 128
SUBLANES = 8
MXU_COLS = 256
VMEM_LIMIT_BYTES = 56 * 2**20

HG_HEADS = 8
DA_HEADS = 8
DA_HEAD_DIM = 64
REL_BUCKETS = 32
REL_MAX_DIST = 128
CONV_W = 3
NEG = -1e30
LOG2E = math.log2(math.e)
ONES_ROWS = 2 * SUBLANES

HG_CHUNK = 128
HG_PROJ_CHUNKS = 2
ATT_BLOCK = 512
FAR_GROUP = 8
FIXED_REF_MAX_SCORE = 64.0
FIXED_REF_MIN_SCORE = -100.0


def _t5_bucket_lower_bounds():
    max_exact = REL_BUCKETS // 2
    lo = list(range(max_exact))
    bucket_of = lambda n: min(
        max_exact + int(math.log(n / max_exact) / math.log(REL_MAX_DIST / max_exact) * (REL_BUCKETS - max_exact)),
        REL_BUCKETS - 1)
    n = max_exact
    for b in range(max_exact, REL_BUCKETS):
        while bucket_of(n) < b:
            n += 1
        lo.append(n)
    return tuple(lo)


T5_BUCKET_LO = _t5_bucket_lower_bounds()


def _dot(a, b):
    return jnp.dot(a, b, preferred_element_type=F32)


def _dot_nt(a, b):
    return lax.dot_general(a, b, (((1,), (1,)), ((), ())), preferred_element_type=F32)


def _dot_tn(a, b):
    return lax.dot_general(a, b, (((0,), (0,)), ((), ())), preferred_element_type=F32)


def _neg_abs(x):
    bits = pltpu.bitcast(x, jnp.uint32) | jnp.uint32(0x80000000)
    return pltpu.bitcast(bits, F32)


def _rms_inv(x):
    return lax.rsqrt(jnp.mean(x * x, axis=-1, keepdims=True) + EPS)


def _resident(shape):
    nd = len(shape)
    return pl.BlockSpec(shape, lambda *_: (0,) * nd, pipeline_mode=pl.Buffered(1))


def _params(sem):
    return pltpu.CompilerParams(dimension_semantics=sem, vmem_limit_bytes=VMEM_LIMIT_BYTES)


def _level_ref(bh, w):
    C, d = bh.shape
    n = 2 * w
    if w >= SUBLANES:
        return jnp.concatenate(
            [jnp.broadcast_to(bh[i * n + w - 1:i * n + w, :], (n, d)) for i in range(C // n)], axis=0)
    b3 = bh.reshape(C // SUBLANES, SUBLANES, d)
    pick = lambda s: jnp.broadcast_to(b3[:, s:s + 1, :], b3.shape)
    sub = lax.broadcasted_iota(jnp.int32, b3.shape, 1)
    r = pick(SUBLANES - n + w - 1)
    for s in range(SUBLANES - 2 * n, -1, -n):
        r = jnp.where(sub < s + n, pick(s + w - 1), r)
    return r.reshape(C, d)


def _hg_mixer_kernel(x_ref, g_ref, w_ref, lbl_ref, o_ref, gs_ref, st_ref, q_s, k_s, v_s, b_s, *, layer, heads, chunk):
    tm = x_ref.shape[0]
    C = chunk
    fd = o_ref.shape[1]
    dk = fd // heads
    dv = fd // heads
    n_chunks = tm // C

    @pl.when(pl.program_id(0) == 0)
    def _zero_state():
        st_ref[...] = jnp.zeros_like(st_ref)

    x = x_ref[...]
    h = (x * _rms_inv(x) * g_ref[...]).astype(BF16)
    lg = lbl_ref[...]
    ex = jnp.exp(lg - jnp.max(lg, axis=0, keepdims=True))
    sm = ex / jnp.sum(ex, axis=0, keepdims=True)
    lb = jnp.sum(sm[:layer + 1], axis=0, keepdims=True)
    log_lb = jnp.log(lb) * LOG2E
    log_1m_lb = jnp.log1p(-lb) * LOG2E

    def softplus_neg_abs(d):
        return jnp.log2(1.0 + jnp.exp2(_neg_abs(d)))

    def silu(z):
        return z / (1.0 + jnp.exp(-z))

    row = lax.broadcasted_iota(jnp.int32, (C, C), 0)
    col = lax.broadcasted_iota(jnp.int32, (C, C), 1)
    tri = jnp.where(row >= col, 1.0, 0.0).astype(BF16)
    tri3 = jnp.concatenate([tri, tri, tri], axis=1)
    levels = [1 << i for i in range(int(math.log2(C)))]
    xor = row ^ col
    below = row > col
    level_mask = [below & ((xor >> int(math.log2(w))) == 1) for w in levels]
    diag = row == col

    G = HG_PROJ_CHUNKS

    def project(i, c0):
        rows = slice(i * C, (i + G) * C)
        hc = h[rows]
        cs = slice(c0, c0 + MXU_COLS)
        wcol = lambda part: w_ref[:, part * fd + c0:part * fd + c0 + MXU_COLS]
        chunk_rows = [(i + n, slice(n * C, (n + 1) * C)) for n in range(G)]

        def stage(buf, val):
            for n, r in chunk_rows:
                buf[n, :, cs] = val[r]

        stage(q_s, silu(_dot(hc, wcol(0))).astype(BF16))
        zf = _dot(hc, wcol(1)) * LOG2E
        stage(v_s, _dot(hc, wcol(2)).astype(BF16))
        c = log_1m_lb[:, cs] + jnp.minimum(zf, 0.0) - softplus_neg_abs(zf)
        a = log_lb[:, cs]
        lf = jnp.maximum(a, c) + softplus_neg_abs(a - c)
        stage(k_s, jnp.exp2(c - zf).astype(BF16))
        gs_ref[rows, cs] = silu(_dot(hc, wcol(3))).astype(gs_ref.dtype)
        p0 = lf.astype(BF16)
        r1 = lf - p0.astype(F32)
        p1 = r1.astype(BF16)
        p2 = (r1 - p1.astype(F32)).astype(BF16)
        for n, r in chunk_rows:
            b_s[n, :, cs] = _dot(tri3, jnp.concatenate([p0[r], p1[r], p2[r]], axis=0))

    decay = lambda expo: jnp.exp2(expo).astype(BF16)

    def recur(i, hh):
        rows = slice(i * C, (i + 1) * C)
        sk = slice(hh * dk, (hh + 1) * dk)
        sv = slice(hh * dv, (hh + 1) * dv)
        bh = b_s[i, :, sk]
        qh = q_s[i, :, sk]
        kh = k_s[i, :, sk]
        vh = v_s[i, :, sv]
        b_last = bh[C - 1:C, :]
        a = jnp.where(diag, _dot_nt(qh, kh), 0.0)
        for w, mk in zip(levels, level_mask):
            e = decay(_neg_abs(bh - _level_ref(bh, w)))
            a = jnp.where(mk, _dot_nt(qh * e, kh * e), a)
        o = _dot(a.astype(BF16), vh)
        st = st_ref[hh]
        o = o + _dot_nt(qh * decay(bh), st.astype(BF16))
        st_ref[hh] = st * jnp.exp2(b_last) + _dot_tn(vh, kh * decay(b_last - bh))
        o_ref[rows, sv] = o.astype(o_ref.dtype)

    col_starts = list(range(0, fd, MXU_COLS))
    per_cols = G * heads // len(col_starts)
    for c0 in col_starts:
        project(0, c0)
    for i in range(0, n_chunks, G):
        units = [(i + n, hh) for n in range(G) for hh in range(heads)]
        for n, c0 in enumerate(col_starts):
            if i + G < n_chunks:
                project(i + G, c0)
            for ci, hh in units[n * per_cols:(n + 1) * per_cols]:
                recur(ci, hh)


def _hg_mixer(x, g, w, lb_logits, layer, heads, tm, chunk):
    T, D = x.shape
    fd = lb_logits.shape[1]
    assert w.shape[1] == 4 * fd
    row = lambda n: pl.BlockSpec((tm, n), lambda i: (i, 0))
    stage = lambda dt: pltpu.VMEM((tm // chunk, chunk, fd), dt)
    return pl.pallas_call(
        functools.partial(_hg_mixer_kernel, layer=layer, heads=heads, chunk=chunk),
        out_shape=(jax.ShapeDtypeStruct((T, fd), BF16), jax.ShapeDtypeStruct((T, fd), BF16)),
        grid=(T // tm,),
        in_specs=[row(D), _resident((1, D)), _resident(w.shape), _resident(lb_logits.shape)],
        out_specs=(row(fd), row(fd)),
        scratch_shapes=[pltpu.VMEM((heads, fd // heads, fd // heads), F32),
                        stage(BF16), stage(BF16), stage(BF16), stage(F32)],
        compiler_params=_params(("arbitrary",)),
        name="hgrn2_mixer",
    )(x, g, w, lb_logits)


def _ffn_kernel(x_ref, a_ref, *refs, gated, final_norm):
    if gated:
        gs_ref, gn_ref, *refs = refs
    wpre_ref, g_ref, wup_ref, cw_ref, cb_ref, wdn_ref, fg_ref, y_ref, u_ref, act_ref = refs
    tm = x_ref.shape[0]
    ff = wdn_ref.shape[0]
    halo = SUBLANES

    @pl.when(pl.program_id(0) == 0)
    def _zero_halo():
        u_ref[0:halo, :] = jnp.zeros((halo, u_ref.shape[1]), F32)

    halves = [slice(r0, r0 + tm // 2) for r0 in range(0, tm, tm // 2)]
    xs, hs = [], []
    for rows in halves:
        a = a_ref[rows, :]
        if gated:
            o = a.astype(F32)
            a = (o * _rms_inv(o) * gn_ref[...] * gs_ref[rows, :].astype(F32)).astype(BF16)
        x = x_ref[rows, :] + _dot(a, wpre_ref[...])
        xs.append(x)
        hs.append((x * _rms_inv(x) * g_ref[...]).astype(BF16))
    h = jnp.concatenate(hs, axis=0)
    up_cols = 512
    for c0 in range(0, 2 * ff, up_cols):
        u_ref[halo:halo + tm, c0:c0 + up_cols] = _dot(h, wup_ref[:, c0:c0 + up_cols])

    def conv(cs):
        acc = cb_ref[:, cs] + cw_ref[CONV_W - 1:CONV_W, cs] * u_ref[halo:halo + tm, cs]
        for j in range(CONV_W - 1):
            off = halo - (CONV_W - 1) + j
            acc = acc + cw_ref[j:j + 1, cs] * u_ref[off:off + tm, cs]
        return acc

    act_cols = 256
    for c0 in range(0, ff, act_cols):
        gate = conv(slice(c0, c0 + act_cols))
        val = conv(slice(ff + c0, ff + c0 + act_cols))
        act_ref[:, c0:c0 + act_cols] = (gate * jax.nn.sigmoid(gate) * val).astype(BF16)

    u_ref[0:halo, :] = u_ref[tm:tm + halo, :]
    for rows, x in zip(halves, xs):
        y = x + _dot(act_ref[rows, :], wdn_ref[...])
        if final_norm:
            y = y * _rms_inv(y) * fg_ref[...]
        y_ref[rows, :] = y


def _ffn(x, a, gate, wpre, g, wup, cw, cb, wdn, fg, final_norm, tm):
    T, D = x.shape
    ff = wdn.shape[0]
    row = lambda n: pl.BlockSpec((tm, n), lambda i: (i, 0))
    resident = [wpre, g, wup, cw, cb, wdn, fg]
    operands = [x, a] + ([gate[0], gate[1]] if gate else []) + resident
    in_specs = ([row(D), row(a.shape[1])] + ([row(gate[0].shape[1]), _resident(gate[1].shape)] if gate else [])
                + [_resident(r.shape) for r in resident])
    return pl.pallas_call(
        functools.partial(_ffn_kernel, gated=gate is not None, final_norm=final_norm),
        out_shape=jax.ShapeDtypeStruct((T, D), F32),
        grid=(T // tm,),
        in_specs=in_specs,
        out_specs=row(D),
        scratch_shapes=[pltpu.VMEM((tm + SUBLANES, 2 * ff), F32), pltpu.VMEM((tm, ff), BF16)],
        compiler_params=_params(("arbitrary",)),
        name="conv_ffn",
    )(*operands)


def _qkv_kernel(x_ref, gkv_ref, gq_ref, wk_ref, wv_ref, wq_ref, qT_ref, k_ref, vT_ref, *, scale):
    heads = k_ref.shape[0]
    hd = k_ref.shape[2]
    x = x_ref[...]
    xn = x * _rms_inv(x)
    hkv = (xn * gkv_ref[...]).astype(BF16)
    hq = (xn * gq_ref[...]).astype(BF16)
    k = _dot(hkv, wk_ref[...]).astype(BF16)
    for h in range(heads):
        k_ref[h] = k[:, h * hd:(h + 1) * hd]
    vT_ref[...] = _dot(hkv, wv_ref[...]).T.astype(BF16)
    qT_ref[...] = (_dot(hq, wq_ref[...]) * scale).T.astype(BF16)


def _qkv(x, gkv, gq, wk, wv, wq, heads, scale, tm):
    T, D = x.shape
    hd = wk.shape[1] // heads
    colblk = lambda n: pl.BlockSpec((n, tm), lambda i: (0, i))
    return pl.pallas_call(
        functools.partial(_qkv_kernel, scale=scale),
        out_shape=(jax.ShapeDtypeStruct((wq.shape[1], T), BF16), jax.ShapeDtypeStruct((heads, T, hd), BF16),
                   jax.ShapeDtypeStruct((wv.shape[1], T), BF16)),
        grid=(T // tm,),
        in_specs=[pl.BlockSpec((tm, D), lambda i: (i, 0)), _resident(gkv.shape), _resident(gq.shape),
                  _resident(wk.shape), _resident(wv.shape), _resident(wq.shape)],
        out_specs=(colblk(wq.shape[1]), pl.BlockSpec((heads, tm, hd), lambda i: (0, i, 0)), colblk(wv.shape[1])),
        compiler_params=_params(("parallel",)),
        name="qkv_proj",
    )(x, gkv, gq, wk, wv, wq)


def _attn_kernel(tbl_ref, lam_ref, g_ref, qT_ref, k_ref, vT_ref, o_ref,
                 qp_ref, dbias_ref, pbias_ref, m_ref, top_ref, acc_ref, *, lambda_init):
    hw, B = qT_ref.shape
    hd = hw // 2
    dv = vT_ref.shape[0]
    h = pl.program_id(0)
    qi = pl.program_id(1)
    nsub = B // LANES

    @pl.when(qi == 0)
    def _build_bias_tiles():
        far = tbl_ref[h * REL_BUCKETS + REL_BUCKETS - 1]
        a = lax.broadcasted_iota(jnp.int32, (LANES, LANES), 0)
        b = lax.broadcasted_iota(jnp.int32, (LANES, LANES), 1)

        def rel_bias(n):
            val = jnp.zeros(n.shape, F32)
            for bucket in range(REL_BUCKETS - 2, -1, -1):
                val = jnp.where(n < T5_BUCKET_LO[bucket + 1], (tbl_ref[h * REL_BUCKETS + bucket] - far) * LOG2E, val)
            return val

        e0 = jnp.where(b >= a, rel_bias(b - a), NEG)
        e1 = rel_bias(LANES + b - a)
        zero = jnp.zeros((LANES, LANES), F32)
        masked = jnp.full((LANES, LANES), NEG, F32)
        for r in range(nsub):
            for c in range(nsub):
                rs, cs = slice(r * LANES, (r + 1) * LANES), slice(c * LANES, (c + 1) * LANES)
                dbias_ref[rs, cs] = masked if c < r else e0 if c == r else e1 if c == r + 1 else zero
        for c in range(nsub):
            pbias_ref[:, c * LANES:(c + 1) * LANES] = e1 if c == 0 else zero

    qT = qT_ref[...].astype(F32)
    half = lax.broadcasted_iota(jnp.int32, qT.shape, 0) < hd
    qp_ref[0] = jnp.where(half, qT, 0.0).astype(BF16)
    qp_ref[1] = jnp.where(half, 0.0, qT).astype(BF16)
    acc_ref[...] = jnp.zeros(acc_ref.shape, F32)
    ones = jnp.ones((ONES_ROWS, B), BF16)

    def scores(j, kind, c):
        st = pl.multiple_of(j * B, B)
        if kind == "prev":
            cut = B - LANES
            last = pl.multiple_of(st + cut, LANES)
            return jnp.concatenate([_dot(k_ref[pl.ds(st, cut), :], qp_ref[c]),
                                    _dot(k_ref[pl.ds(last, LANES), :], qp_ref[c]) + pbias_ref[...]], axis=0)
        s = _dot(k_ref[pl.ds(st, B), :], qp_ref[c])
        return s + dbias_ref[...] if kind == "diag" else s

    def update(j, sc):
        st = pl.multiple_of(j * B, B)
        vt = jnp.concatenate([vT_ref[:, pl.ds(st, B)], ones], axis=0)
        for c in range(2):
            m_old = m_ref[c]
            m_new = jnp.maximum(m_old, jnp.max(sc[c], axis=0, keepdims=True))
            p = jnp.exp2(sc[c] - m_new).astype(BF16)
            acc_ref[c] = jnp.exp2(m_old - m_new) * acc_ref[c] + _dot(vt, p)
            m_ref[c] = m_new

    def process(tiles):
        qk = lambda t: [scores(tiles[t][0], tiles[t][1], c) for c in range(2)]
        nxt = qk(0)
        for t in range(len(tiles)):
            cur, nxt = nxt, (qk(t + 1) if t + 1 < len(tiles) else None)
            update(tiles[t][0], cur)

    def process_fast(tiles):
        qk = lambda t: [scores(tiles[t][0], tiles[t][1], c) for c in range(2)]
        top = [None, None]
        ps = [[], []]
        vts = []
        nxt = qk(0)
        for t in range(len(tiles)):
            cur, nxt = nxt, (qk(t + 1) if t + 1 < len(tiles) else None)
            st = pl.multiple_of(tiles[t][0] * B, B)
            vts.append(jnp.concatenate([vT_ref[:, pl.ds(st, B)], ones], axis=0))
            for c in range(2):
                cm = jnp.max(cur[c], axis=0, keepdims=True)
                top[c] = cm if top[c] is None else jnp.maximum(top[c], cm)
                ps[c].append(jnp.exp2(cur[c]).astype(BF16))
        vt_all = jnp.concatenate(vts, axis=1)
        for c in range(2):
            acc_ref[c] = acc_ref[c] + _dot(vt_all, jnp.concatenate(ps[c], axis=0))
            top_ref[c] = jnp.maximum(top_ref[c], top[c])

    top_ref[...] = jnp.full(top_ref.shape, NEG, F32)
    n_far = jnp.maximum(qi - 1, 0)
    far = lambda first, n: [(first + t, "far") for t in range(n)]
    edge = [(qi - 1, "prev"), (qi, "diag")]

    def far_group(g, carry):
        process_fast(far(FAR_GROUP * g, FAR_GROUP))
        return carry

    n_groups = n_far // FAR_GROUP
    lax.fori_loop(0, n_groups, far_group, 0)
    done = n_groups * FAR_GROUP
    rem = n_far - done
    size = FAR_GROUP // 2
    while size >= 1:
        first = done + (rem & ~(2 * size - 1))
        has, is_last = (rem & size) != 0, (rem & (size - 1)) == 0

        @pl.when(jnp.logical_and(has, jnp.logical_not(is_last)))
        def _leftover(first=first, size=size):
            process_fast(far(first, size))

        @pl.when(jnp.logical_and(has, is_last))
        def _leftover_and_edge(first=first, size=size):
            process_fast(far(first, size) + edge)
        size //= 2

    @pl.when(jnp.logical_and(rem == 0, qi >= 1))
    def _edge_only():
        process_fast(edge)

    @pl.when(qi == 0)
    def _diagonal_only():
        process_fast([(qi, "diag")])

    def causal_edge_tiles():
        @pl.when(qi >= 1)
        def _previous_and_diagonal():
            process(edge)

        @pl.when(qi == 0)
        def _diagonal_only_slow():
            process([(qi, "diag")])

    out_of_window = jnp.logical_or(jnp.max(top_ref[...]) > FIXED_REF_MAX_SCORE,
                                   jnp.min(top_ref[...]) < FIXED_REF_MIN_SCORE)

    @pl.when(out_of_window)
    def _redo_with_running_max():
        m_ref[...] = jnp.full(m_ref.shape, NEG, F32)
        acc_ref[...] = jnp.zeros(acc_ref.shape, F32)
        causal_edge_tiles()

        def one_far_tile(j, carry):
            process([(j, "far")])
            return carry

        lax.fori_loop(0, n_far, one_far_tile, 0)

    lam = (jnp.exp(jnp.sum(lam_ref[0:1, :] * lam_ref[1:2, :], axis=-1, keepdims=True))
           - jnp.exp(jnp.sum(lam_ref[2:3, :] * lam_ref[3:4, :], axis=-1, keepdims=True)) + lambda_init)
    o1 = acc_ref[0, 0:dv, :] / acc_ref[0, dv:dv + 1, :]
    o2 = acc_ref[1, 0:dv, :] / acc_ref[1, dv:dv + 1, :]
    oT = o1 - lam * o2
    inv = lax.rsqrt(jnp.mean(oT * oT, axis=0, keepdims=True) + EPS)
    y = oT * inv * g_ref[...] * (1.0 - lambda_init)
    o_ref[...] = y.T.astype(o_ref.dtype)


def _attention(tbl, lam, g, qT, k, vT, lambda_init, block):
    heads, T, hw = k.shape
    dv = vT.shape[0] // heads
    B = block
    smem = pl.BlockSpec(memory_space=pltpu.SMEM)
    return pl.pallas_call(
        functools.partial(_attn_kernel, lambda_init=lambda_init),
        out_shape=jax.ShapeDtypeStruct((T, heads * dv), BF16),
        grid=(heads, T // B),
        in_specs=[smem,
                  pl.BlockSpec(lam.shape, lambda h, i: (0, 0)),
                  pl.BlockSpec(g.shape, lambda h, i: (0, 0)),
                  pl.BlockSpec((hw, B), lambda h, i: (h, i)),
                  pl.BlockSpec((None, T, hw), lambda h, i: (h, 0, 0)),
                  pl.BlockSpec((dv, T), lambda h, i: (h, 0))],
        out_specs=pl.BlockSpec((B, dv), lambda h, i: (i, h)),
        scratch_shapes=[pltpu.VMEM((2, hw, B), BF16), pltpu.VMEM((B, B), F32), pltpu.VMEM((LANES, B), F32),
                        pltpu.VMEM((2, 1, B), F32), pltpu.VMEM((2, 1, B), F32), pltpu.VMEM((2, dv + ONES_ROWS, B), F32)],
        compiler_params=_params(("arbitrary", "arbitrary")),
        name="diff_attention",
    )(tbl, lam, g, qT, k, vT)


def kernel(x, a_w_in, a_w_out, a_gnorm, a_lb_logits, b_w_q, b_w_o, b_lam_q1, b_lam_k1, b_lam_q2, b_lam_k2,
           b_subln, kv_norm, kv_w, rel_table, norm_mix, norm_ffn, ffn_w_up, ffn_conv_w, ffn_conv_b,
           ffn_w_down, final_norm):
    batch, T, D = x.shape
    depth = norm_mix.shape[0]
    n_a = a_w_in.shape[0]
    qdim = b_w_q.shape[2]
    row = lambda v: v.reshape(1, -1).astype(F32)
    tm = min(512, T)
    chunk = min(HG_CHUNK, T)
    block = min(ATT_BLOCK, T)
    tbl = rel_table.astype(F32).T.reshape(-1)

    outs = []
    for bi in range(batch):
        xs = x[bi]
        qT = k3 = vT = None
        for li in range(depth):
            if li == n_a:
                j = 0
                qT, k3, vT = _qkv(xs, row(kv_norm), row(norm_mix[li]), kv_w[:, :qdim].astype(BF16),
                                  kv_w[:, qdim:].astype(BF16), b_w_q[j].astype(BF16), DA_HEADS,
                                  DA_HEAD_DIM ** -0.5 * LOG2E, tm)
            if li < n_a:
                mix, gs = _hg_mixer(xs, row(norm_mix[li]), a_w_in[li].astype(BF16), a_lb_logits.astype(F32), li,
                                    HG_HEADS, tm, chunk)
                gate, w_mix = (gs, row(a_gnorm[li])), a_w_out[li].astype(BF16)
            else:
                j = li - n_a
                if j > 0:
                    qT, _, _ = _qkv(xs, row(kv_norm), row(norm_mix[li]), kv_w[:, :qdim].astype(BF16),
                                    kv_w[:, qdim:].astype(BF16), b_w_q[j].astype(BF16), DA_HEADS,
                                    DA_HEAD_DIM ** -0.5 * LOG2E, tm)
                lambda_init = 0.8 - 0.6 * math.exp(-0.3 * li)
                lam = jnp.stack([b_lam_q1[j], b_lam_k1[j], b_lam_q2[j], b_lam_k2[j]]).astype(F32)
                mix = _attention(tbl, lam, b_subln[j].reshape(-1, 1).astype(F32), qT, k3, vT, lambda_init, block)
                gate, w_mix = None, b_w_o[j].astype(BF16)
            last = li == depth - 1
            xs = _ffn(xs, mix, gate, w_mix, row(norm_ffn[li]), ffn_w_up[li].astype(BF16), ffn_conv_w[li].astype(F32),
                      row(ffn_conv_b[li]), ffn_w_down[li].astype(BF16), row(final_norm), last, tm)
        outs.append(xs)
    return jnp.stack(outs).astype(x.dtype)
```

```python
import functools
import math

import jax
import jax.numpy as jnp
from jax import lax
from jax.experimental import pallas as pl
from jax.experimental.pallas import tpu as pltpu

F32 = jnp.float32
BF16 = jnp.bfloat16
EPS = 1e-6

LANES = 128
SUBLANES = 8
MXU_COLS = 256
VMEM_LIMIT_BYTES = 56 * 2**20

HG_HEADS = 8
DA_HEADS = 8
DA_HEAD_DIM = 64
REL_BUCKETS = 32
REL_MAX_DIST = 128
CONV_W = 3
NEG = -1e30
LOG2E = math.log2(math.e)

HG_CHUNK = 128
HG_PROJ_CHUNKS = 2
ATT_BLOCK = 512
FAR_GROUP = 8
FIXED_REF_MIN_SUM = 2.0 ** -100


def _t5_bucket_lower_bounds():
    max_exact = REL_BUCKETS // 2
    lo = list(range(max_exact))
    bucket_of = lambda n: min(
        max_exact + int(math.log(n / max_exact) / math.log(REL_MAX_DIST / max_exact) * (REL_BUCKETS - max_exact)),
        REL_BUCKETS - 1)
    n = max_exact
    for b in range(max_exact, REL_BUCKETS):
        while bucket_of(n) < b:
            n += 1
        lo.append(n)
    return tuple(lo)


T5_BUCKET_LO = _t5_bucket_lower_bounds()


def _dot(a, b):
    return jnp.dot(a, b, preferred_element_type=F32)


def _dot_nt(a, b):
    return lax.dot_general(a, b, (((1,), (1,)), ((), ())), preferred_element_type=F32)


def _dot_tn(a, b):
    return lax.dot_general(a, b, (((0,), (0,)), ((), ())), preferred_element_type=F32)


def _neg_abs(x):
    bits = pltpu.bitcast(x, jnp.uint32) | jnp.uint32(0x80000000)
    return pltpu.bitcast(bits, F32)


def _rms_inv(x):
    return lax.rsqrt(jnp.mean(x * x, axis=-1, keepdims=True) + EPS)


def _resident(shape):
    nd = len(shape)
    return pl.BlockSpec(shape, lambda *_: (0,) * nd, pipeline_mode=pl.Buffered(1))


def _params(sem):
    return pltpu.CompilerParams(dimension_semantics=sem, vmem_limit_bytes=VMEM_LIMIT_BYTES)


def _level_ref(bh, w):
    C, d = bh.shape
    n = 2 * w
    if w >= SUBLANES:
        return jnp.concatenate(
            [jnp.broadcast_to(bh[i * n + w - 1:i * n + w, :], (n, d)) for i in range(C // n)], axis=0)
    b3 = bh.reshape(C // SUBLANES, SUBLANES, d)
    pick = lambda s: jnp.broadcast_to(b3[:, s:s + 1, :], b3.shape)
    sub = lax.broadcasted_iota(jnp.int32, b3.shape, 1)
    r = pick(SUBLANES - n + w - 1)
    for s in range(SUBLANES - 2 * n, -1, -n):
        r = jnp.where(sub < s + n, pick(s + w - 1), r)
    return r.reshape(C, d)


def _hg_mixer_kernel(x_ref, g_ref, w_ref, lbl_ref, o_ref, gs_ref, st_ref, q_s, k_s, v_s, b_s, *, layer, heads, chunk):
    tm = x_ref.shape[0]
    C = chunk
    fd = o_ref.shape[1]
    dk = fd // heads
    dv = fd // heads
    n_chunks = tm // C

    @pl.when(pl.program_id(0) == 0)
    def _zero_state():
        st_ref[...] = jnp.zeros_like(st_ref)

    x = x_ref[...]
    h = (x * _rms_inv(x) * g_ref[...]).astype(BF16)
    lg = lbl_ref[...]
    ex = jnp.exp(lg - jnp.max(lg, axis=0, keepdims=True))
    sm = ex / jnp.sum(ex, axis=0, keepdims=True)
    lb = jnp.sum(sm[:layer + 1], axis=0, keepdims=True)
    log_lb = jnp.log(lb) * LOG2E
    log_1m_lb = jnp.log1p(-lb) * LOG2E

    def softplus_neg_abs(d):
        return jnp.log2(1.0 + jnp.exp2(_neg_abs(d)))

    def silu(z):
        return z / (1.0 + jnp.exp(-z))

    row = lax.broadcasted_iota(jnp.int32, (C, C), 0)
    col = lax.broadcasted_iota(jnp.int32, (C, C), 1)
    tri = jnp.where(row >= col, 1.0, 0.0).astype(BF16)
    tri3 = jnp.concatenate([tri, tri, tri], axis=1)
    levels = [1 << i for i in range(int(math.log2(C)))]
    xor = row ^ col
    below = row > col
    level_mask = [below & ((xor >> int(math.log2(w))) == 1) for w in levels]
    diag = row == col

    G = HG_PROJ_CHUNKS

    def project(i, c0):
        rows = slice(i * C, (i + G) * C)
        hc = h[rows]
        cs = slice(c0, c0 + MXU_COLS)
        wcol = lambda part: w_ref[:, part * fd + c0:part * fd + c0 + MXU_COLS]
        chunk_rows = [(i + n, slice(n * C, (n + 1) * C)) for n in range(G)]

        def stage(buf, val):
            for n, r in chunk_rows:
                buf[n, :, cs] = val[r]

        stage(q_s, silu(_dot(hc, wcol(0))).astype(BF16))
        zf = _dot(hc, wcol(1)) * LOG2E
        stage(v_s, _dot(hc, wcol(2)).astype(BF16))
        c = log_1m_lb[:, cs] + jnp.minimum(zf, 0.0) - softplus_neg_abs(zf)
        a = log_lb[:, cs]
        lf = jnp.maximum(a, c) + softplus_neg_abs(a - c)
        stage(k_s, jnp.exp2(c - zf).astype(BF16))
        gs_ref[rows, cs] = silu(_dot(hc, wcol(3))).astype(gs_ref.dtype)
        p0 = lf.astype(BF16)
        r1 = lf - p0.astype(F32)
        p1 = r1.astype(BF16)
        p2 = (r1 - p1.astype(F32)).astype(BF16)
        for n, r in chunk_rows:
            b_s[n, :, cs] = _dot(tri3, jnp.concatenate([p0[r], p1[r], p2[r]], axis=0))

    decay = lambda expo: jnp.exp2(expo).astype(BF16)

    def recur(i, hh):
        rows = slice(i * C, (i + 1) * C)
        sk = slice(hh * dk, (hh + 1) * dk)
        sv = slice(hh * dv, (hh + 1) * dv)
        bh = b_s[i, :, sk]
        qh = q_s[i, :, sk]
        kh = k_s[i, :, sk]
        vh = v_s[i, :, sv]
        b_last = bh[C - 1:C, :]
        a = jnp.where(diag, _dot_nt(qh, kh), 0.0)
        for w, mk in zip(levels, level_mask):
            e = decay(_neg_abs(bh - _level_ref(bh, w)))
            a = jnp.where(mk, _dot_nt(qh * e, kh * e), a)
        o = _dot(a.astype(BF16), vh)
        st = st_ref[hh]
        o = o + _dot_nt(qh * decay(bh), st.astype(BF16))
        st_ref[hh] = st * jnp.exp2(b_last) + _dot_tn(vh, kh * decay(b_last - bh))
        o_ref[rows, sv] = o.astype(o_ref.dtype)

    col_starts = list(range(0, fd, MXU_COLS))
    per_cols = G * heads // len(col_starts)
    for c0 in col_starts:
        project(0, c0)
    for i in range(0, n_chunks, G):
        units = [(i + n, hh) for n in range(G) for hh in range(heads)]
        for n, c0 in enumerate(col_starts):
            if i + G < n_chunks:
                project(i + G, c0)
            for ci, hh in units[n * per_cols:(n + 1) * per_cols]:
                recur(ci, hh)


def _hg_mixer(x, g, w, lb_logits, layer, heads, tm, chunk):
    T, D = x.shape
    fd = lb_logits.shape[1]
    assert w.shape[1] == 4 * fd
    row = lambda n: pl.BlockSpec((tm, n), lambda i: (i, 0))
    stage = lambda dt: pltpu.VMEM((tm // chunk, chunk, fd), dt)
    return pl.pallas_call(
        functools.partial(_hg_mixer_kernel, layer=layer, heads=heads, chunk=chunk),
        out_shape=(jax.ShapeDtypeStruct((T, fd), BF16), jax.ShapeDtypeStruct((T, fd), BF16)),
        grid=(T // tm,),
        in_specs=[row(D), _resident((1, D)), _resident(w.shape), _resident(lb_logits.shape)],
        out_specs=(row(fd), row(fd)),
        scratch_shapes=[pltpu.VMEM((heads, fd // heads, fd // heads), F32),
                        stage(BF16), stage(BF16), stage(BF16), stage(F32)],
        compiler_params=_params(("arbitrary",)),
        name="hgrn2_mixer",
    )(x, g, w, lb_logits)


def _ffn_kernel(x_ref, a_ref, *refs, gated, final_norm):
    if gated:
        gs_ref, gn_ref, *refs = refs
    wpre_ref, g_ref, wup_ref, cw_ref, cb_ref, wdn_ref, fg_ref, y_ref, u_ref, act_ref = refs
    tm = x_ref.shape[0]
    ff = wdn_ref.shape[0]
    halo = SUBLANES

    @pl.when(pl.program_id(0) == 0)
    def _zero_halo():
        u_ref[0:halo, :] = jnp.zeros((halo, u_ref.shape[1]), F32)

    halves = [slice(r0, r0 + tm // 2) for r0 in range(0, tm, tm // 2)]
    xs, hs = [], []
    for rows in halves:
        a = a_ref[rows, :]
        if gated:
            o = a.astype(F32)
            a = (o * _rms_inv(o) * gn_ref[...] * gs_ref[rows, :].astype(F32)).astype(BF16)
        x = x_ref[rows, :] + _dot(a, wpre_ref[...])
        xs.append(x)
        hs.append((x * _rms_inv(x) * g_ref[...]).astype(BF16))
    h = jnp.concatenate(hs, axis=0)
    up_cols = 512
    for c0 in range(0, 2 * ff, up_cols):
        u_ref[halo:halo + tm, c0:c0 + up_cols] = _dot(h, wup_ref[:, c0:c0 + up_cols])

    def conv(cs):
        acc = cb_ref[:, cs] + cw_ref[CONV_W - 1:CONV_W, cs] * u_ref[halo:halo + tm, cs]
        for j in range(CONV_W - 1):
            off = halo - (CONV_W - 1) + j
            acc = acc + cw_ref[j:j + 1, cs] * u_ref[off:off + tm, cs]
        return acc

    act_cols = 256
    for c0 in range(0, ff, act_cols):
        gate = conv(slice(c0, c0 + act_cols))
        val = conv(slice(ff + c0, ff + c0 + act_cols))
        act_ref[:, c0:c0 + act_cols] = (gate * jax.nn.sigmoid(gate) * val).astype(BF16)

    u_ref[0:halo, :] = u_ref[tm:tm + halo, :]
    for rows, x in zip(halves, xs):
        y = x + _dot(act_ref[rows, :], wdn_ref[...])
        if final_norm:
            y = y * _rms_inv(y) * fg_ref[...]
        y_ref[rows, :] = y


def _ffn(x, a, gate, wpre, g, wup, cw, cb, wdn, fg, final_norm, tm):
    T, D = x.shape
    ff = wdn.shape[0]
    row = lambda n: pl.BlockSpec((tm, n), lambda i: (i, 0))
    resident = [wpre, g, wup, cw, cb, wdn, fg]
    operands = [x, a] + ([gate[0], gate[1]] if gate else []) + resident
    in_specs = ([row(D), row(a.shape[1])] + ([row(gate[0].shape[1]), _resident(gate[1].shape)] if gate else [])
                + [_resident(r.shape) for r in resident])
    return pl.pallas_call(
        functools.partial(_ffn_kernel, gated=gate is not None, final_norm=final_norm),
        out_shape=jax.ShapeDtypeStruct((T, D), F32),
        grid=(T // tm,),
        in_specs=in_specs,
        out_specs=row(D),
        scratch_shapes=[pltpu.VMEM((tm + SUBLANES, 2 * ff), F32), pltpu.VMEM((tm, ff), BF16)],
        compiler_params=_params(("arbitrary",)),
        name="conv_ffn",
    )(*operands)


def _qkv_kernel(x_ref, gkv_ref, gq_ref, wk_ref, wv_ref, wq_ref, qT_ref, k_ref, vT_ref, *, scale):
    heads = k_ref.shape[0]
    hd = k_ref.shape[2]
    x = x_ref[...]
    xn = x * _rms_inv(x)
    hkv = (xn * gkv_ref[...]).astype(BF16)
    hq = (xn * gq_ref[...]).astype(BF16)
    k = _dot(hkv, wk_ref[...]).astype(BF16)
    for h in range(heads):
        k_ref[h] = k[:, h * hd:(h + 1) * hd]
    vT_ref[...] = _dot(hkv, wv_ref[...]).T.astype(BF16)
    qT_ref[...] = (_dot(hq, wq_ref[...]) * scale).T.astype(BF16)


def _qkv(x, gkv, gq, wk, wv, wq, heads, scale, tm):
    T, D = x.shape
    hd = wk.shape[1] // heads
    colblk = lambda n: pl.BlockSpec((n, tm), lambda i: (0, i))
    return pl.pallas_call(
        functools.partial(_qkv_kernel, scale=scale),
        out_shape=(jax.ShapeDtypeStruct((wq.shape[1], T), BF16), jax.ShapeDtypeStruct((heads, T, hd), BF16),
                   jax.ShapeDtypeStruct((wv.shape[1], T), BF16)),
        grid=(T // tm,),
        in_specs=[pl.BlockSpec((tm, D), lambda i: (i, 0)), _resident(gkv.shape), _resident(gq.shape),
                  _resident(wk.shape), _resident(wv.shape), _resident(wq.shape)],
        out_specs=(colblk(wq.shape[1]), pl.BlockSpec((heads, tm, hd), lambda i: (0, i, 0)), colblk(wv.shape[1])),
        compiler_params=_params(("parallel",)),
        name="qkv_proj",
    )(x, gkv, gq, wk, wv, wq)


def _attn_kernel(tbl_ref, lam_ref, g_ref, qT_ref, k_ref, vT_ref, o_ref,
                 qp_ref, dbias_ref, pbias_ref, m_ref, l_ref, acc_ref, *, lambda_init):
    hw, B = qT_ref.shape
    hd = hw // 2
    h = pl.program_id(0)
    qi = pl.program_id(1)
    nsub = B // LANES

    @pl.when(qi == 0)
    def _build_bias_tiles():
        far = tbl_ref[h * REL_BUCKETS + REL_BUCKETS - 1]
        a = lax.broadcasted_iota(jnp.int32, (LANES, LANES), 0)
        b = lax.broadcasted_iota(jnp.int32, (LANES, LANES), 1)

        def rel_bias(n):
            val = jnp.zeros(n.shape, F32)
            for bucket in range(REL_BUCKETS - 2, -1, -1):
                val = jnp.where(n < T5_BUCKET_LO[bucket + 1], (tbl_ref[h * REL_BUCKETS + bucket] - far) * LOG2E, val)
            return val

        e0 = jnp.where(b >= a, rel_bias(b - a), NEG)
        e1 = rel_bias(LANES + b - a)
        zero = jnp.zeros((LANES, LANES), F32)
        masked = jnp.full((LANES, LANES), NEG, F32)
        for r in range(nsub):
            for c in range(nsub):
                rs, cs = slice(r * LANES, (r + 1) * LANES), slice(c * LANES, (c + 1) * LANES)
                dbias_ref[rs, cs] = masked if c < r else e0 if c == r else e1 if c == r + 1 else zero
        for c in range(nsub):
            pbias_ref[:, c * LANES:(c + 1) * LANES] = e1 if c == 0 else zero

    qT = qT_ref[...].astype(F32)
    half = lax.broadcasted_iota(jnp.int32, qT.shape, 0) < hd
    qp_ref[0] = jnp.where(half, qT, 0.0).astype(BF16)
    qp_ref[1] = jnp.where(half, 0.0, qT).astype(BF16)
    l_ref[...] = jnp.zeros(l_ref.shape, F32)
    acc_ref[...] = jnp.zeros(acc_ref.shape, F32)

    def scores(j, kind, c):
        st = pl.multiple_of(j * B, B)
        if kind == "prev":
            cut = B - LANES
            last = pl.multiple_of(st + cut, LANES)
            return jnp.concatenate([_dot(k_ref[pl.ds(st, cut), :], qp_ref[c]),
                                    _dot(k_ref[pl.ds(last, LANES), :], qp_ref[c]) + pbias_ref[...]], axis=0)
        s = _dot(k_ref[pl.ds(st, B), :], qp_ref[c])
        return s + dbias_ref[...] if kind == "diag" else s

    def update(j, sc):
        st = pl.multiple_of(j * B, B)
        vt = vT_ref[:, pl.ds(st, B)]
        for c in range(2):
            m_old = m_ref[c]
            m_new = jnp.maximum(m_old, jnp.max(sc[c], axis=0, keepdims=True))
            p = jnp.exp2(sc[c] - m_new)
            alpha = jnp.exp2(m_old - m_new)
            l_ref[c] = alpha * l_ref[c] + jnp.sum(p, axis=0, keepdims=True)
            acc_ref[c] = alpha * acc_ref[c] + _dot(vt, p.astype(BF16))
            m_ref[c] = m_new

    def process(tiles):
        qk = lambda t: [scores(tiles[t][0], tiles[t][1], c) for c in range(2)]
        nxt = qk(0)
        for t in range(len(tiles)):
            cur, nxt = nxt, (qk(t + 1) if t + 1 < len(tiles) else None)
            update(tiles[t][0], cur)

    def process_fast(tiles):
        qk = lambda t: [scores(tiles[t][0], tiles[t][1], c) for c in range(2)]
        psum = [None, None]
        ps = [[], []]
        vts = []
        nxt = qk(0)
        for t in range(len(tiles)):
            cur, nxt = nxt, (qk(t + 1) if t + 1 < len(tiles) else None)
            vts.append(vT_ref[:, pl.ds(pl.multiple_of(tiles[t][0] * B, B), B)])
            for c in range(2):
                p = jnp.exp2(cur[c])
                part = jnp.sum(p, axis=0, keepdims=True)
                psum[c] = part if psum[c] is None else psum[c] + part
                ps[c].append(p.astype(BF16))
        vt_all = jnp.concatenate(vts, axis=1)
        for c in range(2):
            acc_ref[c] = acc_ref[c] + _dot(vt_all, jnp.concatenate(ps[c], axis=0))
            l_ref[c] = l_ref[c] + psum[c]

    n_far = jnp.maximum(qi - 1, 0)
    far = lambda first, n: [(first + t, "far") for t in range(n)]
    edge = [(qi - 1, "prev"), (qi, "diag")]

    def far_group(g, carry):
        process_fast(far(FAR_GROUP * g, FAR_GROUP))
        return carry

    n_groups = n_far // FAR_GROUP
    lax.fori_loop(0, n_groups, far_group, 0)
    done = n_groups * FAR_GROUP
    rem = n_far - done
    size = FAR_GROUP // 2
    while size >= 1:
        first = done + (rem & ~(2 * size - 1))
        has, is_last = (rem & size) != 0, (rem & (size - 1)) == 0

        @pl.when(jnp.logical_and(has, jnp.logical_not(is_last)))
        def _leftover(first=first, size=size):
            process_fast(far(first, size))

        @pl.when(jnp.logical_and(has, is_last))
        def _leftover_and_edge(first=first, size=size):
            process_fast(far(first, size) + edge)
        size //= 2

    @pl.when(jnp.logical_and(rem == 0, qi >= 1))
    def _edge_only():
        process_fast(edge)

    @pl.when(qi == 0)
    def _diagonal_only():
        process_fast([(qi, "diag")])

    def causal_edge_tiles():
        @pl.when(qi >= 1)
        def _previous_and_diagonal():
            process(edge)

        @pl.when(qi == 0)
        def _diagonal_only_slow():
            process([(qi, "diag")])

    bad = lambda x: jnp.sum(jnp.where(jnp.isfinite(x), 0.0, 1.0))
    out_of_window = jnp.logical_or(bad(acc_ref[...]) + bad(l_ref[...]) > 0.0, jnp.min(l_ref[...]) < FIXED_REF_MIN_SUM)

    @pl.when(out_of_window)
    def _redo_with_running_max():
        m_ref[...] = jnp.full(m_ref.shape, NEG, F32)
        l_ref[...] = jnp.zeros(l_ref.shape, F32)
        acc_ref[...] = jnp.zeros(acc_ref.shape, F32)
        causal_edge_tiles()

        def one_far_tile(j, carry):
            process([(j, "far")])
            return carry

        lax.fori_loop(0, n_far, one_far_tile, 0)

    lam = (jnp.exp(jnp.sum(lam_ref[0:1, :] * lam_ref[1:2, :], axis=-1, keepdims=True))
           - jnp.exp(jnp.sum(lam_ref[2:3, :] * lam_ref[3:4, :], axis=-1, keepdims=True)) + lambda_init)
    oT = acc_ref[0] / l_ref[0] - lam * (acc_ref[1] / l_ref[1])
    inv = lax.rsqrt(jnp.mean(oT * oT, axis=0, keepdims=True) + EPS)
    y = oT * inv * g_ref[...] * (1.0 - lambda_init)
    o_ref[...] = y.T.astype(o_ref.dtype)


def _attention(tbl, lam, g, qT, k, vT, lambda_init, block):
    heads, T, hw = k.shape
    dv = vT.shape[0] // heads
    B = block
    smem = pl.BlockSpec(memory_space=pltpu.SMEM)
    return pl.pallas_call(
        functools.partial(_attn_kernel, lambda_init=lambda_init),
        out_shape=jax.ShapeDtypeStruct((T, heads * dv), BF16),
        grid=(heads, T // B),
        in_specs=[smem,
                  pl.BlockSpec(lam.shape, lambda h, i: (0, 0)),
                  pl.BlockSpec(g.shape, lambda h, i: (0, 0)),
                  pl.BlockSpec((hw, B), lambda h, i: (h, i)),
                  pl.BlockSpec((None, T, hw), lambda h, i: (h, 0, 0)),
                  pl.BlockSpec((dv, T), lambda h, i: (h, 0))],
        out_specs=pl.BlockSpec((B, dv), lambda h, i: (i, h)),
        scratch_shapes=[pltpu.VMEM((2, hw, B), BF16), pltpu.VMEM((B, B), F32), pltpu.VMEM((LANES, B), F32),
                        pltpu.VMEM((2, 1, B), F32), pltpu.VMEM((2, 1, B), F32), pltpu.VMEM((2, dv, B), F32)],
        compiler_params=_params(("arbitrary", "arbitrary")),
        name="diff_attention",
    )(tbl, lam, g, qT, k, vT)


def kernel(x, a_w_in, a_w_out, a_gnorm, a_lb_logits, b_w_q, b_w_o, b_lam_q1, b_lam_k1, b_lam_q2, b_lam_k2,
           b_subln, kv_norm, kv_w, rel_table, norm_mix, norm_ffn, ffn_w_up, ffn_conv_w, ffn_conv_b,
           ffn_w_down, final_norm):
    batch, T, D = x.shape
    depth = norm_mix.shape[0]
    n_a = a_w_in.shape[0]
    qdim = b_w_q.shape[2]
    row = lambda v: v.reshape(1, -1).astype(F32)
    tm = min(512, T)
    chunk = min(HG_CHUNK, T)
    block = min(ATT_BLOCK, T)
    tbl = rel_table.astype(F32).T.reshape(-1)

    outs = []
    for bi in range(batch):
        xs = x[bi]
        qT = k3 = vT = None
        for li in range(depth):
            if li == n_a:
                j = 0
                qT, k3, vT = _qkv(xs, row(kv_norm), row(norm_mix[li]), kv_w[:, :qdim].astype(BF16),
                                  kv_w[:, qdim:].astype(BF16), b_w_q[j].astype(BF16), DA_HEADS,
                                  DA_HEAD_DIM ** -0.5 * LOG2E, tm)
            if li < n_a:
                mix, gs = _hg_mixer(xs, row(norm_mix[li]), a_w_in[li].astype(BF16), a_lb_logits.astype(F32), li,
                                    HG_HEADS, tm, chunk)
                gate, w_mix = (gs, row(a_gnorm[li])), a_w_out[li].astype(BF16)
            else:
                j = li - n_a
                if j > 0:
                    qT, _, _ = _qkv(xs, row(kv_norm), row(norm_mix[li]), kv_w[:, :qdim].astype(BF16),
                                    kv_w[:, qdim:].astype(BF16), b_w_q[j].astype(BF16), DA_HEADS,
                                    DA_HEAD_DIM ** -0.5 * LOG2E, tm)
                lambda_init = 0.8 - 0.6 * math.exp(-0.3 * li)
                lam = jnp.stack([b_lam_q1[j], b_lam_k1[j], b_lam_q2[j], b_lam_k2[j]]).astype(F32)
                mix = _attention(tbl, lam, b_subln[j].reshape(-1, 1).astype(F32), qT, k3, vT, lambda_init, block)
                gate, w_mix = None, b_w_o[j].astype(BF16)
            last = li == depth - 1
            xs = _ffn(xs, mix, gate, w_mix, row(norm_ffn[li]), ffn_w_up[li].astype(BF16), ffn_conv_w[li].astype(F32),
                      row(ffn_conv_b[li]), ffn_w_down[li].astype(BF16), row(final_norm), last, tm)
        outs.append(xs)
    return jnp.stack(outs).astype(x.dtype)
```

```python
import functools
import math

import jax
import jax.numpy as jnp
from jax import lax
from jax.experimental import pallas as pl
from jax.experimental.pallas import tpu as pltpu

F32 = jnp.float32
BF16 = jnp.bfloat16
EPS = 1e-6

LANES = 128
SUBLANES = 8
MXU_COLS = 256
VMEM_LIMIT_BYTES = 56 * 2**20

HG_HEADS = 8
DA_HEADS = 8
DA_HEAD_DIM = 64
REL_BUCKETS = 32
REL_MAX_DIST = 128
CONV_W = 3
NEG = -1e30
LOG2E = math.log2(math.e)

HG_CHUNK = 128
HG_PROJ_CHUNKS = 2
ATT_BLOCK = 512
FAR_GROUP = 8
FIXED_REF_MIN_SUM = 2.0 ** -100


def _t5_bucket_lower_bounds():
    max_exact = REL_BUCKETS // 2
    lo = list(range(max_exact))
    bucket_of = lambda n: min(
        max_exact + int(math.log(n / max_exact) / math.log(REL_MAX_DIST / max_exact) * (REL_BUCKETS - max_exact)),
        REL_BUCKETS - 1)
    n = max_exact
    for b in range(max_exact, REL_BUCKETS):
        while bucket_of(n) < b:
            n += 1
        lo.append(n)
    return tuple(lo)


T5_BUCKET_LO = _t5_bucket_lower_bounds()


def _dot(a, b):
    return jnp.dot(a, b, preferred_element_type=F32)


def _dot_nt(a, b):
    return lax.dot_general(a, b, (((1,), (1,)), ((), ())), preferred_element_type=F32)


def _dot_tn(a, b):
    return lax.dot_general(a, b, (((0,), (0,)), ((), ())), preferred_element_type=F32)


def _neg_abs(x):
    bits = pltpu.bitcast(x, jnp.uint32) | jnp.uint32(0x80000000)
    return pltpu.bitcast(bits, F32)


def _rms_inv(x):
    return lax.rsqrt(jnp.mean(x * x, axis=-1, keepdims=True) + EPS)


def _resident(shape):
    nd = len(shape)
    return pl.BlockSpec(shape, lambda *_: (0,) * nd, pipeline_mode=pl.Buffered(1))


def _params(sem):
    return pltpu.CompilerParams(dimension_semantics=sem, vmem_limit_bytes=VMEM_LIMIT_BYTES)


def _level_ref(bh, w):
    C, d = bh.shape
    n = 2 * w
    if w >= SUBLANES:
        return jnp.concatenate(
            [jnp.broadcast_to(bh[i * n + w - 1:i * n + w, :], (n, d)) for i in range(C // n)], axis=0)
    b3 = bh.reshape(C // SUBLANES, SUBLANES, d)
    pick = lambda s: jnp.broadcast_to(b3[:, s:s + 1, :], b3.shape)
    sub = lax.broadcasted_iota(jnp.int32, b3.shape, 1)
    r = pick(SUBLANES - n + w - 1)
    for s in range(SUBLANES - 2 * n, -1, -n):
        r = jnp.where(sub < s + n, pick(s + w - 1), r)
    return r.reshape(C, d)


def _hg_mixer_kernel(x_ref, g_ref, w_ref, lbl_ref, o_ref, gs_ref, st_ref, q_s, k_s, v_s, b_s, *, layer, heads, chunk):
    tm = x_ref.shape[0]
    C = chunk
    fd = o_ref.shape[1]
    dk = fd // heads
    dv = fd // heads
    n_chunks = tm // C

    @pl.when(pl.program_id(0) == 0)
    def _zero_state():
        st_ref[...] = jnp.zeros_like(st_ref)

    x = x_ref[...]
    h = (x * _rms_inv(x) * g_ref[...]).astype(BF16)
    lg = lbl_ref[...]
    ex = jnp.exp(lg - jnp.max(lg, axis=0, keepdims=True))
    sm = ex / jnp.sum(ex, axis=0, keepdims=True)
    lb = jnp.sum(sm[:layer + 1], axis=0, keepdims=True)
    log_lb = jnp.log(lb) * LOG2E
    log_1m_lb = jnp.log1p(-lb) * LOG2E

    def softplus_neg_abs(d):
        return jnp.log2(1.0 + jnp.exp2(_neg_abs(d)))

    def silu(z):
        return z / (1.0 + jnp.exp(-z))

    row = lax.broadcasted_iota(jnp.int32, (C, C), 0)
    col = lax.broadcasted_iota(jnp.int32, (C, C), 1)
    tri = jnp.where(row >= col, 1.0, 0.0).astype(BF16)
    tri3 = jnp.concatenate([tri, tri, tri], axis=1)
    levels = [1 << i for i in range(int(math.log2(C)))]
    xor = row ^ col
    below = row > col
    level_mask = [below & ((xor >> int(math.log2(w))) == 1) for w in levels]
    diag = row == col

    G = HG_PROJ_CHUNKS

    def project(i, c0):
        rows = slice(i * C, (i + G) * C)
        hc = h[rows]
        cs = slice(c0, c0 + MXU_COLS)
        wcol = lambda part: w_ref[:, part * fd + c0:part * fd + c0 + MXU_COLS]
        chunk_rows = [(i + n, slice(n * C, (n + 1) * C)) for n in range(G)]

        def stage(buf, val):
            for n, r in chunk_rows:
                buf[n, :, cs] = val[r]

        stage(q_s, silu(_dot(hc, wcol(0))).astype(BF16))
        zf = _dot(hc, wcol(1)) * LOG2E
        stage(v_s, _dot(hc, wcol(2)).astype(BF16))
        c = log_1m_lb[:, cs] + jnp.minimum(zf, 0.0) - softplus_neg_abs(zf)
        a = log_lb[:, cs]
        lf = jnp.maximum(a, c) + softplus_neg_abs(a - c)
        stage(k_s, jnp.exp2(c - zf).astype(BF16))
        gs_ref[rows, cs] = silu(_dot(hc, wcol(3))).astype(gs_ref.dtype)
        p0 = lf.astype(BF16)
        r1 = lf - p0.astype(F32)
        p1 = r1.astype(BF16)
        p2 = (r1 - p1.astype(F32)).astype(BF16)
        for n, r in chunk_rows:
            b_s[n, :, cs] = _dot(tri3, jnp.concatenate([p0[r], p1[r], p2[r]], axis=0))

    decay = lambda expo: jnp.exp2(expo).astype(BF16)

    def recur(i, hh):
        rows = slice(i * C, (i + 1) * C)
        sk = slice(hh * dk, (hh + 1) * dk)
        sv = slice(hh * dv, (hh + 1) * dv)
        bh = b_s[i, :, sk]
        qh = q_s[i, :, sk]
        kh = k_s[i, :, sk]
        vh = v_s[i, :, sv]
        b_last = bh[C - 1:C, :]
        a = jnp.where(diag, _dot_nt(qh, kh), 0.0)
        for w, mk in zip(levels, level_mask):
            e = decay(_neg_abs(bh - _level_ref(bh, w)))
            a = jnp.where(mk, _dot_nt(qh * e, kh * e), a)
        o = _dot(a.astype(BF16), vh)
        st = st_ref[hh]
        o = o + _dot_nt(qh * decay(bh), st.astype(BF16))
        st_ref[hh] = st * jnp.exp2(b_last) + _dot_tn(vh, kh * decay(b_last - bh))
        o_ref[rows, sv] = o.astype(o_ref.dtype)

    col_starts = list(range(0, fd, MXU_COLS))
    per_cols = G * heads // len(col_starts)
    for c0 in col_starts:
        project(0, c0)
    for i in range(0, n_chunks, G):
        units = [(i + n, hh) for n in range(G) for hh in range(heads)]
        for n, c0 in enumerate(col_starts):
            if i + G < n_chunks:
                project(i + G, c0)
            for ci, hh in units[n * per_cols:(n + 1) * per_cols]:
                recur(ci, hh)


def _hg_mixer(x, g, w, lb_logits, layer, heads, tm, chunk):
    T, D = x.shape
    fd = lb_logits.shape[1]
    assert w.shape[1] == 4 * fd
    row = lambda n: pl.BlockSpec((tm, n), lambda i: (i, 0))
    stage = lambda dt: pltpu.VMEM((tm // chunk, chunk, fd), dt)
    return pl.pallas_call(
        functools.partial(_hg_mixer_kernel, layer=layer, heads=heads, chunk=chunk),
        out_shape=(jax.ShapeDtypeStruct((T, fd), BF16), jax.ShapeDtypeStruct((T, fd), BF16)),
        grid=(T // tm,),
        in_specs=[row(D), _resident((1, D)), _resident(w.shape), _resident(lb_logits.shape)],
        out_specs=(row(fd), row(fd)),
        scratch_shapes=[pltpu.VMEM((heads, fd // heads, fd // heads), F32),
                        stage(BF16), stage(BF16), stage(BF16), stage(F32)],
        compiler_params=_params(("arbitrary",)),
        name="hgrn2_mixer",
    )(x, g, w, lb_logits)


def _ffn_kernel(x_ref, a_ref, *refs, gated, final_norm):
    if gated:
        gs_ref, gn_ref, *refs = refs
    wpre_ref, g_ref, wup_ref, cw_ref, cb_ref, wdn_ref, fg_ref, y_ref, u_ref, act_ref = refs
    tm = x_ref.shape[0]
    ff = wdn_ref.shape[0]
    halo = SUBLANES

    @pl.when(pl.program_id(0) == 0)
    def _zero_halo():
        u_ref[0:halo, :] = jnp.zeros((halo, u_ref.shape[1]), F32)

    halves = [slice(r0, r0 + tm // 2) for r0 in range(0, tm, tm // 2)]
    xs, hs = [], []
    for rows in halves:
        a = a_ref[rows, :]
        if gated:
            o = a.astype(F32)
            a = (o * _rms_inv(o) * gn_ref[...] * gs_ref[rows, :].astype(F32)).astype(BF16)
        x = x_ref[rows, :] + _dot(a, wpre_ref[...])
        xs.append(x)
        hs.append((x * _rms_inv(x) * g_ref[...]).astype(BF16))
    h = jnp.concatenate(hs, axis=0)
    up_cols = 512
    for c0 in range(0, 2 * ff, up_cols):
        u_ref[halo:halo + tm, c0:c0 + up_cols] = _dot(h, wup_ref[:, c0:c0 + up_cols])

    def conv(cs):
        acc = cb_ref[:, cs] + cw_ref[CONV_W - 1:CONV_W, cs] * u_ref[halo:halo + tm, cs]
        for j in range(CONV_W - 1):
            off = halo - (CONV_W - 1) + j
            acc = acc + cw_ref[j:j + 1, cs] * u_ref[off:off + tm, cs]
        return acc

    act_cols = 256
    for c0 in range(0, ff, act_cols):
        gate = conv(slice(c0, c0 + act_cols))
        val = conv(slice(ff + c0, ff + c0 + act_cols))
        act_ref[:, c0:c0 + act_cols] = (gate * jax.nn.sigmoid(gate) * val).astype(BF16)

    u_ref[0:halo, :] = u_ref[tm:tm + halo, :]
    for rows, x in zip(halves, xs):
        y = x + _dot(act_ref[rows, :], wdn_ref[...])
        if final_norm:
            y = y * _rms_inv(y) * fg_ref[...]
        y_ref[rows, :] = y


def _ffn(x, a, gate, wpre, g, wup, cw, cb, wdn, fg, final_norm, tm):
    T, D = x.shape
    ff = wdn.shape[0]
    row = lambda n: pl.BlockSpec((tm, n), lambda i: (i, 0))
    resident = [wpre, g, wup, cw, cb, wdn, fg]
    operands = [x, a] + ([gate[0], gate[1]] if gate else []) + resident
    in_specs = ([row(D), row(a.shape[1])] + ([row(gate[0].shape[1]), _resident(gate[1].shape)] if gate else [])
                + [_resident(r.shape) for r in resident])
    return pl.pallas_call(
        functools.partial(_ffn_kernel, gated=gate is not None, final_norm=final_norm),
        out_shape=jax.ShapeDtypeStruct((T, D), F32),
        grid=(T // tm,),
        in_specs=in_specs,
        out_specs=row(D),
        scratch_shapes=[pltpu.VMEM((tm + SUBLANES, 2 * ff), F32), pltpu.VMEM((tm, ff), BF16)],
        compiler_params=_params(("arbitrary",)),
        name="conv_ffn",
    )(*operands)


def _qkv_kernel(x_ref, gkv_ref, gq_ref, wk_ref, wv_ref, wq_ref, qT_ref, k_ref, vT_ref, *, scale):
    heads = k_ref.shape[0]
    hd = k_ref.shape[2]
    x = x_ref[...]
    xn = x * _rms_inv(x)
    hkv = (xn * gkv_ref[...]).astype(BF16)
    hq = (xn * gq_ref[...]).astype(BF16)
    k = _dot(hkv, wk_ref[...]).astype(BF16)
    for h in range(heads):
        k_ref[h] = k[:, h * hd:(h + 1) * hd]
    vT_ref[...] = _dot(hkv, wv_ref[...]).T.astype(BF16)
    qT_ref[...] = (_dot(hq, wq_ref[...]) * scale).T.astype(BF16)


def _qkv(x, gkv, gq, wk, wv, wq, heads, scale, tm):
    T, D = x.shape
    hd = wk.shape[1] // heads
    colblk = lambda n: pl.BlockSpec((n, tm), lambda i: (0, i))
    return pl.pallas_call(
        functools.partial(_qkv_kernel, scale=scale),
        out_shape=(jax.ShapeDtypeStruct((wq.shape[1], T), BF16), jax.ShapeDtypeStruct((heads, T, hd), BF16),
                   jax.ShapeDtypeStruct((wv.shape[1], T), BF16)),
        grid=(T // tm,),
        in_specs=[pl.BlockSpec((tm, D), lambda i: (i, 0)), _resident(gkv.shape), _resident(gq.shape),
                  _resident(wk.shape), _resident(wv.shape), _resident(wq.shape)],
        out_specs=(colblk(wq.shape[1]), pl.BlockSpec((heads, tm, hd), lambda i: (0, i, 0)), colblk(wv.shape[1])),
        compiler_params=_params(("parallel",)),
        name="qkv_proj",
    )(x, gkv, gq, wk, wv, wq)


def _attn_kernel(tbl_ref, lam_ref, g_ref, qT_ref, k_ref, vT_ref, o_ref,
                 qp_ref, dbias_ref, pbias_ref, m_ref, l_ref, acc_ref, *, lambda_init):
    hw, B = qT_ref.shape
    hd = hw // 2
    h = pl.program_id(0)
    qi = pl.program_id(1)
    nsub = B // LANES

    @pl.when(qi == 0)
    def _build_bias_tiles():
        far = tbl_ref[h * REL_BUCKETS + REL_BUCKETS - 1]
        a = lax.broadcasted_iota(jnp.int32, (LANES, LANES), 0)
        b = lax.broadcasted_iota(jnp.int32, (LANES, LANES), 1)

        def rel_bias(n):
            val = jnp.zeros(n.shape, F32)
            for bucket in range(REL_BUCKETS - 2, -1, -1):
                val = jnp.where(n < T5_BUCKET_LO[bucket + 1], (tbl_ref[h * REL_BUCKETS + bucket] - far) * LOG2E, val)
            return val

        e0 = jnp.where(b >= a, rel_bias(b - a), NEG)
        e1 = rel_bias(LANES + b - a)
        zero = jnp.zeros((LANES, LANES), F32)
        masked = jnp.full((LANES, LANES), NEG, F32)
        for r in range(nsub):
            for c in range(nsub):
                rs, cs = slice(r * LANES, (r + 1) * LANES), slice(c * LANES, (c + 1) * LANES)
                dbias_ref[rs, cs] = masked if c < r else e0 if c == r else e1 if c == r + 1 else zero
        for c in range(nsub):
            pbias_ref[:, c * LANES:(c + 1) * LANES] = e1 if c == 0 else zero

    qT = qT_ref[...].astype(F32)
    half = lax.broadcasted_iota(jnp.int32, qT.shape, 0) < hd
    qp_ref[0] = jnp.where(half, qT, 0.0).astype(BF16)
    qp_ref[1] = jnp.where(half, 0.0, qT).astype(BF16)
    l_ref[...] = jnp.zeros(l_ref.shape, F32)
    acc_ref[...] = jnp.zeros(acc_ref.shape, F32)

    def scores(j, kind, c):
        st = pl.multiple_of(j * B, B)
        if kind == "prev":
            cut = B - LANES
            last = pl.multiple_of(st + cut, LANES)
            return jnp.concatenate([_dot(k_ref[pl.ds(st, cut), :], qp_ref[c]),
                                    _dot(k_ref[pl.ds(last, LANES), :], qp_ref[c]) + pbias_ref[...]], axis=0)
        s = _dot(k_ref[pl.ds(st, B), :], qp_ref[c])
        return s + dbias_ref[...] if kind == "diag" else s

    def update(j, sc):
        st = pl.multiple_of(j * B, B)
        vt = vT_ref[:, pl.ds(st, B)]
        for c in range(2):
            m_old = m_ref[c]
            m_new = jnp.maximum(m_old, jnp.max(sc[c], axis=0, keepdims=True))
            p = jnp.exp2(sc[c] - m_new)
            alpha = jnp.exp2(m_old - m_new)
            l_ref[c] = alpha * l_ref[c] + jnp.sum(p, axis=0, keepdims=True)
            acc_ref[c] = alpha * acc_ref[c] + _dot(vt, p.astype(BF16))
            m_ref[c] = m_new

    def process(tiles):
        qk = lambda t: [scores(tiles[t][0], tiles[t][1], c) for c in range(2)]
        nxt = qk(0)
        for t in range(len(tiles)):
            cur, nxt = nxt, (qk(t + 1) if t + 1 < len(tiles) else None)
            update(tiles[t][0], cur)

    def process_fast(tiles):
        qk = lambda t: [scores(tiles[t][0], tiles[t][1], c) for c in range(2)]
        psum = [None, None]
        ps = [[], []]
        vts = []
        nxt = qk(0)
        for t in range(len(tiles)):
            cur, nxt = nxt, (qk(t + 1) if t + 1 < len(tiles) else None)
            vts.append(vT_ref[:, pl.ds(pl.multiple_of(tiles[t][0] * B, B), B)])
            for c in range(2):
                p = jnp.exp2(cur[c])
                part = jnp.sum(p, axis=0, keepdims=True)
                psum[c] = part if psum[c] is None else psum[c] + part
                ps[c].append(p.astype(BF16))
        vt_all = jnp.concatenate(vts, axis=1)
        for c in range(2):
            acc_ref[c] = acc_ref[c] + _dot(vt_all, jnp.concatenate(ps[c], axis=0))
            l_ref[c] = l_ref[c] + psum[c]

    n_far = jnp.maximum(qi - 1, 0)
    far = lambda first, n: [(first + t, "far") for t in range(n)]
    edge = [(qi - 1, "prev"), (qi, "diag")]

    def far_group(g, carry):
        process_fast(far(FAR_GROUP * g, FAR_GROUP))
        return carry

    n_groups = n_far // FAR_GROUP
    lax.fori_loop(0, n_groups, far_group, 0)
    done = n_groups * FAR_GROUP
    rem = n_far - done
    size = FAR_GROUP // 2
    while size >= 1:
        first = done + (rem & ~(2 * size - 1))
        has, is_last = (rem & size) != 0, (rem & (size - 1)) == 0

        @pl.when(jnp.logical_and(has, jnp.logical_not(is_last)))
        def _leftover(first=first, size=size):
            process_fast(far(first, size))

        @pl.when(jnp.logical_and(has, is_last))
        def _leftover_and_edge(first=first, size=size):
            process_fast(far(first, size) + edge)
        size //= 2

    @pl.when(jnp.logical_and(rem == 0, qi >= 1))
    def _edge_only():
        process_fast(edge)

    @pl.when(qi == 0)
    def _diagonal_only():
        process_fast([(qi, "diag")])

    def causal_edge_tiles():
        @pl.when(qi >= 1)
        def _previous_and_diagonal():
            process(edge)

        @pl.when(qi == 0)
        def _diagonal_only_slow():
            process([(qi, "diag")])

    def write_output():
        lam = (jnp.exp(jnp.sum(lam_ref[0:1, :] * lam_ref[1:2, :], axis=-1, keepdims=True))
               - jnp.exp(jnp.sum(lam_ref[2:3, :] * lam_ref[3:4, :], axis=-1, keepdims=True)) + lambda_init)
        oT = acc_ref[0] * (1.0 / l_ref[0]) - acc_ref[1] * (lam / l_ref[1])
        inv = lax.rsqrt(jnp.mean(oT * oT, axis=0, keepdims=True) + EPS)
        y = oT * inv * g_ref[...] * (1.0 - lambda_init)
        o_ref[...] = y.T.astype(o_ref.dtype)

    write_output()
    not_finite = lambda x: jnp.where(jnp.isfinite(x), 0.0, 1.0)
    l_low = jnp.minimum(l_ref[0], l_ref[1])
    flags = (jnp.sum(not_finite(acc_ref[0]) + not_finite(acc_ref[1]), axis=0, keepdims=True)
             + not_finite(l_ref[0]) + not_finite(l_ref[1]) + jnp.where(l_low >= FIXED_REF_MIN_SUM, 0.0, 1.0))

    @pl.when(jnp.sum(flags) > 0.0)
    def _redo_with_running_max():
        m_ref[...] = jnp.full(m_ref.shape, NEG, F32)
        l_ref[...] = jnp.zeros(l_ref.shape, F32)
        acc_ref[...] = jnp.zeros(acc_ref.shape, F32)
        causal_edge_tiles()

        def one_far_tile(j, carry):
            process([(j, "far")])
            return carry

        lax.fori_loop(0, n_far, one_far_tile, 0)
        write_output()


def _attention(tbl, lam, g, qT, k, vT, lambda_init, block):
    heads, T, hw = k.shape
    dv = vT.shape[0] // heads
    B = block
    smem = pl.BlockSpec(memory_space=pltpu.SMEM)
    return pl.pallas_call(
        functools.partial(_attn_kernel, lambda_init=lambda_init),
        out_shape=jax.ShapeDtypeStruct((T, heads * dv), BF16),
        grid=(heads, T // B),
        in_specs=[smem,
                  pl.BlockSpec(lam.shape, lambda h, i: (0, 0)),
                  pl.BlockSpec(g.shape, lambda h, i: (0, 0)),
                  pl.BlockSpec((hw, B), lambda h, i: (h, i)),
                  pl.BlockSpec((None, T, hw), lambda h, i: (h, 0, 0)),
                  pl.BlockSpec((dv, T), lambda h, i: (h, 0))],
        out_specs=pl.BlockSpec((B, dv), lambda h, i: (i, h)),
        scratch_shapes=[pltpu.VMEM((2, hw, B), BF16), pltpu.VMEM((B, B), F32), pltpu.VMEM((LANES, B), F32),
                        pltpu.VMEM((2, 1, B), F32), pltpu.VMEM((2, 1, B), F32), pltpu.VMEM((2, dv, B), F32)],
        compiler_params=_params(("arbitrary", "arbitrary")),
        name="diff_attention",
    )(tbl, lam, g, qT, k, vT)


def kernel(x, a_w_in, a_w_out, a_gnorm, a_lb_logits, b_w_q, b_w_o, b_lam_q1, b_lam_k1, b_lam_q2, b_lam_k2,
           b_subln, kv_norm, kv_w, rel_table, norm_mix, norm_ffn, ffn_w_up, ffn_conv_w, ffn_conv_b,
           ffn_w_down, final_norm):
    batch, T, D = x.shape
    depth = norm_mix.shape[0]
    n_a = a_w_in.shape[0]
    qdim = b_w_q.shape[2]
    row = lambda v: v.reshape(1, -1).astype(F32)
    tm = min(512, T)
    chunk = min(HG_CHUNK, T)
    block = min(ATT_BLOCK, T)
    tbl = rel_table.astype(F32).T.reshape(-1)

    outs = []
    for bi in range(batch):
        xs = x[bi]
        qT = k3 = vT = None
        for li in range(depth):
            if li == n_a:
                j = 0
                qT, k3, vT = _qkv(xs, row(kv_norm), row(norm_mix[li]), kv_w[:, :qdim].astype(BF16),
                                  kv_w[:, qdim:].astype(BF16), b_w_q[j].astype(BF16), DA_HEADS,
                                  DA_HEAD_DIM ** -0.5 * LOG2E, tm)
            if li < n_a:
                mix, gs = _hg_mixer(xs, row(norm_mix[li]), a_w_in[li].astype(BF16), a_lb_logits.astype(F32), li,
                                    HG_HEADS, tm, chunk)
                gate, w_mix = (gs, row(a_gnorm[li])), a_w_out[li].astype(BF16)
            else:
                j = li - n_a
                if j > 0:
                    qT, _, _ = _qkv(xs, row(kv_norm), row(norm_mix[li]), kv_w[:, :qdim].astype(BF16),
                                    kv_w[:, qdim:].astype(BF16), b_w_q[j].astype(BF16), DA_HEADS,
                                    DA_HEAD_DIM ** -0.5 * LOG2E, tm)
                lambda_init = 0.8 - 0.6 * math.exp(-0.3 * li)
                lam = jnp.stack([b_lam_q1[j], b_lam_k1[j], b_lam_q2[j], b_lam_k2[j]]).astype(F32)
                mix = _attention(tbl, lam, b_subln[j].reshape(-1, 1).astype(F32), qT, k3, vT, lambda_init, block)
                gate, w_mix = None, b_w_o[j].astype(BF16)
            last = li == depth - 1
            xs = _ffn(xs, mix, gate, w_mix, row(norm_ffn[li]), ffn_w_up[li].astype(BF16), ffn_conv_w[li].astype(F32),
                      row(ffn_conv_b[li]), ffn_w_down[li].astype(BF16), row(final_norm), last, tm)
        outs.append(xs)
    return jnp.stack(outs).astype(x.dtype)
```

```python
import functools
import math

import jax
import jax.numpy as jnp
from jax import lax
from jax.experimental import pallas as pl
from jax.experimental.pallas import tpu as pltpu

F32 = jnp.float32
BF16 = jnp.bfloat16
EPS = 1e-6

LANES = 128
SUBLANES = 8
MXU_COLS = 256
VMEM_LIMIT_BYTES = 56 * 2**20

HG_HEADS = 8
DA_HEADS = 8
DA_HEAD_DIM = 64
REL_BUCKETS = 32
REL_MAX_DIST = 128
CONV_W = 3
NEG = -1e30
LOG2E = math.log2(math.e)

HG_CHUNK = 256
HG_PROJ_CHUNKS = 2
ATT_BLOCK = 512
FAR_GROUP = 8
FIXED_REF_MIN_SUM = 2.0 ** -100


def _t5_bucket_lower_bounds():
    max_exact = REL_BUCKETS // 2
    lo = list(range(max_exact))
    bucket_of = lambda n: min(
        max_exact + int(math.log(n / max_exact) / math.log(REL_MAX_DIST / max_exact) * (REL_BUCKETS - max_exact)),
        REL_BUCKETS - 1)
    n = max_exact
    for b in range(max_exact, REL_BUCKETS):
        while bucket_of(n) < b:
            n += 1
        lo.append(n)
    return tuple(lo)


T5_BUCKET_LO = _t5_bucket_lower_bounds()


def _dot(a, b):
    return jnp.dot(a, b, preferred_element_type=F32)


def _dot_nt(a, b):
    return lax.dot_general(a, b, (((1,), (1,)), ((), ())), preferred_element_type=F32)


def _dot_tn(a, b):
    return lax.dot_general(a, b, (((0,), (0,)), ((), ())), preferred_element_type=F32)


def _neg_abs(x):
    bits = pltpu.bitcast(x, jnp.uint32) | jnp.uint32(0x80000000)
    return pltpu.bitcast(bits, F32)


def _rms_inv(x):
    return lax.rsqrt(jnp.mean(x * x, axis=-1, keepdims=True) + EPS)


def _resident(shape):
    nd = len(shape)
    return pl.BlockSpec(shape, lambda *_: (0,) * nd, pipeline_mode=pl.Buffered(1))


def _params(sem):
    return pltpu.CompilerParams(dimension_semantics=sem, vmem_limit_bytes=VMEM_LIMIT_BYTES)


def _level_ref(bh, w):
    C, d = bh.shape
    n = 2 * w
    if w >= SUBLANES:
        return jnp.concatenate(
            [jnp.broadcast_to(bh[i * n + w - 1:i * n + w, :], (n, d)) for i in range(C // n)], axis=0)
    b3 = bh.reshape(C // SUBLANES, SUBLANES, d)
    pick = lambda s: jnp.broadcast_to(b3[:, s:s + 1, :], b3.shape)
    sub = lax.broadcasted_iota(jnp.int32, b3.shape, 1)
    r = pick(SUBLANES - n + w - 1)
    for s in range(SUBLANES - 2 * n, -1, -n):
        r = jnp.where(sub < s + n, pick(s + w - 1), r)
    return r.reshape(C, d)


def _hg_mixer_kernel(x_ref, g_ref, w_ref, lbl_ref, o_ref, gs_ref, st_ref, q_s, k_s, v_s, b_s, *, layer, heads, chunk):
    tm = x_ref.shape[0]
    C = chunk
    fd = o_ref.shape[1]
    dk = fd // heads
    dv = fd // heads
    n_chunks = tm // C

    @pl.when(pl.program_id(0) == 0)
    def _zero_state():
        st_ref[...] = jnp.zeros_like(st_ref)

    x = x_ref[...]
    h = (x * _rms_inv(x) * g_ref[...]).astype(BF16)
    lg = lbl_ref[...]
    ex = jnp.exp(lg - jnp.max(lg, axis=0, keepdims=True))
    sm = ex / jnp.sum(ex, axis=0, keepdims=True)
    lb = jnp.sum(sm[:layer + 1], axis=0, keepdims=True)
    log_lb = jnp.log(lb) * LOG2E
    log_1m_lb = jnp.log1p(-lb) * LOG2E

    def softplus_neg_abs(d):
        return jnp.log2(1.0 + jnp.exp2(_neg_abs(d)))

    def silu(z):
        return z / (1.0 + jnp.exp(-z))

    row = lax.broadcasted_iota(jnp.int32, (C, C), 0)
    col = lax.broadcasted_iota(jnp.int32, (C, C), 1)
    tri = jnp.where(row >= col, 1.0, 0.0).astype(BF16)
    tri3 = jnp.concatenate([tri, tri, tri], axis=1)
    levels = [1 << i for i in range(int(math.log2(C)))]
    xor = row ^ col
    below = row > col
    level_mask = [below & ((xor >> int(math.log2(w))) == 1) for w in levels]
    diag = row == col

    G = HG_PROJ_CHUNKS

    def project(i, c0):
        rows = slice(i * C, (i + G) * C)
        hc = h[rows]
        cs = slice(c0, c0 + MXU_COLS)
        wcol = lambda part: w_ref[:, part * fd + c0:part * fd + c0 + MXU_COLS]
        chunk_rows = [(i + n, slice(n * C, (n + 1) * C)) for n in range(G)]

        def stage(buf, val):
            for n, r in chunk_rows:
                buf[n, :, cs] = val[r]

        stage(q_s, silu(_dot(hc, wcol(0))).astype(BF16))
        zf = _dot(hc, wcol(1)) * LOG2E
        stage(v_s, _dot(hc, wcol(2)).astype(BF16))
        c = log_1m_lb[:, cs] + jnp.minimum(zf, 0.0) - softplus_neg_abs(zf)
        a = log_lb[:, cs]
        lf = jnp.maximum(a, c) + softplus_neg_abs(a - c)
        stage(k_s, jnp.exp2(c - zf).astype(BF16))
        gs_ref[rows, cs] = silu(_dot(hc, wcol(3))).astype(gs_ref.dtype)
        p0 = lf.astype(BF16)
        r1 = lf - p0.astype(F32)
        p1 = r1.astype(BF16)
        p2 = (r1 - p1.astype(F32)).astype(BF16)
        for n, r in chunk_rows:
            b_s[n, :, cs] = _dot(tri3, jnp.concatenate([p0[r], p1[r], p2[r]], axis=0))

    decay = lambda expo: jnp.exp2(expo).astype(BF16)

    def recur(i, hh):
        rows = slice(i * C, (i + 1) * C)
        sk = slice(hh * dk, (hh + 1) * dk)
        sv = slice(hh * dv, (hh + 1) * dv)
        bh = b_s[i, :, sk]
        qh = q_s[i, :, sk]
        kh = k_s[i, :, sk]
        vh = v_s[i, :, sv]
        b_last = bh[C - 1:C, :]
        a = jnp.where(diag, _dot_nt(qh, kh), 0.0)
        for w, mk in zip(levels, level_mask):
            e = decay(_neg_abs(bh - _level_ref(bh, w)))
            a = jnp.where(mk, _dot_nt(qh * e, kh * e), a)
        o = _dot(a.astype(BF16), vh)
        st = st_ref[hh]
        o = o + _dot_nt(qh * decay(bh), st.astype(BF16))
        st_ref[hh] = st * jnp.exp2(b_last) + _dot_tn(vh, kh * decay(b_last - bh))
        o_ref[rows, sv] = o.astype(o_ref.dtype)

    col_starts = list(range(0, fd, MXU_COLS))
    per_cols = G * heads // len(col_starts)
    for c0 in col_starts:
        project(0, c0)
    for i in range(0, n_chunks, G):
        units = [(i + n, hh) for n in range(G) for hh in range(heads)]
        for n, c0 in enumerate(col_starts):
            if i + G < n_chunks:
                project(i + G, c0)
            for ci, hh in units[n * per_cols:(n + 1) * per_cols]:
                recur(ci, hh)


def _hg_mixer(x, g, w, lb_logits, layer, heads, tm, chunk):
    T, D = x.shape
    fd = lb_logits.shape[1]
    assert w.shape[1] == 4 * fd
    row = lambda n: pl.BlockSpec((tm, n), lambda i: (i, 0))
    stage = lambda dt: pltpu.VMEM((tm // chunk, chunk, fd), dt)
    return pl.pallas_call(
        functools.partial(_hg_mixer_kernel, layer=layer, heads=heads, chunk=chunk),
        out_shape=(jax.ShapeDtypeStruct((T, fd), BF16), jax.ShapeDtypeStruct((T, fd), BF16)),
        grid=(T // tm,),
        in_specs=[row(D), _resident((1, D)), _resident(w.shape), _resident(lb_logits.shape)],
        out_specs=(row(fd), row(fd)),
        scratch_shapes=[pltpu.VMEM((heads, fd // heads, fd // heads), F32),
                        stage(BF16), stage(BF16), stage(BF16), stage(F32)],
        compiler_params=_params(("arbitrary",)),
        name="hgrn2_mixer",
    )(x, g, w, lb_logits)


def _ffn_kernel(x_ref, a_ref, *refs, gated, final_norm):
    if gated:
        gs_ref, gn_ref, *refs = refs
    wpre_ref, g_ref, wup_ref, cw_ref, cb_ref, wdn_ref, fg_ref, y_ref, u_ref, act_ref = refs
    tm = x_ref.shape[0]
    ff = wdn_ref.shape[0]
    halo = SUBLANES

    @pl.when(pl.program_id(0) == 0)
    def _zero_halo():
        u_ref[0:halo, :] = jnp.zeros((halo, u_ref.shape[1]), F32)

    halves = [slice(r0, r0 + tm // 2) for r0 in range(0, tm, tm // 2)]
    xs, hs = [], []
    for rows in halves:
        a = a_ref[rows, :]
        if gated:
            o = a.astype(F32)
            a = (o * _rms_inv(o) * gn_ref[...] * gs_ref[rows, :].astype(F32)).astype(BF16)
        x = x_ref[rows, :] + _dot(a, wpre_ref[...])
        xs.append(x)
        hs.append((x * _rms_inv(x) * g_ref[...]).astype(BF16))
    h = jnp.concatenate(hs, axis=0)
    up_cols = 512
    for c0 in range(0, 2 * ff, up_cols):
        u_ref[halo:halo + tm, c0:c0 + up_cols] = _dot(h, wup_ref[:, c0:c0 + up_cols])

    def conv(cs):
        acc = cb_ref[:, cs] + cw_ref[CONV_W - 1:CONV_W, cs] * u_ref[halo:halo + tm, cs]
        for j in range(CONV_W - 1):
            off = halo - (CONV_W - 1) + j
            acc = acc + cw_ref[j:j + 1, cs] * u_ref[off:off + tm, cs]
        return acc

    act_cols = 256
    for c0 in range(0, ff, act_cols):
        gate = conv(slice(c0, c0 + act_cols))
        val = conv(slice(ff + c0, ff + c0 + act_cols))
        act_ref[:, c0:c0 + act_cols] = (gate * jax.nn.sigmoid(gate) * val).astype(BF16)

    u_ref[0:halo, :] = u_ref[tm:tm + halo, :]
    for rows, x in zip(halves, xs):
        y = x + _dot(act_ref[rows, :], wdn_ref[...])
        if final_norm:
            y = y * _rms_inv(y) * fg_ref[...]
        y_ref[rows, :] = y


def _ffn(x, a, gate, wpre, g, wup, cw, cb, wdn, fg, final_norm, tm):
    T, D = x.shape
    ff = wdn.shape[0]
    row = lambda n: pl.BlockSpec((tm, n), lambda i: (i, 0))
    resident = [wpre, g, wup, cw, cb, wdn, fg]
    operands = [x, a] + ([gate[0], gate[1]] if gate else []) + resident
    in_specs = ([row(D), row(a.shape[1])] + ([row(gate[0].shape[1]), _resident(gate[1].shape)] if gate else [])
                + [_resident(r.shape) for r in resident])
    return pl.pallas_call(
        functools.partial(_ffn_kernel, gated=gate is not None, final_norm=final_norm),
        out_shape=jax.ShapeDtypeStruct((T, D), F32),
        grid=(T // tm,),
        in_specs=in_specs,
        out_specs=row(D),
        scratch_shapes=[pltpu.VMEM((tm + SUBLANES, 2 * ff), F32), pltpu.VMEM((tm, ff), BF16)],
        compiler_params=_params(("arbitrary",)),
        name="conv_ffn",
    )(*operands)


def _qkv_kernel(x_ref, gkv_ref, gq_ref, wk_ref, wv_ref, wq_ref, qT_ref, k_ref, vT_ref, *, scale):
    heads = k_ref.shape[0]
    hd = k_ref.shape[2]
    x = x_ref[...]
    xn = x * _rms_inv(x)
    hkv = (xn * gkv_ref[...]).astype(BF16)
    hq = (xn * gq_ref[...]).astype(BF16)
    k = _dot(hkv, wk_ref[...]).astype(BF16)
    for h in range(heads):
        k_ref[h] = k[:, h * hd:(h + 1) * hd]
    vT_ref[...] = _dot(hkv, wv_ref[...]).T.astype(BF16)
    qT_ref[...] = (_dot(hq, wq_ref[...]) * scale).T.astype(BF16)


def _qkv(x, gkv, gq, wk, wv, wq, heads, scale, tm):
    T, D = x.shape
    hd = wk.shape[1] // heads
    colblk = lambda n: pl.BlockSpec((n, tm), lambda i: (0, i))
    return pl.pallas_call(
        functools.partial(_qkv_kernel, scale=scale),
        out_shape=(jax.ShapeDtypeStruct((wq.shape[1], T), BF16), jax.ShapeDtypeStruct((heads, T, hd), BF16),
                   jax.ShapeDtypeStruct((wv.shape[1], T), BF16)),
        grid=(T // tm,),
        in_specs=[pl.BlockSpec((tm, D), lambda i: (i, 0)), _resident(gkv.shape), _resident(gq.shape),
                  _resident(wk.shape), _resident(wv.shape), _resident(wq.shape)],
        out_specs=(colblk(wq.shape[1]), pl.BlockSpec((heads, tm, hd), lambda i: (0, i, 0)), colblk(wv.shape[1])),
        compiler_params=_params(("parallel",)),
        name="qkv_proj",
    )(x, gkv, gq, wk, wv, wq)


def _attn_kernel(tbl_ref, lam_ref, g_ref, qT_ref, k_ref, vT_ref, o_ref,
                 qp_ref, dbias_ref, pbias_ref, m_ref, l_ref, acc_ref, *, lambda_init):
    hw, B = qT_ref.shape
    hd = hw // 2
    h = pl.program_id(0)
    qi = pl.program_id(1)
    nsub = B // LANES

    @pl.when(qi == 0)
    def _build_bias_tiles():
        far = tbl_ref[h * REL_BUCKETS + REL_BUCKETS - 1]
        a = lax.broadcasted_iota(jnp.int32, (LANES, LANES), 0)
        b = lax.broadcasted_iota(jnp.int32, (LANES, LANES), 1)

        def rel_bias(n):
            val = jnp.zeros(n.shape, F32)
            for bucket in range(REL_BUCKETS - 2, -1, -1):
                val = jnp.where(n < T5_BUCKET_LO[bucket + 1], (tbl_ref[h * REL_BUCKETS + bucket] - far) * LOG2E, val)
            return val

        e0 = jnp.where(b >= a, rel_bias(b - a), NEG)
        e1 = rel_bias(LANES + b - a)
        zero = jnp.zeros((LANES, LANES), F32)
        masked = jnp.full((LANES, LANES), NEG, F32)
        for r in range(nsub):
            for c in range(nsub):
                rs, cs = slice(r * LANES, (r + 1) * LANES), slice(c * LANES, (c + 1) * LANES)
                dbias_ref[rs, cs] = masked if c < r else e0 if c == r else e1 if c == r + 1 else zero
        for c in range(nsub):
            pbias_ref[:, c * LANES:(c + 1) * LANES] = e1 if c == 0 else zero

    qT = qT_ref[...].astype(F32)
    half = lax.broadcasted_iota(jnp.int32, qT.shape, 0) < hd
    qp_ref[0] = jnp.where(half, qT, 0.0).astype(BF16)
    qp_ref[1] = jnp.where(half, 0.0, qT).astype(BF16)
    l_ref[...] = jnp.zeros(l_ref.shape, F32)
    acc_ref[...] = jnp.zeros(acc_ref.shape, F32)

    def scores(j, kind, c):
        st = pl.multiple_of(j * B, B)
        if kind == "prev":
            cut = B - LANES
            last = pl.multiple_of(st + cut, LANES)
            return jnp.concatenate([_dot(k_ref[pl.ds(st, cut), :], qp_ref[c]),
                                    _dot(k_ref[pl.ds(last, LANES), :], qp_ref[c]) + pbias_ref[...]], axis=0)
        s = _dot(k_ref[pl.ds(st, B), :], qp_ref[c])
        return s + dbias_ref[...] if kind == "diag" else s

    def update(j, sc):
        st = pl.multiple_of(j * B, B)
        vt = vT_ref[:, pl.ds(st, B)]
        for c in range(2):
            m_old = m_ref[c]
            m_new = jnp.maximum(m_old, jnp.max(sc[c], axis=0, keepdims=True))
            p = jnp.exp2(sc[c] - m_new)
            alpha = jnp.exp2(m_old - m_new)
            l_ref[c] = alpha * l_ref[c] + jnp.sum(p, axis=0, keepdims=True)
            acc_ref[c] = alpha * acc_ref[c] + _dot(vt, p.astype(BF16))
            m_ref[c] = m_new

    def process(tiles):
        qk = lambda t: [scores(tiles[t][0], tiles[t][1], c) for c in range(2)]
        nxt = qk(0)
        for t in range(len(tiles)):
            cur, nxt = nxt, (qk(t + 1) if t + 1 < len(tiles) else None)
            update(tiles[t][0], cur)

    def process_fast(tiles):
        qk = lambda t: [scores(tiles[t][0], tiles[t][1], c) for c in range(2)]
        psum = [None, None]
        ps = [[], []]
        vts = []
        nxt = qk(0)
        for t in range(len(tiles)):
            cur, nxt = nxt, (qk(t + 1) if t + 1 < len(tiles) else None)
            vts.append(vT_ref[:, pl.ds(pl.multiple_of(tiles[t][0] * B, B), B)])
            for c in range(2):
                p = jnp.exp2(cur[c])
                part = jnp.sum(p, axis=0, keepdims=True)
                psum[c] = part if psum[c] is None else psum[c] + part
                ps[c].append(p.astype(BF16))
        vt_all = jnp.concatenate(vts, axis=1)
        for c in range(2):
            acc_ref[c] = acc_ref[c] + _dot(vt_all, jnp.concatenate(ps[c], axis=0))
            l_ref[c] = l_ref[c] + psum[c]

    n_far = jnp.maximum(qi - 1, 0)
    far = lambda first, n: [(first + t, "far") for t in range(n)]
    edge = [(qi - 1, "prev"), (qi, "diag")]

    def far_group(g, carry):
        process_fast(far(FAR_GROUP * g, FAR_GROUP))
        return carry

    n_groups = n_far // FAR_GROUP
    lax.fori_loop(0, n_groups, far_group, 0)
    done = n_groups * FAR_GROUP
    rem = n_far - done
    size = FAR_GROUP // 2
    while size >= 1:
        first = done + (rem & ~(2 * size - 1))
        has, is_last = (rem & size) != 0, (rem & (size - 1)) == 0

        @pl.when(jnp.logical_and(has, jnp.logical_not(is_last)))
        def _leftover(first=first, size=size):
            process_fast(far(first, size))

        @pl.when(jnp.logical_and(has, is_last))
        def _leftover_and_edge(first=first, size=size):
            process_fast(far(first, size) + edge)
        size //= 2

    @pl.when(jnp.logical_and(rem == 0, qi >= 1))
    def _edge_only():
        process_fast(edge)

    @pl.when(qi == 0)
    def _diagonal_only():
        process_fast([(qi, "diag")])

    def causal_edge_tiles():
        @pl.when(qi >= 1)
        def _previous_and_diagonal():
            process(edge)

        @pl.when(qi == 0)
        def _diagonal_only_slow():
            process([(qi, "diag")])

    def write_output():
        lam = (jnp.exp(jnp.sum(lam_ref[0:1, :] * lam_ref[1:2, :], axis=-1, keepdims=True))
               - jnp.exp(jnp.sum(lam_ref[2:3, :] * lam_ref[3:4, :], axis=-1, keepdims=True)) + lambda_init)
        oT = acc_ref[0] * (1.0 / l_ref[0]) - acc_ref[1] * (lam / l_ref[1])
        inv = lax.rsqrt(jnp.mean(oT * oT, axis=0, keepdims=True) + EPS)
        y = oT * inv * g_ref[...] * (1.0 - lambda_init)
        o_ref[...] = y.T.astype(o_ref.dtype)

    write_output()
    not_finite = lambda x: jnp.where(jnp.isfinite(x), 0.0, 1.0)
    l_low = jnp.minimum(l_ref[0], l_ref[1])
    flags = (jnp.sum(not_finite(acc_ref[0]) + not_finite(acc_ref[1]), axis=0, keepdims=True)
             + not_finite(l_ref[0]) + not_finite(l_ref[1]) + jnp.where(l_low >= FIXED_REF_MIN_SUM, 0.0, 1.0))

    @pl.when(jnp.sum(flags) > 0.0)
    def _redo_with_running_max():
        m_ref[...] = jnp.full(m_ref.shape, NEG, F32)
        l_ref[...] = jnp.zeros(l_ref.shape, F32)
        acc_ref[...] = jnp.zeros(acc_ref.shape, F32)
        causal_edge_tiles()

        def one_far_tile(j, carry):
            process([(j, "far")])
            return carry

        lax.fori_loop(0, n_far, one_far_tile, 0)
        write_output()


def _attention(tbl, lam, g, qT, k, vT, lambda_init, block):
    heads, T, hw = k.shape
    dv = vT.shape[0] // heads
    B = block
    smem = pl.BlockSpec(memory_space=pltpu.SMEM)
    return pl.pallas_call(
        functools.partial(_attn_kernel, lambda_init=lambda_init),
        out_shape=jax.ShapeDtypeStruct((T, heads * dv), BF16),
        grid=(heads, T // B),
        in_specs=[smem,
                  pl.BlockSpec(lam.shape, lambda h, i: (0, 0)),
                  pl.BlockSpec(g.shape, lambda h, i: (0, 0)),
                  pl.BlockSpec((hw, B), lambda h, i: (h, i)),
                  pl.BlockSpec((None, T, hw), lambda h, i: (h, 0, 0)),
                  pl.BlockSpec((dv, T), lambda h, i: (h, 0))],
        out_specs=pl.BlockSpec((B, dv), lambda h, i: (i, h)),
        scratch_shapes=[pltpu.VMEM((2, hw, B), BF16), pltpu.VMEM((B, B), F32), pltpu.VMEM((LANES, B), F32),
                        pltpu.VMEM((2, 1, B), F32), pltpu.VMEM((2, 1, B), F32), pltpu.VMEM((2, dv, B), F32)],
        compiler_params=_params(("arbitrary", "arbitrary")),
        name="diff_attention",
    )(tbl, lam, g, qT, k, vT)


def kernel(x, a_w_in, a_w_out, a_gnorm, a_lb_logits, b_w_q, b_w_o, b_lam_q1, b_lam_k1, b_lam_q2, b_lam_k2,
           b_subln, kv_norm, kv_w, rel_table, norm_mix, norm_ffn, ffn_w_up, ffn_conv_w, ffn_conv_b,
           ffn_w_down, final_norm):
    batch, T, D = x.shape
    depth = norm_mix.shape[0]
    n_a = a_w_in.shape[0]
    qdim = b_w_q.shape[2]
    row = lambda v: v.reshape(1, -1).astype(F32)
    tm = min(512, T)
    chunk = min(HG_CHUNK, T)
    block = min(ATT_BLOCK, T)
    tbl = rel_table.astype(F32).T.reshape(-1)

    outs = []
    for bi in range(batch):
        xs = x[bi]
        qT = k3 = vT = None
        for li in range(depth):
            if li == n_a:
                j = 0
                qT, k3, vT = _qkv(xs, row(kv_norm), row(norm_mix[li]), kv_w[:, :qdim].astype(BF16),
                                  kv_w[:, qdim:].astype(BF16), b_w_q[j].astype(BF16), DA_HEADS,
                                  DA_HEAD_DIM ** -0.5 * LOG2E, tm)
            if li < n_a:
                mix, gs = _hg_mixer(xs, row(norm_mix[li]), a_w_in[li].astype(BF16), a_lb_logits.astype(F32), li,
                                    HG_HEADS, tm, chunk)
                gate, w_mix = (gs, row(a_gnorm[li])), a_w_out[li].astype(BF16)
            else:
                j = li - n_a
                if j > 0:
                    qT, _, _ = _qkv(xs, row(kv_norm), row(norm_mix[li]), kv_w[:, :qdim].astype(BF16),
                                    kv_w[:, qdim:].astype(BF16), b_w_q[j].astype(BF16), DA_HEADS,
                                    DA_HEAD_DIM ** -0.5 * LOG2E, tm)
                lambda_init = 0.8 - 0.6 * math.exp(-0.3 * li)
                lam = jnp.stack([b_lam_q1[j], b_lam_k1[j], b_lam_q2[j], b_lam_k2[j]]).astype(F32)
                mix = _attention(tbl, lam, b_subln[j].reshape(-1, 1).astype(F32), qT, k3, vT, lambda_init, block)
                gate, w_mix = None, b_w_o[j].astype(BF16)
            last = li == depth - 1
            xs = _ffn(xs, mix, gate, w_mix, row(norm_ffn[li]), ffn_w_up[li].astype(BF16), ffn_conv_w[li].astype(F32),
                      row(ffn_conv_b[li]), ffn_w_down[li].astype(BF16), row(final_norm), last, tm)
        outs.append(xs)
    return jnp.stack(outs).astype(x.dtype)
```

```python
import functools
import math

import jax
import jax.numpy as jnp
from jax import lax
from jax.experimental import pallas as pl
from jax.experimental.pallas import tpu as pltpu

F32 = jnp.float32
BF16 = jnp.bfloat16
EPS = 1e-6

LANES = 128
SUBLANES = 8
MXU_COLS = 256
VMEM_LIMIT_BYTES = 56 * 2**20

HG_HEADS = 8
DA_HEADS = 8
DA_HEAD_DIM = 64
REL_BUCKETS = 32
REL_MAX_DIST = 128
CONV_W = 3
NEG = -1e30
LOG2E = math.log2(math.e)

HG_CHUNK = 256
HG_PROJ_CHUNKS = 2
ATT_BLOCK = 512
FAR_GROUP = 8
CAST_ROWS = 2 * SUBLANES
FIXED_REF_MIN_SUM = 2.0 ** -100


def _t5_bucket_lower_bounds():
    max_exact = REL_BUCKETS // 2
    lo = list(range(max_exact))
    bucket_of = lambda n: min(
        max_exact + int(math.log(n / max_exact) / math.log(REL_MAX_DIST / max_exact) * (REL_BUCKETS - max_exact)),
        REL_BUCKETS - 1)
    n = max_exact
    for b in range(max_exact, REL_BUCKETS):
        while bucket_of(n) < b:
            n += 1
        lo.append(n)
    return tuple(lo)


T5_BUCKET_LO = _t5_bucket_lower_bounds()


def _dot(a, b):
    return jnp.dot(a, b, preferred_element_type=F32)


def _dot_nt(a, b):
    return lax.dot_general(a, b, (((1,), (1,)), ((), ())), preferred_element_type=F32)


def _dot_tn(a, b):
    return lax.dot_general(a, b, (((0,), (0,)), ((), ())), preferred_element_type=F32)


def _neg_abs(x):
    bits = pltpu.bitcast(x, jnp.uint32) | jnp.uint32(0x80000000)
    return pltpu.bitcast(bits, F32)


def _rms_inv(x):
    return lax.rsqrt(jnp.mean(x * x, axis=-1, keepdims=True) + EPS)


def _resident(shape):
    nd = len(shape)
    return pl.BlockSpec(shape, lambda *_: (0,) * nd, pipeline_mode=pl.Buffered(1))


def _params(sem):
    return pltpu.CompilerParams(dimension_semantics=sem, vmem_limit_bytes=VMEM_LIMIT_BYTES)


def _level_ref(bh, w):
    C, d = bh.shape
    n = 2 * w
    if w >= SUBLANES:
        return jnp.concatenate(
            [jnp.broadcast_to(bh[i * n + w - 1:i * n + w, :], (n, d)) for i in range(C // n)], axis=0)
    b3 = bh.reshape(C // SUBLANES, SUBLANES, d)
    pick = lambda s: jnp.broadcast_to(b3[:, s:s + 1, :], b3.shape)
    sub = lax.broadcasted_iota(jnp.int32, b3.shape, 1)
    r = pick(SUBLANES - n + w - 1)
    for s in range(SUBLANES - 2 * n, -1, -n):
        r = jnp.where(sub < s + n, pick(s + w - 1), r)
    return r.reshape(C, d)


def _hg_mixer_kernel(x_ref, g_ref, w_ref, lbl_ref, o_ref, gs_ref, st_ref, q_s, k_s, v_s, b_s, *, layer, heads, chunk):
    tm = x_ref.shape[0]
    C = chunk
    fd = o_ref.shape[1]
    dk = fd // heads
    dv = fd // heads
    n_chunks = tm // C

    @pl.when(pl.program_id(0) == 0)
    def _zero_state():
        st_ref[...] = jnp.zeros_like(st_ref)

    x = x_ref[...]
    h = (x * _rms_inv(x) * g_ref[...]).astype(BF16)
    lg = lbl_ref[...]
    ex = jnp.exp(lg - jnp.max(lg, axis=0, keepdims=True))
    sm = ex / jnp.sum(ex, axis=0, keepdims=True)
    lb = jnp.sum(sm[:layer + 1], axis=0, keepdims=True)
    log_lb = jnp.log(lb) * LOG2E
    log_1m_lb = jnp.log1p(-lb) * LOG2E

    def softplus_neg_abs(d):
        return jnp.log2(1.0 + jnp.exp2(_neg_abs(d)))

    def silu(z):
        return z / (1.0 + jnp.exp(-z))

    row = lax.broadcasted_iota(jnp.int32, (C, C), 0)
    col = lax.broadcasted_iota(jnp.int32, (C, C), 1)
    tri = jnp.where(row >= col, 1.0, 0.0).astype(BF16)
    tri3 = jnp.concatenate([tri, tri, tri], axis=1)
    levels = [1 << i for i in range(int(math.log2(C)))]
    xor = row ^ col
    below = row > col
    level_mask = [below & ((xor >> int(math.log2(w))) == 1) for w in levels]
    diag = row == col

    G = HG_PROJ_CHUNKS

    def project(i, c0):
        rows = slice(i * C, (i + G) * C)
        hc = h[rows]
        cs = slice(c0, c0 + MXU_COLS)
        wcol = lambda part: w_ref[:, part * fd + c0:part * fd + c0 + MXU_COLS]
        chunk_rows = [(i + n, slice(n * C, (n + 1) * C)) for n in range(G)]

        def stage(buf, val):
            for n, r in chunk_rows:
                buf[n, :, cs] = val[r]

        stage(q_s, silu(_dot(hc, wcol(0))).astype(BF16))
        zf = _dot(hc, wcol(1)) * LOG2E
        stage(v_s, _dot(hc, wcol(2)).astype(BF16))
        c = log_1m_lb[:, cs] + jnp.minimum(zf, 0.0) - softplus_neg_abs(zf)
        a = log_lb[:, cs]
        lf = jnp.maximum(a, c) + softplus_neg_abs(a - c)
        stage(k_s, jnp.exp2(c - zf).astype(BF16))
        gs_ref[rows, cs] = silu(_dot(hc, wcol(3))).astype(gs_ref.dtype)
        p0 = lf.astype(BF16)
        r1 = lf - p0.astype(F32)
        p1 = r1.astype(BF16)
        p2 = (r1 - p1.astype(F32)).astype(BF16)
        for n, r in chunk_rows:
            b_s[n, :, cs] = _dot(tri3, jnp.concatenate([p0[r], p1[r], p2[r]], axis=0))

    decay = lambda expo: jnp.exp2(expo).astype(BF16)

    def recur(i, hh):
        rows = slice(i * C, (i + 1) * C)
        sk = slice(hh * dk, (hh + 1) * dk)
        sv = slice(hh * dv, (hh + 1) * dv)
        bh = b_s[i, :, sk]
        qh = q_s[i, :, sk]
        kh = k_s[i, :, sk]
        vh = v_s[i, :, sv]
        b_last = bh[C - 1:C, :]
        a = jnp.where(diag, _dot_nt(qh, kh), 0.0)
        for w, mk in zip(levels, level_mask):
            e = decay(_neg_abs(bh - _level_ref(bh, w)))
            a = jnp.where(mk, _dot_nt(qh * e, kh * e), a)
        o = _dot(a.astype(BF16), vh)
        st = st_ref[hh]
        o = o + _dot_nt(qh * decay(bh), st.astype(BF16))
        st_ref[hh] = st * jnp.exp2(b_last) + _dot_tn(vh, kh * decay(b_last - bh))
        o_ref[rows, sv] = o.astype(o_ref.dtype)

    col_starts = list(range(0, fd, MXU_COLS))
    per_cols = G * heads // len(col_starts)
    for c0 in col_starts:
        project(0, c0)
    for i in range(0, n_chunks, G):
        units = [(i + n, hh) for n in range(G) for hh in range(heads)]
        for n, c0 in enumerate(col_starts):
            if i + G < n_chunks:
                project(i + G, c0)
            for ci, hh in units[n * per_cols:(n + 1) * per_cols]:
                recur(ci, hh)


def _hg_mixer(x, g, w, lb_logits, layer, heads, tm, chunk):
    T, D = x.shape
    fd = lb_logits.shape[1]
    assert w.shape[1] == 4 * fd
    row = lambda n: pl.BlockSpec((tm, n), lambda i: (i, 0))
    stage = lambda dt: pltpu.VMEM((tm // chunk, chunk, fd), dt)
    return pl.pallas_call(
        functools.partial(_hg_mixer_kernel, layer=layer, heads=heads, chunk=chunk),
        out_shape=(jax.ShapeDtypeStruct((T, fd), BF16), jax.ShapeDtypeStruct((T, fd), BF16)),
        grid=(T // tm,),
        in_specs=[row(D), _resident((1, D)), _resident(w.shape), _resident(lb_logits.shape)],
        out_specs=(row(fd), row(fd)),
        scratch_shapes=[pltpu.VMEM((heads, fd // heads, fd // heads), F32),
                        stage(BF16), stage(BF16), stage(BF16), stage(F32)],
        compiler_params=_params(("arbitrary",)),
        name="hgrn2_mixer",
    )(x, g, w, lb_logits)


def _ffn_kernel(x_ref, a_ref, *refs, gated, final_norm):
    if gated:
        gs_ref, gn_ref, *refs = refs
    wpre_ref, g_ref, wup_ref, cw_ref, cb_ref, wdn_ref, fg_ref, y_ref, u_ref, act_ref = refs
    tm = x_ref.shape[0]
    ff = wdn_ref.shape[0]
    halo = SUBLANES

    @pl.when(pl.program_id(0) == 0)
    def _zero_halo():
        u_ref[0:halo, :] = jnp.zeros((halo, u_ref.shape[1]), F32)

    halves = [slice(r0, r0 + tm // 2) for r0 in range(0, tm, tm // 2)]
    xs, hs = [], []
    for rows in halves:
        a = a_ref[rows, :]
        if gated:
            o = a.astype(F32)
            a = (o * _rms_inv(o) * gn_ref[...] * gs_ref[rows, :].astype(F32)).astype(BF16)
        x = x_ref[rows, :] + _dot(a, wpre_ref[...])
        xs.append(x)
        hs.append((x * _rms_inv(x) * g_ref[...]).astype(BF16))
    h = jnp.concatenate(hs, axis=0)
    up_cols = 512
    for c0 in range(0, 2 * ff, up_cols):
        u_ref[halo:halo + tm, c0:c0 + up_cols] = _dot(h, wup_ref[:, c0:c0 + up_cols])

    def conv(cs):
        acc = cb_ref[:, cs] + cw_ref[CONV_W - 1:CONV_W, cs] * u_ref[halo:halo + tm, cs]
        for j in range(CONV_W - 1):
            off = halo - (CONV_W - 1) + j
            acc = acc + cw_ref[j:j + 1, cs] * u_ref[off:off + tm, cs]
        return acc

    act_cols = 256
    for c0 in range(0, ff, act_cols):
        gate = conv(slice(c0, c0 + act_cols))
        val = conv(slice(ff + c0, ff + c0 + act_cols))
        act_ref[:, c0:c0 + act_cols] = (gate * jax.nn.sigmoid(gate) * val).astype(BF16)

    u_ref[0:halo, :] = u_ref[tm:tm + halo, :]
    for rows, x in zip(halves, xs):
        y = x + _dot(act_ref[rows, :], wdn_ref[...])
        if final_norm:
            y = y * _rms_inv(y) * fg_ref[...]
        y_ref[rows, :] = y


def _ffn(x, a, gate, wpre, g, wup, cw, cb, wdn, fg, final_norm, tm):
    T, D = x.shape
    ff = wdn.shape[0]
    row = lambda n: pl.BlockSpec((tm, n), lambda i: (i, 0))
    resident = [wpre, g, wup, cw, cb, wdn, fg]
    operands = [x, a] + ([gate[0], gate[1]] if gate else []) + resident
    in_specs = ([row(D), row(a.shape[1])] + ([row(gate[0].shape[1]), _resident(gate[1].shape)] if gate else [])
                + [_resident(r.shape) for r in resident])
    return pl.pallas_call(
        functools.partial(_ffn_kernel, gated=gate is not None, final_norm=final_norm),
        out_shape=jax.ShapeDtypeStruct((T, D), F32),
        grid=(T // tm,),
        in_specs=in_specs,
        out_specs=row(D),
        scratch_shapes=[pltpu.VMEM((tm + SUBLANES, 2 * ff), F32), pltpu.VMEM((tm, ff), BF16)],
        compiler_params=_params(("arbitrary",)),
        name="conv_ffn",
    )(*operands)


def _qkv_kernel(x_ref, gkv_ref, gq_ref, wk_ref, wv_ref, wq_ref, qT_ref, k_ref, vT_ref, *, scale):
    heads = k_ref.shape[0]
    hd = k_ref.shape[2]
    x = x_ref[...]
    xn = x * _rms_inv(x)
    hkv = (xn * gkv_ref[...]).astype(BF16)
    hq = (xn * gq_ref[...]).astype(BF16)
    k = _dot(hkv, wk_ref[...]).astype(BF16)
    for h in range(heads):
        k_ref[h] = k[:, h * hd:(h + 1) * hd]
    vT_ref[...] = _dot(hkv, wv_ref[...]).T.astype(BF16)
    qT_ref[...] = (_dot(hq, wq_ref[...]) * scale).T.astype(BF16)


def _qkv(x, gkv, gq, wk, wv, wq, heads, scale, tm):
    T, D = x.shape
    hd = wk.shape[1] // heads
    colblk = lambda n: pl.BlockSpec((n, tm), lambda i: (0, i))
    return pl.pallas_call(
        functools.partial(_qkv_kernel, scale=scale),
        out_shape=(jax.ShapeDtypeStruct((wq.shape[1], T), BF16), jax.ShapeDtypeStruct((heads, T, hd), BF16),
                   jax.ShapeDtypeStruct((wv.shape[1], T), BF16)),
        grid=(T // tm,),
        in_specs=[pl.BlockSpec((tm, D), lambda i: (i, 0)), _resident(gkv.shape), _resident(gq.shape),
                  _resident(wk.shape), _resident(wv.shape), _resident(wq.shape)],
        out_specs=(colblk(wq.shape[1]), pl.BlockSpec((heads, tm, hd), lambda i: (0, i, 0)), colblk(wv.shape[1])),
        compiler_params=_params(("parallel",)),
        name="qkv_proj",
    )(x, gkv, gq, wk, wv, wq)


def _attn_kernel(tbl_ref, lam_ref, g_ref, qT_ref, k_ref, vT_ref, *refs, n_casts, lambda_init):
    cast_in, (o_ref, *refs) = refs[:n_casts], refs[n_casts:]
    cast_out, (qp_ref, dbias_ref, pbias_ref, m_ref, l_ref, acc_ref) = refs[:n_casts], refs[n_casts:]
    for src, dst in zip(cast_in, cast_out):
        dst[...] = src[...].astype(dst.dtype)

    hw, B = qT_ref.shape
    hd = hw // 2
    h = pl.program_id(0)
    qi = pl.program_id(1)
    nsub = B // LANES

    @pl.when(qi == 0)
    def _build_bias_tiles():
        far = tbl_ref[h * REL_BUCKETS + REL_BUCKETS - 1]
        a = lax.broadcasted_iota(jnp.int32, (LANES, LANES), 0)
        b = lax.broadcasted_iota(jnp.int32, (LANES, LANES), 1)

        def rel_bias(n):
            val = jnp.zeros(n.shape, F32)
            for bucket in range(REL_BUCKETS - 2, -1, -1):
                val = jnp.where(n < T5_BUCKET_LO[bucket + 1], (tbl_ref[h * REL_BUCKETS + bucket] - far) * LOG2E, val)
            return val

        e0 = jnp.where(b >= a, rel_bias(b - a), NEG)
        e1 = rel_bias(LANES + b - a)
        zero = jnp.zeros((LANES, LANES), F32)
        masked = jnp.full((LANES, LANES), NEG, F32)
        for r in range(nsub):
            for c in range(nsub):
                rs, cs = slice(r * LANES, (r + 1) * LANES), slice(c * LANES, (c + 1) * LANES)
                dbias_ref[rs, cs] = masked if c < r else e0 if c == r else e1 if c == r + 1 else zero
        for c in range(nsub):
            pbias_ref[:, c * LANES:(c + 1) * LANES] = e1 if c == 0 else zero

    qT = qT_ref[...].astype(F32)
    half = lax.broadcasted_iota(jnp.int32, qT.shape, 0) < hd
    qp_ref[0] = jnp.where(half, qT, 0.0).astype(BF16)
    qp_ref[1] = jnp.where(half, 0.0, qT).astype(BF16)
    l_ref[...] = jnp.zeros(l_ref.shape, F32)
    acc_ref[...] = jnp.zeros(acc_ref.shape, F32)

    def scores(j, kind, c):
        st = pl.multiple_of(j * B, B)
        if kind == "prev":
            cut = B - LANES
            last = pl.multiple_of(st + cut, LANES)
            return jnp.concatenate([_dot(k_ref[pl.ds(st, cut), :], qp_ref[c]),
                                    _dot(k_ref[pl.ds(last, LANES), :], qp_ref[c]) + pbias_ref[...]], axis=0)
        s = _dot(k_ref[pl.ds(st, B), :], qp_ref[c])
        return s + dbias_ref[...] if kind == "diag" else s

    def update(j, sc):
        st = pl.multiple_of(j * B, B)
        vt = vT_ref[:, pl.ds(st, B)]
        for c in range(2):
            m_old = m_ref[c]
            m_new = jnp.maximum(m_old, jnp.max(sc[c], axis=0, keepdims=True))
            p = jnp.exp2(sc[c] - m_new)
            alpha = jnp.exp2(m_old - m_new)
            l_ref[c] = alpha * l_ref[c] + jnp.sum(p, axis=0, keepdims=True)
            acc_ref[c] = alpha * acc_ref[c] + _dot(vt, p.astype(BF16))
            m_ref[c] = m_new

    def process(tiles):
        qk = lambda t: [scores(tiles[t][0], tiles[t][1], c) for c in range(2)]
        nxt = qk(0)
        for t in range(len(tiles)):
            cur, nxt = nxt, (qk(t + 1) if t + 1 < len(tiles) else None)
            update(tiles[t][0], cur)

    def process_fast(tiles):
        qk = lambda t: [scores(tiles[t][0], tiles[t][1], c) for c in range(2)]
        psum = [None, None]
        ps = [[], []]
        vts = []
        nxt = qk(0)
        for t in range(len(tiles)):
            cur, nxt = nxt, (qk(t + 1) if t + 1 < len(tiles) else None)
            vts.append(vT_ref[:, pl.ds(pl.multiple_of(tiles[t][0] * B, B), B)])
            for c in range(2):
                p = jnp.exp2(cur[c])
                part = jnp.sum(p, axis=0, keepdims=True)
                psum[c] = part if psum[c] is None else psum[c] + part
                ps[c].append(p.astype(BF16))
        vt_all = jnp.concatenate(vts, axis=1)
        for c in range(2):
            acc_ref[c] = acc_ref[c] + _dot(vt_all, jnp.concatenate(ps[c], axis=0))
            l_ref[c] = l_ref[c] + psum[c]

    n_far = jnp.maximum(qi - 1, 0)
    far = lambda first, n: [(first + t, "far") for t in range(n)]
    edge = [(qi - 1, "prev"), (qi, "diag")]

    def far_group(g, carry):
        process_fast(far(FAR_GROUP * g, FAR_GROUP))
        return carry

    n_groups = n_far // FAR_GROUP
    lax.fori_loop(0, n_groups, far_group, 0)
    done = n_groups * FAR_GROUP
    rem = n_far - done
    size = FAR_GROUP // 2
    while size >= 1:
        first = done + (rem & ~(2 * size - 1))
        has, is_last = (rem & size) != 0, (rem & (size - 1)) == 0

        @pl.when(jnp.logical_and(has, jnp.logical_not(is_last)))
        def _leftover(first=first, size=size):
            process_fast(far(first, size))

        @pl.when(jnp.logical_and(has, is_last))
        def _leftover_and_edge(first=first, size=size):
            process_fast(far(first, size) + edge)
        size //= 2

    @pl.when(jnp.logical_and(rem == 0, qi >= 1))
    def _edge_only():
        process_fast(edge)

    @pl.when(qi == 0)
    def _diagonal_only():
        process_fast([(qi, "diag")])

    def causal_edge_tiles():
        @pl.when(qi >= 1)
        def _previous_and_diagonal():
            process(edge)

        @pl.when(qi == 0)
        def _diagonal_only_slow():
            process([(qi, "diag")])

    def write_output():
        lam = (jnp.exp(jnp.sum(lam_ref[0:1, :] * lam_ref[1:2, :], axis=-1, keepdims=True))
               - jnp.exp(jnp.sum(lam_ref[2:3, :] * lam_ref[3:4, :], axis=-1, keepdims=True)) + lambda_init)
        oT = acc_ref[0] * (1.0 / l_ref[0]) - acc_ref[1] * (lam / l_ref[1])
        inv = lax.rsqrt(jnp.mean(oT * oT, axis=0, keepdims=True) + EPS)
        y = oT * inv * g_ref[...] * (1.0 - lambda_init)
        o_ref[...] = y.T.astype(o_ref.dtype)

    write_output()
    not_finite = lambda x: jnp.where(jnp.isfinite(x), 0.0, 1.0)
    l_low = jnp.minimum(l_ref[0], l_ref[1])
    flags = (jnp.sum(not_finite(acc_ref[0]) + not_finite(acc_ref[1]), axis=0, keepdims=True)
             + not_finite(l_ref[0]) + not_finite(l_ref[1]) + jnp.where(l_low >= FIXED_REF_MIN_SUM, 0.0, 1.0))

    @pl.when(jnp.sum(flags) > 0.0)
    def _redo_with_running_max():
        m_ref[...] = jnp.full(m_ref.shape, NEG, F32)
        l_ref[...] = jnp.zeros(l_ref.shape, F32)
        acc_ref[...] = jnp.zeros(acc_ref.shape, F32)
        causal_edge_tiles()

        def one_far_tile(j, carry):
            process([(j, "far")])
            return carry

        lax.fori_loop(0, n_far, one_far_tile, 0)
        write_output()


def _attention(tbl, lam, g, qT, k, vT, lambda_init, block, weights):
    heads, T, hw = k.shape
    dv = vT.shape[0] // heads
    B = block
    nq = T // B
    rows = CAST_ROWS * ((heads * nq + 1) // 2)
    slabs = [w.reshape(w.shape[0], rows, w[0].size // rows) for w, _ in weights]
    assert all(s.shape[2] % LANES == 0 for s in slabs)

    def slab_spec(s, layer=None):
        blk = lambda h, i: (h * nq + i) // 2
        if layer is None:
            return pl.BlockSpec((CAST_ROWS, s.shape[2]), lambda h, i: (blk(h, i), 0))
        return pl.BlockSpec((None, CAST_ROWS, s.shape[2]), lambda h, i: (layer, blk(h, i), 0))

    smem = pl.BlockSpec(memory_space=pltpu.SMEM)
    out = pl.pallas_call(
        functools.partial(_attn_kernel, n_casts=len(slabs), lambda_init=lambda_init),
        out_shape=(jax.ShapeDtypeStruct((T, heads * dv), BF16),
                   *[jax.ShapeDtypeStruct(s.shape[1:], BF16) for s in slabs]),
        grid=(heads, nq),
        in_specs=[smem,
                  pl.BlockSpec(lam.shape, lambda h, i: (0, 0)),
                  pl.BlockSpec(g.shape, lambda h, i: (0, 0)),
                  pl.BlockSpec((hw, B), lambda h, i: (h, i)),
                  pl.BlockSpec((None, T, hw), lambda h, i: (h, 0, 0)),
                  pl.BlockSpec((dv, T), lambda h, i: (h, 0)),
                  *[slab_spec(s, layer) for s, (_, layer) in zip(slabs, weights)]],
        out_specs=(pl.BlockSpec((B, dv), lambda h, i: (i, h)), *[slab_spec(s) for s in slabs]),
        scratch_shapes=[pltpu.VMEM((2, hw, B), BF16), pltpu.VMEM((B, B), F32), pltpu.VMEM((LANES, B), F32),
                        pltpu.VMEM((2, 1, B), F32), pltpu.VMEM((2, 1, B), F32), pltpu.VMEM((2, dv, B), F32)],
        compiler_params=_params(("arbitrary", "arbitrary")),
        name="diff_attention",
    )(tbl, lam, g, qT, k, vT, *slabs)
    return out[0], [o.reshape(w.shape[1:]) for o, (w, _) in zip(out[1:], weights)]


def kernel(x, a_w_in, a_w_out, a_gnorm, a_lb_logits, b_w_q, b_w_o, b_lam_q1, b_lam_k1, b_lam_q2, b_lam_k2,
           b_subln, kv_norm, kv_w, rel_table, norm_mix, norm_ffn, ffn_w_up, ffn_conv_w, ffn_conv_b,
           ffn_w_down, final_norm):
    batch, T, D = x.shape
    depth = norm_mix.shape[0]
    n_a = a_w_in.shape[0]
    qdim = b_w_q.shape[2]
    row = lambda v: v.reshape(1, -1).astype(F32)
    tm = min(512, T)
    chunk = min(HG_CHUNK, T)
    block = min(ATT_BLOCK, T)
    tbl = rel_table.astype(F32).T.reshape(-1)

    outs = []
    for bi in range(batch):
        xs = x[bi]
        qT = k3 = vT = None
        for li in range(depth):
            if li == n_a:
                j = 0
                qT, k3, vT = _qkv(xs, row(kv_norm), row(norm_mix[li]), kv_w[:, :qdim].astype(BF16),
                                  kv_w[:, qdim:].astype(BF16), b_w_q[j].astype(BF16), DA_HEADS,
                                  DA_HEAD_DIM ** -0.5 * LOG2E, tm)
            if li < n_a:
                mix, gs = _hg_mixer(xs, row(norm_mix[li]), a_w_in[li].astype(BF16), a_lb_logits.astype(F32), li,
                                    HG_HEADS, tm, chunk)
                gate, w_mix = (gs, row(a_gnorm[li])), a_w_out[li].astype(BF16)
                w_up, w_dn = ffn_w_up[li].astype(BF16), ffn_w_down[li].astype(BF16)
            else:
                j = li - n_a
                if j > 0:
                    qT, _, _ = _qkv(xs, row(kv_norm), row(norm_mix[li]), kv_w[:, :qdim].astype(BF16),
                                    kv_w[:, qdim:].astype(BF16), b_w_q[j].astype(BF16), DA_HEADS,
                                    DA_HEAD_DIM ** -0.5 * LOG2E, tm)
                lambda_init = 0.8 - 0.6 * math.exp(-0.3 * li)
                lam = jnp.stack([b_lam_q1[j], b_lam_k1[j], b_lam_q2[j], b_lam_k2[j]]).astype(F32)
                mix, (w_mix, w_up, w_dn) = _attention(tbl, lam, b_subln[j].reshape(-1, 1).astype(F32), qT, k3, vT,
                                                      lambda_init, block, [(b_w_o, j), (ffn_w_up, li), (ffn_w_down, li)])
                gate = None
            last = li == depth - 1
            xs = _ffn(xs, mix, gate, w_mix, row(norm_ffn[li]), w_up, ffn_conv_w[li].astype(F32),
                      row(ffn_conv_b[li]), w_dn, row(final_norm), last, tm)
        outs.append(xs)
    return jnp.stack(outs).astype(x.dtype)
```

```python
import functools
import math

import jax
import jax.numpy as jnp
from jax import lax
from jax.experimental import pallas as pl
from jax.experimental.pallas import tpu as pltpu

F32 = jnp.float32
BF16 = jnp.bfloat16
EPS = 1e-6

LANES = 128
SUBLANES = 8
MXU_COLS = 256
VMEM_LIMIT_BYTES = 56 * 2**20

HG_HEADS = 8
DA_HEADS = 8
DA_HEAD_DIM = 64
REL_BUCKETS = 32
REL_MAX_DIST = 128
CONV_W = 3
NEG = -1e30
LOG2E = math.log2(math.e)

HG_CHUNK = 256
HG_PROJ_CHUNKS = 2
ATT_BLOCK = 512
FAR_GROUP = 8
CAST_ROWS = 2 * SUBLANES
FIXED_REF_MIN_SUM = 2.0 ** -100


def _t5_bucket_lower_bounds():
    max_exact = REL_BUCKETS // 2
    lo = list(range(max_exact))
    bucket_of = lambda n: min(
        max_exact + int(math.log(n / max_exact) / math.log(REL_MAX_DIST / max_exact) * (REL_BUCKETS - max_exact)),
        REL_BUCKETS - 1)
    n = max_exact
    for b in range(max_exact, REL_BUCKETS):
        while bucket_of(n) < b:
            n += 1
        lo.append(n)
    return tuple(lo)


T5_BUCKET_LO = _t5_bucket_lower_bounds()


def _dot(a, b):
    return jnp.dot(a, b, preferred_element_type=F32)


def _dot_nt(a, b):
    return lax.dot_general(a, b, (((1,), (1,)), ((), ())), preferred_element_type=F32)


def _dot_tn(a, b):
    return lax.dot_general(a, b, (((0,), (0,)), ((), ())), preferred_element_type=F32)


def _neg_abs(x):
    bits = pltpu.bitcast(x, jnp.uint32) | jnp.uint32(0x80000000)
    return pltpu.bitcast(bits, F32)


def _rms_inv(x):
    return lax.rsqrt(jnp.mean(x * x, axis=-1, keepdims=True) + EPS)


def _resident(shape):
    nd = len(shape)
    return pl.BlockSpec(shape, lambda *_: (0,) * nd, pipeline_mode=pl.Buffered(1))


def _params(sem):
    return pltpu.CompilerParams(dimension_semantics=sem, vmem_limit_bytes=VMEM_LIMIT_BYTES)


def _level_ref(bh, w):
    C, d = bh.shape
    n = 2 * w
    if w >= SUBLANES:
        return jnp.concatenate(
            [jnp.broadcast_to(bh[i * n + w - 1:i * n + w, :], (n, d)) for i in range(C // n)], axis=0)
    b3 = bh.reshape(C // SUBLANES, SUBLANES, d)
    pick = lambda s: jnp.broadcast_to(b3[:, s:s + 1, :], b3.shape)
    sub = lax.broadcasted_iota(jnp.int32, b3.shape, 1)
    r = pick(SUBLANES - n + w - 1)
    for s in range(SUBLANES - 2 * n, -1, -n):
        r = jnp.where(sub < s + n, pick(s + w - 1), r)
    return r.reshape(C, d)


def _hg_mixer_kernel(x_ref, g_ref, w_ref, lbl_ref, o_ref, gs_ref, st_ref, q_s, k_s, v_s, b_s, *, layer, heads, chunk):
    tm = x_ref.shape[0]
    C = chunk
    fd = o_ref.shape[1]
    dk = fd // heads
    dv = fd // heads
    n_chunks = tm // C

    @pl.when(pl.program_id(0) == 0)
    def _zero_state():
        st_ref[...] = jnp.zeros_like(st_ref)

    x = x_ref[...]
    h = (x * _rms_inv(x) * g_ref[...]).astype(BF16)
    lg = lbl_ref[...]
    ex = jnp.exp(lg - jnp.max(lg, axis=0, keepdims=True))
    sm = ex / jnp.sum(ex, axis=0, keepdims=True)
    lb = jnp.sum(sm[:layer + 1], axis=0, keepdims=True)
    log_lb = jnp.log(lb) * LOG2E
    log_1m_lb = jnp.log1p(-lb) * LOG2E

    def softplus_neg_abs(d):
        return jnp.log2(1.0 + jnp.exp2(_neg_abs(d)))

    def silu(z):
        return z / (1.0 + jnp.exp(-z))

    row = lax.broadcasted_iota(jnp.int32, (C, C), 0)
    col = lax.broadcasted_iota(jnp.int32, (C, C), 1)
    tri = jnp.where(row >= col, 1.0, 0.0).astype(BF16)
    tri3 = jnp.concatenate([tri, tri, tri], axis=1)
    levels = [1 << i for i in range(int(math.log2(C)))]
    xor = row ^ col
    below = row > col
    level_mask = [below & ((xor >> int(math.log2(w))) == 1) for w in levels]
    diag = row == col

    G = HG_PROJ_CHUNKS

    def project(i, c0):
        rows = slice(i * C, (i + G) * C)
        hc = h[rows]
        cs = slice(c0, c0 + MXU_COLS)
        wcol = lambda part: w_ref[:, part * fd + c0:part * fd + c0 + MXU_COLS]
        chunk_rows = [(i + n, slice(n * C, (n + 1) * C)) for n in range(G)]

        def stage(buf, val):
            for n, r in chunk_rows:
                buf[n, :, cs] = val[r]

        stage(q_s, silu(_dot(hc, wcol(0))).astype(BF16))
        zf = _dot(hc, wcol(1)) * LOG2E
        stage(v_s, _dot(hc, wcol(2)).astype(BF16))
        c = log_1m_lb[:, cs] + jnp.minimum(zf, 0.0) - softplus_neg_abs(zf)
        a = log_lb[:, cs]
        lf = jnp.maximum(a, c) + softplus_neg_abs(a - c)
        stage(k_s, jnp.exp2(c - zf).astype(BF16))
        gs_ref[rows, cs] = silu(_dot(hc, wcol(3))).astype(gs_ref.dtype)
        p0 = lf.astype(BF16)
        r1 = lf - p0.astype(F32)
        p1 = r1.astype(BF16)
        p2 = (r1 - p1.astype(F32)).astype(BF16)
        for n, r in chunk_rows:
            b_s[n, :, cs] = _dot(tri3, jnp.concatenate([p0[r], p1[r], p2[r]], axis=0))

    decay = lambda expo: jnp.exp2(expo).astype(BF16)

    def recur(i, hh):
        rows = slice(i * C, (i + 1) * C)
        sk = slice(hh * dk, (hh + 1) * dk)
        sv = slice(hh * dv, (hh + 1) * dv)
        bh = b_s[i, :, sk]
        qh = q_s[i, :, sk]
        kh = k_s[i, :, sk]
        vh = v_s[i, :, sv]
        b_last = bh[C - 1:C, :]
        a = jnp.where(diag, _dot_nt(qh, kh), 0.0)
        for w, mk in zip(levels, level_mask):
            e = decay(_neg_abs(bh - _level_ref(bh, w)))
            a = jnp.where(mk, _dot_nt(qh * e, kh * e), a)
        o = _dot(a.astype(BF16), vh)
        st = st_ref[hh]
        o = o + _dot_nt(qh * decay(bh), st.astype(BF16))
        st_ref[hh] = st * jnp.exp2(b_last) + _dot_tn(vh, kh * decay(b_last - bh))
        o_ref[rows, sv] = o.astype(o_ref.dtype)

    col_starts = list(range(0, fd, MXU_COLS))
    per_cols = G * heads // len(col_starts)
    for c0 in col_starts:
        project(0, c0)
    for i in range(0, n_chunks, G):
        units = [(i + n, hh) for n in range(G) for hh in range(heads)]
        for n, c0 in enumerate(col_starts):
            if i + G < n_chunks:
                project(i + G, c0)
            for ci, hh in units[n * per_cols:(n + 1) * per_cols]:
                recur(ci, hh)


def _hg_mixer(x, g, w, lb_logits, layer, heads, tm, chunk):
    T, D = x.shape
    fd = lb_logits.shape[1]
    assert w.shape[1] == 4 * fd
    row = lambda n: pl.BlockSpec((tm, n), lambda i: (i, 0))
    stage = lambda dt: pltpu.VMEM((tm // chunk, chunk, fd), dt)
    return pl.pallas_call(
        functools.partial(_hg_mixer_kernel, layer=layer, heads=heads, chunk=chunk),
        out_shape=(jax.ShapeDtypeStruct((T, fd), BF16), jax.ShapeDtypeStruct((T, fd), BF16)),
        grid=(T // tm,),
        in_specs=[row(D), _resident((1, D)), _resident(w.shape), _resident(lb_logits.shape)],
        out_specs=(row(fd), row(fd)),
        scratch_shapes=[pltpu.VMEM((heads, fd // heads, fd // heads), F32),
                        stage(BF16), stage(BF16), stage(BF16), stage(F32)],
        compiler_params=_params(("arbitrary",)),
        name="hgrn2_mixer",
    )(x, g, w, lb_logits)


def _ffn_kernel(x_ref, a_ref, *refs, gated, final_norm):
    if gated:
        gs_ref, gn_ref, *refs = refs
    wpre_ref, g_ref, wup_ref, cw_ref, cb_ref, wdn_ref, fg_ref, y_ref, u_ref, act_ref = refs
    tm = x_ref.shape[0]
    ff = wdn_ref.shape[0]
    halo = SUBLANES

    @pl.when(pl.program_id(0) == 0)
    def _zero_halo():
        u_ref[0:halo, :] = jnp.zeros((halo, u_ref.shape[1]), F32)

    halves = [slice(r0, r0 + tm // 2) for r0 in range(0, tm, tm // 2)]
    xs, hs = [], []
    for rows in halves:
        a = a_ref[rows, :]
        if gated:
            o = a.astype(F32)
            a = (o * _rms_inv(o) * gn_ref[...] * gs_ref[rows, :].astype(F32)).astype(BF16)
        x = x_ref[rows, :] + _dot(a, wpre_ref[...])
        xs.append(x)
        hs.append((x * _rms_inv(x) * g_ref[...]).astype(BF16))
    h = jnp.concatenate(hs, axis=0)
    up_cols = 512
    for c0 in range(0, 2 * ff, up_cols):
        u_ref[halo:halo + tm, c0:c0 + up_cols] = _dot(h, wup_ref[:, c0:c0 + up_cols])

    def conv(cs):
        acc = cb_ref[:, cs] + cw_ref[CONV_W - 1:CONV_W, cs] * u_ref[halo:halo + tm, cs]
        for j in range(CONV_W - 1):
            off = halo - (CONV_W - 1) + j
            acc = acc + cw_ref[j:j + 1, cs] * u_ref[off:off + tm, cs]
        return acc

    act_cols = 256
    for c0 in range(0, ff, act_cols):
        gate = conv(slice(c0, c0 + act_cols))
        val = conv(slice(ff + c0, ff + c0 + act_cols))
        act_ref[:, c0:c0 + act_cols] = (gate * jax.nn.sigmoid(gate) * val).astype(BF16)

    u_ref[0:halo, :] = u_ref[tm:tm + halo, :]
    for rows, x in zip(halves, xs):
        y = x + _dot(act_ref[rows, :], wdn_ref[...])
        if final_norm:
            y = y * _rms_inv(y) * fg_ref[...]
        y_ref[rows, :] = y


def _ffn(x, a, gate, wpre, g, wup, cw, cb, wdn, fg, final_norm, tm):
    T, D = x.shape
    ff = wdn.shape[0]
    row = lambda n: pl.BlockSpec((tm, n), lambda i: (i, 0))
    resident = [wpre, g, wup, cw, cb, wdn, fg]
    operands = [x, a] + ([gate[0], gate[1]] if gate else []) + resident
    in_specs = ([row(D), row(a.shape[1])] + ([row(gate[0].shape[1]), _resident(gate[1].shape)] if gate else [])
                + [_resident(r.shape) for r in resident])
    return pl.pallas_call(
        functools.partial(_ffn_kernel, gated=gate is not None, final_norm=final_norm),
        out_shape=jax.ShapeDtypeStruct((T, D), F32),
        grid=(T // tm,),
        in_specs=in_specs,
        out_specs=row(D),
        scratch_shapes=[pltpu.VMEM((tm + SUBLANES, 2 * ff), F32), pltpu.VMEM((tm, ff), BF16)],
        compiler_params=_params(("arbitrary",)),
        name="conv_ffn",
    )(*operands)


def _qkv_kernel(x_ref, gkv_ref, gq_ref, wk_ref, wv_ref, wq_ref, qT_ref, k_ref, vT_ref, *, scale):
    heads = k_ref.shape[0]
    hd = k_ref.shape[2]
    x = x_ref[...]
    xn = x * _rms_inv(x)
    hkv = (xn * gkv_ref[...]).astype(BF16)
    hq = (xn * gq_ref[...]).astype(BF16)
    k = _dot(hkv, wk_ref[...]).astype(BF16)
    for h in range(heads):
        k_ref[h] = k[:, h * hd:(h + 1) * hd]
    vT_ref[...] = _dot(hkv, wv_ref[...]).T.astype(BF16)
    qT_ref[...] = (_dot(hq, wq_ref[...]) * scale).T.astype(BF16)


def _qkv(x, gkv, gq, wk, wv, wq, heads, scale, tm):
    T, D = x.shape
    hd = wk.shape[1] // heads
    colblk = lambda n: pl.BlockSpec((n, tm), lambda i: (0, i))
    return pl.pallas_call(
        functools.partial(_qkv_kernel, scale=scale),
        out_shape=(jax.ShapeDtypeStruct((wq.shape[1], T), BF16), jax.ShapeDtypeStruct((heads, T, hd), BF16),
                   jax.ShapeDtypeStruct((wv.shape[1], T), BF16)),
        grid=(T // tm,),
        in_specs=[pl.BlockSpec((tm, D), lambda i: (i, 0)), _resident(gkv.shape), _resident(gq.shape),
                  _resident(wk.shape), _resident(wv.shape), _resident(wq.shape)],
        out_specs=(colblk(wq.shape[1]), pl.BlockSpec((heads, tm, hd), lambda i: (0, i, 0)), colblk(wv.shape[1])),
        compiler_params=_params(("parallel",)),
        name="qkv_proj",
    )(x, gkv, gq, wk, wv, wq)


def _attn_kernel(tbl_ref, lam_ref, g_ref, qT_ref, k_ref, vT_ref, *refs, n_casts, lambda_init):
    cast_in, (o_ref, *refs) = refs[:n_casts], refs[n_casts:]
    cast_out, (qp_ref, dbias_ref, pbias_ref, m_ref, l_ref, acc_ref) = refs[:n_casts], refs[n_casts:]
    for src, dst in zip(cast_in, cast_out):
        dst[...] = src[...].astype(dst.dtype)

    hw, B = qT_ref.shape
    hd = hw // 2
    h = pl.program_id(0)
    qi = pl.program_id(1)
    nsub = B // LANES

    @pl.when(qi == 0)
    def _build_bias_tiles():
        far = tbl_ref[h * REL_BUCKETS + REL_BUCKETS - 1]
        a = lax.broadcasted_iota(jnp.int32, (LANES, LANES), 0)
        b = lax.broadcasted_iota(jnp.int32, (LANES, LANES), 1)

        def rel_bias(n):
            val = jnp.zeros(n.shape, F32)
            for bucket in range(REL_BUCKETS - 2, -1, -1):
                val = jnp.where(n < T5_BUCKET_LO[bucket + 1], (tbl_ref[h * REL_BUCKETS + bucket] - far) * LOG2E, val)
            return val

        e0 = jnp.where(b >= a, rel_bias(b - a), NEG)
        e1 = rel_bias(LANES + b - a)
        zero = jnp.zeros((LANES, LANES), F32)
        masked = jnp.full((LANES, LANES), NEG, F32)
        for r in range(nsub):
            for c in range(nsub):
                rs, cs = slice(r * LANES, (r + 1) * LANES), slice(c * LANES, (c + 1) * LANES)
                dbias_ref[rs, cs] = masked if c < r else e0 if c == r else e1 if c == r + 1 else zero
        for c in range(nsub):
            pbias_ref[:, c * LANES:(c + 1) * LANES] = e1 if c == 0 else zero

    qT = qT_ref[...].astype(F32)
    half = lax.broadcasted_iota(jnp.int32, qT.shape, 0) < hd
    qp_ref[0] = jnp.where(half, qT, 0.0).astype(BF16)
    qp_ref[1] = jnp.where(half, 0.0, qT).astype(BF16)
    l_ref[...] = jnp.zeros(l_ref.shape, F32)
    acc_ref[...] = jnp.zeros(acc_ref.shape, F32)

    def scores(j, kind, c):
        st = pl.multiple_of(j * B, B)
        if kind == "prev":
            cut = B - LANES
            last = pl.multiple_of(st + cut, LANES)
            return jnp.concatenate([_dot(k_ref[pl.ds(st, cut), :], qp_ref[c]),
                                    _dot(k_ref[pl.ds(last, LANES), :], qp_ref[c]) + pbias_ref[...]], axis=0)
        s = _dot(k_ref[pl.ds(st, B), :], qp_ref[c])
        return s + dbias_ref[...] if kind == "diag" else s

    def update(j, sc):
        st = pl.multiple_of(j * B, B)
        vt = vT_ref[:, pl.ds(st, B)]
        for c in range(2):
            m_old = m_ref[c]
            m_new = jnp.maximum(m_old, jnp.max(sc[c], axis=0, keepdims=True))
            p = jnp.exp2(sc[c] - m_new)
            alpha = jnp.exp2(m_old - m_new)
            l_ref[c] = alpha * l_ref[c] + jnp.sum(p, axis=0, keepdims=True)
            acc_ref[c] = alpha * acc_ref[c] + _dot(vt, p.astype(BF16))
            m_ref[c] = m_new

    def process(tiles):
        qk = lambda t: [scores(tiles[t][0], tiles[t][1], c) for c in range(2)]
        nxt = qk(0)
        for t in range(len(tiles)):
            cur, nxt = nxt, (qk(t + 1) if t + 1 < len(tiles) else None)
            update(tiles[t][0], cur)

    def process_fast(tiles):
        qk = lambda t: [scores(tiles[t][0], tiles[t][1], c) for c in range(2)]
        psum = [None, None]
        ps = [[], []]
        vts = []
        nxt = qk(0)
        for t in range(len(tiles)):
            cur, nxt = nxt, (qk(t + 1) if t + 1 < len(tiles) else None)
            vts.append(vT_ref[:, pl.ds(pl.multiple_of(tiles[t][0] * B, B), B)])
            for c in range(2):
                p = jnp.exp2(cur[c])
                part = jnp.sum(p, axis=0, keepdims=True)
                psum[c] = part if psum[c] is None else psum[c] + part
                ps[c].append(p.astype(BF16))
        vt_all = jnp.concatenate(vts, axis=1)
        for c in range(2):
            acc_ref[c] = acc_ref[c] + _dot(vt_all, jnp.concatenate(ps[c], axis=0))
            l_ref[c] = l_ref[c] + psum[c]

    n_far = jnp.maximum(qi - 1, 0)
    far = lambda first, n: [(first + t, "far") for t in range(n)]
    edge = [(qi - 1, "prev"), (qi, "diag")]

    def far_group(g, carry):
        process_fast(far(FAR_GROUP * g, FAR_GROUP))
        return carry

    n_groups = n_far // FAR_GROUP
    lax.fori_loop(0, n_groups, far_group, 0)
    done = n_groups * FAR_GROUP
    rem = n_far - done
    size = FAR_GROUP // 2
    while size >= 1:
        first = done + (rem & ~(2 * size - 1))
        has, is_last = (rem & size) != 0, (rem & (size - 1)) == 0

        @pl.when(jnp.logical_and(has, jnp.logical_not(is_last)))
        def _leftover(first=first, size=size):
            process_fast(far(first, size))

        @pl.when(jnp.logical_and(has, is_last))
        def _leftover_and_edge(first=first, size=size):
            process_fast(far(first, size) + edge)
        size //= 2

    @pl.when(jnp.logical_and(rem == 0, qi >= 1))
    def _edge_only():
        process_fast(edge)

    @pl.when(qi == 0)
    def _diagonal_only():
        process_fast([(qi, "diag")])

    def causal_edge_tiles():
        @pl.when(qi >= 1)
        def _previous_and_diagonal():
            process(edge)

        @pl.when(qi == 0)
        def _diagonal_only_slow():
            process([(qi, "diag")])

    def write_output():
        lam = (jnp.exp(jnp.sum(lam_ref[0:1, :] * lam_ref[1:2, :], axis=-1, keepdims=True))
               - jnp.exp(jnp.sum(lam_ref[2:3, :] * lam_ref[3:4, :], axis=-1, keepdims=True)) + lambda_init)
        oT = acc_ref[0] * (1.0 / l_ref[0]) - acc_ref[1] * (lam / l_ref[1])
        inv = lax.rsqrt(jnp.mean(oT * oT, axis=0, keepdims=True) + EPS)
        y = oT * inv * g_ref[...] * (1.0 - lambda_init)
        o_ref[...] = y.T.astype(o_ref.dtype)

    write_output()
    not_finite = lambda x: jnp.where(jnp.isfinite(x), 0.0, 1.0)
    l_low = jnp.minimum(l_ref[0], l_ref[1])
    flags = (jnp.sum(not_finite(acc_ref[0]) + not_finite(acc_ref[1]), axis=0, keepdims=True)
             + not_finite(l_ref[0]) + not_finite(l_ref[1]) + jnp.where(l_low >= FIXED_REF_MIN_SUM, 0.0, 1.0))

    @pl.when(jnp.sum(flags) > 0.0)
    def _redo_with_running_max():
        m_ref[...] = jnp.full(m_ref.shape, NEG, F32)
        l_ref[...] = jnp.zeros(l_ref.shape, F32)
        acc_ref[...] = jnp.zeros(acc_ref.shape, F32)
        causal_edge_tiles()

        def one_far_tile(j, carry):
            process([(j, "far")])
            return carry

        lax.fori_loop(0, n_far, one_far_tile, 0)
        write_output()


def _attention(tbl, lam, g, qT, k, vT, lambda_init, block, weights):
    heads, T, hw = k.shape
    dv = vT.shape[0] // heads
    B = block
    nq = T // B
    steps = heads * nq
    slabs = [w for w, _ in weights]

    def block_rows(s):
        return next(r for r in range(CAST_ROWS, s.shape[1] + 1, CAST_ROWS)
                    if s.shape[1] % r == 0 and s.shape[1] // r <= steps)

    def slab_spec(s, layer=None):
        r = block_rows(s)
        blk = lambda h, i: ((h * nq + i) * (s.shape[1] // r)) // steps
        if layer is None:
            return pl.BlockSpec((r, s.shape[2]), lambda h, i: (blk(h, i), 0))
        return pl.BlockSpec((None, r, s.shape[2]), lambda h, i: (layer, blk(h, i), 0))

    smem = pl.BlockSpec(memory_space=pltpu.SMEM)
    out = pl.pallas_call(
        functools.partial(_attn_kernel, n_casts=len(slabs), lambda_init=lambda_init),
        out_shape=(jax.ShapeDtypeStruct((T, heads * dv), BF16),
                   *[jax.ShapeDtypeStruct(s.shape[1:], BF16) for s in slabs]),
        grid=(heads, nq),
        in_specs=[smem,
                  pl.BlockSpec(lam.shape, lambda h, i: (0, 0)),
                  pl.BlockSpec(g.shape, lambda h, i: (0, 0)),
                  pl.BlockSpec((hw, B), lambda h, i: (h, i)),
                  pl.BlockSpec((None, T, hw), lambda h, i: (h, 0, 0)),
                  pl.BlockSpec((dv, T), lambda h, i: (h, 0)),
                  *[slab_spec(s, layer) for s, (_, layer) in zip(slabs, weights)]],
        out_specs=(pl.BlockSpec((B, dv), lambda h, i: (i, h)), *[slab_spec(s) for s in slabs]),
        scratch_shapes=[pltpu.VMEM((2, hw, B), BF16), pltpu.VMEM((B, B), F32), pltpu.VMEM((LANES, B), F32),
                        pltpu.VMEM((2, 1, B), F32), pltpu.VMEM((2, 1, B), F32), pltpu.VMEM((2, dv, B), F32)],
        compiler_params=_params(("arbitrary", "arbitrary")),
        name="diff_attention",
    )(tbl, lam, g, qT, k, vT, *slabs)
    return out[0], list(out[1:])


def kernel(x, a_w_in, a_w_out, a_gnorm, a_lb_logits, b_w_q, b_w_o, b_lam_q1, b_lam_k1, b_lam_q2, b_lam_k2,
           b_subln, kv_norm, kv_w, rel_table, norm_mix, norm_ffn, ffn_w_up, ffn_conv_w, ffn_conv_b,
           ffn_w_down, final_norm):
    batch, T, D = x.shape
    depth = norm_mix.shape[0]
    n_a = a_w_in.shape[0]
    qdim = b_w_q.shape[2]
    row = lambda v: v.reshape(1, -1).astype(F32)
    tm = min(512, T)
    chunk = min(HG_CHUNK, T)
    block = min(ATT_BLOCK, T)
    tbl = rel_table.astype(F32).T.reshape(-1)

    outs = []
    for bi in range(batch):
        xs = x[bi]
        qT = k3 = vT = None
        for li in range(depth):
            if li == n_a:
                j = 0
                qT, k3, vT = _qkv(xs, row(kv_norm), row(norm_mix[li]), kv_w[:, :qdim].astype(BF16),
                                  kv_w[:, qdim:].astype(BF16), b_w_q[j].astype(BF16), DA_HEADS,
                                  DA_HEAD_DIM ** -0.5 * LOG2E, tm)
            if li < n_a:
                mix, gs = _hg_mixer(xs, row(norm_mix[li]), a_w_in[li].astype(BF16), a_lb_logits.astype(F32), li,
                                    HG_HEADS, tm, chunk)
                gate, w_mix = (gs, row(a_gnorm[li])), a_w_out[li].astype(BF16)
                w_up, w_dn = ffn_w_up[li].astype(BF16), ffn_w_down[li].astype(BF16)
            else:
                j = li - n_a
                if j > 0:
                    qT, _, _ = _qkv(xs, row(kv_norm), row(norm_mix[li]), kv_w[:, :qdim].astype(BF16),
                                    kv_w[:, qdim:].astype(BF16), b_w_q[j].astype(BF16), DA_HEADS,
                                    DA_HEAD_DIM ** -0.5 * LOG2E, tm)
                lambda_init = 0.8 - 0.6 * math.exp(-0.3 * li)
                lam = jnp.stack([b_lam_q1[j], b_lam_k1[j], b_lam_q2[j], b_lam_k2[j]]).astype(F32)
                mix, (w_mix, w_up, w_dn) = _attention(tbl, lam, b_subln[j].reshape(-1, 1).astype(F32), qT, k3, vT,
                                                      lambda_init, block, [(b_w_o, j), (ffn_w_up, li), (ffn_w_down, li)])
                gate = None
            last = li == depth - 1
            xs = _ffn(xs, mix, gate, w_mix, row(norm_ffn[li]), w_up, ffn_conv_w[li].astype(F32),
                      row(ffn_conv_b[li]), w_dn, row(final_norm), last, tm)
        outs.append(xs)
    return jnp.stack(outs).astype(x.dtype)
```

```python
import functools
import math

import jax
import jax.numpy as jnp
from jax import lax
from jax.experimental import pallas as pl
from jax.experimental.pallas import tpu as pltpu

F32 = jnp.float32
BF16 = jnp.bfloat16
EPS = 1e-6

LANES = 128
SUBLANES = 8
MXU_COLS = 256
VMEM_LIMIT_BYTES = 56 * 2**20

HG_HEADS = 8
DA_HEADS = 8
DA_HEAD_DIM = 64
REL_BUCKETS = 32
REL_MAX_DIST = 128
CONV_W = 3
NEG = -1e30
LOG2E = math.log2(math.e)

HG_CHUNK = 256
HG_PROJ_CHUNKS = 2
ATT_BLOCK = 512
FAR_GROUP = 8
CAST_ROWS = 2 * SUBLANES
FIXED_REF_MIN_SUM = 2.0 ** -100


def _t5_bucket_lower_bounds():
    max_exact = REL_BUCKETS // 2
    lo = list(range(max_exact))
    bucket_of = lambda n: min(
        max_exact + int(math.log(n / max_exact) / math.log(REL_MAX_DIST / max_exact) * (REL_BUCKETS - max_exact)),
        REL_BUCKETS - 1)
    n = max_exact
    for b in range(max_exact, REL_BUCKETS):
        while bucket_of(n) < b:
            n += 1
        lo.append(n)
    return tuple(lo)


T5_BUCKET_LO = _t5_bucket_lower_bounds()


def _dot(a, b):
    return jnp.dot(a, b, preferred_element_type=F32)


def _dot_nt(a, b):
    return lax.dot_general(a, b, (((1,), (1,)), ((), ())), preferred_element_type=F32)


def _dot_tn(a, b):
    return lax.dot_general(a, b, (((0,), (0,)), ((), ())), preferred_element_type=F32)


def _neg_abs(x):
    bits = pltpu.bitcast(x, jnp.uint32) | jnp.uint32(0x80000000)
    return pltpu.bitcast(bits, F32)


def _rms_inv(x):
    return lax.rsqrt(jnp.mean(x * x, axis=-1, keepdims=True) + EPS)


def _resident(shape):
    nd = len(shape)
    return pl.BlockSpec(shape, lambda *_: (0,) * nd, pipeline_mode=pl.Buffered(1))


def _params(sem):
    return pltpu.CompilerParams(dimension_semantics=sem, vmem_limit_bytes=VMEM_LIMIT_BYTES)


def _level_ref(bh, w):
    C, d = bh.shape
    n = 2 * w
    if w >= SUBLANES:
        return jnp.concatenate(
            [jnp.broadcast_to(bh[i * n + w - 1:i * n + w, :], (n, d)) for i in range(C // n)], axis=0)
    b3 = bh.reshape(C // SUBLANES, SUBLANES, d)
    pick = lambda s: jnp.broadcast_to(b3[:, s:s + 1, :], b3.shape)
    sub = lax.broadcasted_iota(jnp.int32, b3.shape, 1)
    r = pick(SUBLANES - n + w - 1)
    for s in range(SUBLANES - 2 * n, -1, -n):
        r = jnp.where(sub < s + n, pick(s + w - 1), r)
    return r.reshape(C, d)


def _hg_mixer_kernel(x_ref, g_ref, w_ref, lbl_ref, o_ref, gs_ref, st_ref, q_s, k_s, v_s, b_s, *, layer, heads, chunk):
    tm = x_ref.shape[0]
    C = chunk
    fd = o_ref.shape[1]
    dk = fd // heads
    dv = fd // heads
    n_chunks = tm // C

    @pl.when(pl.program_id(0) == 0)
    def _zero_state():
        st_ref[...] = jnp.zeros_like(st_ref)

    x = x_ref[...]
    h = (x * _rms_inv(x) * g_ref[...]).astype(BF16)
    lg = lbl_ref[...]
    ex = jnp.exp(lg - jnp.max(lg, axis=0, keepdims=True))
    sm = ex / jnp.sum(ex, axis=0, keepdims=True)
    lb = jnp.sum(sm[:layer + 1], axis=0, keepdims=True)
    log_lb = jnp.log(lb) * LOG2E
    log_1m_lb = jnp.log1p(-lb) * LOG2E

    def softplus_neg_abs(d):
        return jnp.log2(1.0 + jnp.exp2(_neg_abs(d)))

    def silu(z):
        return z / (1.0 + jnp.exp(-z))

    row = lax.broadcasted_iota(jnp.int32, (C, C), 0)
    col = lax.broadcasted_iota(jnp.int32, (C, C), 1)
    tri = jnp.where(row >= col, 1.0, 0.0).astype(BF16)
    tri3 = jnp.concatenate([tri, tri, tri], axis=1)
    levels = [1 << i for i in range(int(math.log2(C)))]
    xor = row ^ col
    below = row > col
    level_mask = [below & ((xor >> int(math.log2(w))) == 1) for w in levels]
    diag = row == col

    G = HG_PROJ_CHUNKS

    def project(i, c0):
        rows = slice(i * C, (i + G) * C)
        hc = h[rows]
        cs = slice(c0, c0 + MXU_COLS)
        wcol = lambda part: w_ref[:, part * fd + c0:part * fd + c0 + MXU_COLS]
        chunk_rows = [(i + n, slice(n * C, (n + 1) * C)) for n in range(G)]

        def stage(buf, val):
            for n, r in chunk_rows:
                buf[n, :, cs] = val[r]

        stage(q_s, silu(_dot(hc, wcol(0))).astype(BF16))
        zf = _dot(hc, wcol(1)) * LOG2E
        stage(v_s, _dot(hc, wcol(2)).astype(BF16))
        c = log_1m_lb[:, cs] + jnp.minimum(zf, 0.0) - softplus_neg_abs(zf)
        a = log_lb[:, cs]
        lf = jnp.maximum(a, c) + softplus_neg_abs(a - c)
        stage(k_s, jnp.exp2(c - zf).astype(BF16))
        gs_ref[rows, cs] = silu(_dot(hc, wcol(3))).astype(gs_ref.dtype)
        p0 = lf.astype(BF16)
        r1 = lf - p0.astype(F32)
        p1 = r1.astype(BF16)
        p2 = (r1 - p1.astype(F32)).astype(BF16)
        for n, r in chunk_rows:
            b_s[n, :, cs] = _dot(tri3, jnp.concatenate([p0[r], p1[r], p2[r]], axis=0))

    decay = lambda expo: jnp.exp2(expo).astype(BF16)

    def recur(i, hh):
        rows = slice(i * C, (i + 1) * C)
        sk = slice(hh * dk, (hh + 1) * dk)
        sv = slice(hh * dv, (hh + 1) * dv)
        bh = b_s[i, :, sk]
        qh = q_s[i, :, sk]
        kh = k_s[i, :, sk]
        vh = v_s[i, :, sv]
        b_last = bh[C - 1:C, :]
        a = jnp.where(diag, _dot_nt(qh, kh), 0.0)
        for w, mk in zip(levels, level_mask):
            e = decay(_neg_abs(bh - _level_ref(bh, w)))
            a = jnp.where(mk, _dot_nt(qh * e, kh * e), a)
        o = _dot(a.astype(BF16), vh)
        st = st_ref[hh]
        o = o + _dot_nt(qh * decay(bh), st.astype(BF16))
        st_ref[hh] = st * jnp.exp2(b_last) + _dot_tn(vh, kh * decay(b_last - bh))
        o_ref[rows, sv] = o.astype(o_ref.dtype)

    col_starts = list(range(0, fd, MXU_COLS))
    per_cols = G * heads // len(col_starts)
    for c0 in col_starts:
        project(0, c0)
    for i in range(0, n_chunks, G):
        units = [(i + n, hh) for n in range(G) for hh in range(heads)]
        for n, c0 in enumerate(col_starts):
            if i + G < n_chunks:
                project(i + G, c0)
            for ci, hh in units[n * per_cols:(n + 1) * per_cols]:
                recur(ci, hh)


def _hg_mixer(x, g, w, lb_logits, layer, heads, tm, chunk):
    T, D = x.shape
    fd = lb_logits.shape[1]
    assert w.shape[1] == 4 * fd
    row = lambda n: pl.BlockSpec((tm, n), lambda i: (i, 0))
    stage = lambda dt: pltpu.VMEM((tm // chunk, chunk, fd), dt)
    return pl.pallas_call(
        functools.partial(_hg_mixer_kernel, layer=layer, heads=heads, chunk=chunk),
        out_shape=(jax.ShapeDtypeStruct((T, fd), BF16), jax.ShapeDtypeStruct((T, fd), BF16)),
        grid=(T // tm,),
        in_specs=[row(D), _resident((1, D)), _resident(w.shape), _resident(lb_logits.shape)],
        out_specs=(row(fd), row(fd)),
        scratch_shapes=[pltpu.VMEM((heads, fd // heads, fd // heads), F32),
                        stage(BF16), stage(BF16), stage(BF16), stage(F32)],
        compiler_params=_params(("arbitrary",)),
        name="hgrn2_mixer",
    )(x, g, w, lb_logits)


def _ffn_kernel(x_ref, a_ref, *refs, gated, final_norm):
    if gated:
        gs_ref, gn_ref, *refs = refs
    wpre_ref, g_ref, wup_ref, cw_ref, cb_ref, wdn_ref, fg_ref, y_ref, u_ref, act_ref = refs
    tm = x_ref.shape[0]
    ff = wdn_ref.shape[0]
    halo = SUBLANES

    @pl.when(pl.program_id(0) == 0)
    def _zero_halo():
        u_ref[0:halo, :] = jnp.zeros((halo, u_ref.shape[1]), F32)

    halves = [slice(r0, r0 + tm // 2) for r0 in range(0, tm, tm // 2)]
    xs, hs = [], []
    for rows in halves:
        a = a_ref[rows, :]
        if gated:
            o = a.astype(F32)
            a = (o * _rms_inv(o) * gn_ref[...] * gs_ref[rows, :].astype(F32)).astype(BF16)
        x = x_ref[rows, :] + _dot(a, wpre_ref[...])
        xs.append(x)
        hs.append((x * _rms_inv(x) * g_ref[...]).astype(BF16))
    h = jnp.concatenate(hs, axis=0)
    up_cols = 512
    for c0 in range(0, 2 * ff, up_cols):
        u_ref[halo:halo + tm, c0:c0 + up_cols] = _dot(h, wup_ref[:, c0:c0 + up_cols])

    def conv(cs):
        acc = cb_ref[:, cs] + cw_ref[CONV_W - 1:CONV_W, cs] * u_ref[halo:halo + tm, cs]
        for j in range(CONV_W - 1):
            off = halo - (CONV_W - 1) + j
            acc = acc + cw_ref[j:j + 1, cs] * u_ref[off:off + tm, cs]
        return acc

    act_cols = 256
    for c0 in range(0, ff, act_cols):
        gate = conv(slice(c0, c0 + act_cols))
        val = conv(slice(ff + c0, ff + c0 + act_cols))
        act_ref[:, c0:c0 + act_cols] = (gate * jax.nn.sigmoid(gate) * val).astype(BF16)

    u_ref[0:halo, :] = u_ref[tm:tm + halo, :]
    for rows, x in zip(halves, xs):
        y = x + _dot(act_ref[rows, :], wdn_ref[...])
        if final_norm:
            y = y * _rms_inv(y) * fg_ref[...]
        y_ref[rows, :] = y


def _ffn(x, a, gate, wpre, g, wup, cw, cb, wdn, fg, final_norm, tm):
    T, D = x.shape
    ff = wdn.shape[0]
    row = lambda n: pl.BlockSpec((tm, n), lambda i: (i, 0))
    resident = [wpre, g, wup, cw, cb, wdn, fg]
    operands = [x, a] + ([gate[0], gate[1]] if gate else []) + resident
    in_specs = ([row(D), row(a.shape[1])] + ([row(gate[0].shape[1]), _resident(gate[1].shape)] if gate else [])
                + [_resident(r.shape) for r in resident])
    return pl.pallas_call(
        functools.partial(_ffn_kernel, gated=gate is not None, final_norm=final_norm),
        out_shape=jax.ShapeDtypeStruct((T, D), F32),
        grid=(T // tm,),
        in_specs=in_specs,
        out_specs=row(D),
        scratch_shapes=[pltpu.VMEM((tm + SUBLANES, 2 * ff), F32), pltpu.VMEM((tm, ff), BF16)],
        compiler_params=_params(("arbitrary",)),
        name="conv_ffn",
    )(*operands)


def _qkv_kernel(x_ref, gkv_ref, gq_ref, wk_ref, wv_ref, wq_ref, qT_ref, k_ref, vT_ref, *, scale):
    heads = k_ref.shape[0]
    hd = k_ref.shape[2]
    x = x_ref[...]
    xn = x * _rms_inv(x)
    hkv = (xn * gkv_ref[...]).astype(BF16)
    hq = (xn * gq_ref[...]).astype(BF16)
    k = _dot(hkv, wk_ref[...]).astype(BF16)
    for h in range(heads):
        k_ref[h] = k[:, h * hd:(h + 1) * hd]
    vT_ref[...] = _dot(hkv, wv_ref[...]).T.astype(BF16)
    qT_ref[...] = (_dot(hq, wq_ref[...]) * scale).T.astype(BF16)


def _qkv(x, gkv, gq, wk, wv, wq, heads, scale, tm):
    T, D = x.shape
    hd = wk.shape[1] // heads
    colblk = lambda n: pl.BlockSpec((n, tm), lambda i: (0, i))
    return pl.pallas_call(
        functools.partial(_qkv_kernel, scale=scale),
        out_shape=(jax.ShapeDtypeStruct((wq.shape[1], T), BF16), jax.ShapeDtypeStruct((heads, T, hd), BF16),
                   jax.ShapeDtypeStruct((wv.shape[1], T), BF16)),
        grid=(T // tm,),
        in_specs=[pl.BlockSpec((tm, D), lambda i: (i, 0)), _resident(gkv.shape), _resident(gq.shape),
                  _resident(wk.shape), _resident(wv.shape), _resident(wq.shape)],
        out_specs=(colblk(wq.shape[1]), pl.BlockSpec((heads, tm, hd), lambda i: (0, i, 0)), colblk(wv.shape[1])),
        compiler_params=_params(("parallel",)),
        name="qkv_proj",
    )(x, gkv, gq, wk, wv, wq)


def _attn_kernel(tbl_ref, lam_ref, g_ref, qT_ref, k_ref, vT_ref, *refs, n_casts, lambda_init):
    cast_in, (o_ref, *refs) = refs[:n_casts], refs[n_casts:]
    cast_out, (qp_ref, dbias_ref, pbias_ref, m_ref, l_ref, acc_ref) = refs[:n_casts], refs[n_casts:]
    for src, dst in zip(cast_in, cast_out):
        dst[...] = src[...].astype(dst.dtype)

    hw, B = qT_ref.shape
    hd = hw // 2
    h = pl.program_id(0)
    qi = pl.program_id(1)
    nsub = B // LANES

    @pl.when(qi == 0)
    def _build_bias_tiles():
        far = tbl_ref[h * REL_BUCKETS + REL_BUCKETS - 1]
        a = lax.broadcasted_iota(jnp.int32, (LANES, LANES), 0)
        b = lax.broadcasted_iota(jnp.int32, (LANES, LANES), 1)

        def rel_bias(n):
            val = jnp.zeros(n.shape, F32)
            for bucket in range(REL_BUCKETS - 2, -1, -1):
                val = jnp.where(n < T5_BUCKET_LO[bucket + 1], (tbl_ref[h * REL_BUCKETS + bucket] - far) * LOG2E, val)
            return val

        e0 = jnp.where(b >= a, rel_bias(b - a), NEG)
        e1 = rel_bias(LANES + b - a)
        zero = jnp.zeros((LANES, LANES), F32)
        masked = jnp.full((LANES, LANES), NEG, F32)
        for r in range(nsub):
            for c in range(nsub):
                rs, cs = slice(r * LANES, (r + 1) * LANES), slice(c * LANES, (c + 1) * LANES)
                dbias_ref[rs, cs] = masked if c < r else e0 if c == r else e1 if c == r + 1 else zero
        for c in range(nsub):
            pbias_ref[:, c * LANES:(c + 1) * LANES] = e1 if c == 0 else zero

    qT = qT_ref[...].astype(F32)
    half = lax.broadcasted_iota(jnp.int32, qT.shape, 0) < hd
    qp_ref[0] = jnp.where(half, qT, 0.0).astype(BF16)
    qp_ref[1] = jnp.where(half, 0.0, qT).astype(BF16)
    l_ref[...] = jnp.zeros(l_ref.shape, F32)
    acc_ref[...] = jnp.zeros(acc_ref.shape, F32)

    def scores(j, kind, c):
        st = pl.multiple_of(j * B, B)
        if kind == "prev":
            cut = B - LANES
            last = pl.multiple_of(st + cut, LANES)
            return jnp.concatenate([_dot(k_ref[pl.ds(st, cut), :], qp_ref[c]),
                                    _dot(k_ref[pl.ds(last, LANES), :], qp_ref[c]) + pbias_ref[...]], axis=0)
        s = _dot(k_ref[pl.ds(st, B), :], qp_ref[c])
        return s + dbias_ref[...] if kind == "diag" else s

    def update(j, sc):
        st = pl.multiple_of(j * B, B)
        vt = vT_ref[:, pl.ds(st, B)]
        for c in range(2):
            m_old = m_ref[c]
            m_new = jnp.maximum(m_old, jnp.max(sc[c], axis=0, keepdims=True))
            p = jnp.exp2(sc[c] - m_new)
            alpha = jnp.exp2(m_old - m_new)
            l_ref[c] = alpha * l_ref[c] + jnp.sum(p, axis=0, keepdims=True)
            acc_ref[c] = alpha * acc_ref[c] + _dot(vt, p.astype(BF16))
            m_ref[c] = m_new

    def process(tiles):
        qk = lambda t: [scores(tiles[t][0], tiles[t][1], c) for c in range(2)]
        nxt = qk(0)
        for t in range(len(tiles)):
            cur, nxt = nxt, (qk(t + 1) if t + 1 < len(tiles) else None)
            update(tiles[t][0], cur)

    def process_fast(tiles):
        qk = lambda t: [scores(tiles[t][0], tiles[t][1], c) for c in range(2)]
        psum = [None, None]
        ps = [[], []]
        vts = []
        nxt = qk(0)
        for t in range(len(tiles)):
            cur, nxt = nxt, (qk(t + 1) if t + 1 < len(tiles) else None)
            vts.append(vT_ref[:, pl.ds(pl.multiple_of(tiles[t][0] * B, B), B)])
            for c in range(2):
                p = jnp.exp2(cur[c])
                part = jnp.sum(p, axis=0, keepdims=True)
                psum[c] = part if psum[c] is None else psum[c] + part
                ps[c].append(p.astype(BF16))
        vt_all = jnp.concatenate(vts, axis=1)
        for c in range(2):
            acc_ref[c] = acc_ref[c] + _dot(vt_all, jnp.concatenate(ps[c], axis=0))
            l_ref[c] = l_ref[c] + psum[c]

    n_far = jnp.maximum(qi - 1, 0)
    far = lambda first, n: [(first + t, "far") for t in range(n)]
    edge = [(qi - 1, "prev"), (qi, "diag")]

    def far_group(g, carry):
        process_fast(far(FAR_GROUP * g, FAR_GROUP))
        return carry

    n_groups = n_far // FAR_GROUP
    lax.fori_loop(0, n_groups, far_group, 0)
    done = n_groups * FAR_GROUP
    rem = n_far - done
    size = FAR_GROUP // 2
    while size >= 1:
        first = done + (rem & ~(2 * size - 1))
        has, is_last = (rem & size) != 0, (rem & (size - 1)) == 0

        @pl.when(jnp.logical_and(has, jnp.logical_not(is_last)))
        def _leftover(first=first, size=size):
            process_fast(far(first, size))

        @pl.when(jnp.logical_and(has, is_last))
        def _leftover_and_edge(first=first, size=size):
            process_fast(far(first, size) + edge)
        size //= 2

    @pl.when(jnp.logical_and(rem == 0, qi >= 1))
    def _edge_only():
        process_fast(edge)

    @pl.when(qi == 0)
    def _diagonal_only():
        process_fast([(qi, "diag")])

    def causal_edge_tiles():
        @pl.when(qi >= 1)
        def _previous_and_diagonal():
            process(edge)

        @pl.when(qi == 0)
        def _diagonal_only_slow():
            process([(qi, "diag")])

    def write_output():
        lam = (jnp.exp(jnp.sum(lam_ref[0:1, :] * lam_ref[1:2, :], axis=-1, keepdims=True))
               - jnp.exp(jnp.sum(lam_ref[2:3, :] * lam_ref[3:4, :], axis=-1, keepdims=True)) + lambda_init)
        oT = acc_ref[0] * (1.0 / l_ref[0]) - acc_ref[1] * (lam / l_ref[1])
        inv = lax.rsqrt(jnp.mean(oT * oT, axis=0, keepdims=True) + EPS)
        y = oT * inv * g_ref[...] * (1.0 - lambda_init)
        o_ref[...] = y.T.astype(o_ref.dtype)

    write_output()
    not_finite = lambda x: jnp.where(jnp.isfinite(x), 0.0, 1.0)
    l_low = jnp.minimum(l_ref[0], l_ref[1])
    flags = (jnp.sum(not_finite(acc_ref[0]) + not_finite(acc_ref[1]), axis=0, keepdims=True)
             + not_finite(l_ref[0]) + not_finite(l_ref[1]) + jnp.where(l_low >= FIXED_REF_MIN_SUM, 0.0, 1.0))

    @pl.when(jnp.sum(flags) > 0.0)
    def _redo_with_running_max():
        m_ref[...] = jnp.full(m_ref.shape, NEG, F32)
        l_ref[...] = jnp.zeros(l_ref.shape, F32)
        acc_ref[...] = jnp.zeros(acc_ref.shape, F32)
        causal_edge_tiles()

        def one_far_tile(j, carry):
            process([(j, "far")])
            return carry

        lax.fori_loop(0, n_far, one_far_tile, 0)
        write_output()


def _attention(tbl, lam, g, qT, k, vT, lambda_init, block, weights):
    heads, T, hw = k.shape
    dv = vT.shape[0] // heads
    B = block
    nq = T // B
    steps = heads * nq
    slabs = [w for w, _ in weights]

    def block_rows(s):
        return next(r for r in range(CAST_ROWS, s.shape[1] + 1, CAST_ROWS)
                    if s.shape[1] % r == 0 and s.shape[1] // r <= max(steps // 4, 1))

    def slab_spec(s, layer=None):
        r = block_rows(s)
        blk = lambda h, i: ((h * nq + i) * (s.shape[1] // r)) // steps
        if layer is None:
            return pl.BlockSpec((r, s.shape[2]), lambda h, i: (blk(h, i), 0))
        return pl.BlockSpec((None, r, s.shape[2]), lambda h, i: (layer, blk(h, i), 0))

    smem = pl.BlockSpec(memory_space=pltpu.SMEM)
    out = pl.pallas_call(
        functools.partial(_attn_kernel, n_casts=len(slabs), lambda_init=lambda_init),
        out_shape=(jax.ShapeDtypeStruct((T, heads * dv), BF16),
                   *[jax.ShapeDtypeStruct(s.shape[1:], BF16) for s in slabs]),
        grid=(heads, nq),
        in_specs=[smem,
                  pl.BlockSpec(lam.shape, lambda h, i: (0, 0)),
                  pl.BlockSpec(g.shape, lambda h, i: (0, 0)),
                  pl.BlockSpec((hw, B), lambda h, i: (h, i)),
                  pl.BlockSpec((None, T, hw), lambda h, i: (h, 0, 0)),
                  pl.BlockSpec((dv, T), lambda h, i: (h, 0)),
                  *[slab_spec(s, layer) for s, (_, layer) in zip(slabs, weights)]],
        out_specs=(pl.BlockSpec((B, dv), lambda h, i: (i, h)), *[slab_spec(s) for s in slabs]),
        scratch_shapes=[pltpu.VMEM((2, hw, B), BF16), pltpu.VMEM((B, B), F32), pltpu.VMEM((LANES, B), F32),
                        pltpu.VMEM((2, 1, B), F32), pltpu.VMEM((2, 1, B), F32), pltpu.VMEM((2, dv, B), F32)],
        compiler_params=_params(("arbitrary", "arbitrary")),
        name="diff_attention",
    )(tbl, lam, g, qT, k, vT, *slabs)
    return out[0], list(out[1:])


def kernel(x, a_w_in, a_w_out, a_gnorm, a_lb_logits, b_w_q, b_w_o, b_lam_q1, b_lam_k1, b_lam_q2, b_lam_k2,
           b_subln, kv_norm, kv_w, rel_table, norm_mix, norm_ffn, ffn_w_up, ffn_conv_w, ffn_conv_b,
           ffn_w_down, final_norm):
    batch, T, D = x.shape
    depth = norm_mix.shape[0]
    n_a = a_w_in.shape[0]
    qdim = b_w_q.shape[2]
    row = lambda v: v.reshape(1, -1).astype(F32)
    tm = min(512, T)
    chunk = min(HG_CHUNK, T)
    block = min(ATT_BLOCK, T)
    tbl = rel_table.astype(F32).T.reshape(-1)

    outs = []
    for bi in range(batch):
        xs = x[bi]
        qT = k3 = vT = None
        for li in range(depth):
            if li == n_a:
                j = 0
                qT, k3, vT = _qkv(xs, row(kv_norm), row(norm_mix[li]), kv_w[:, :qdim].astype(BF16),
                                  kv_w[:, qdim:].astype(BF16), b_w_q[j].astype(BF16), DA_HEADS,
                                  DA_HEAD_DIM ** -0.5 * LOG2E, tm)
            if li < n_a:
                mix, gs = _hg_mixer(xs, row(norm_mix[li]), a_w_in[li].astype(BF16), a_lb_logits.astype(F32), li,
                                    HG_HEADS, tm, chunk)
                gate, w_mix = (gs, row(a_gnorm[li])), a_w_out[li].astype(BF16)
                w_up, w_dn = ffn_w_up[li].astype(BF16), ffn_w_down[li].astype(BF16)
            else:
                j = li - n_a
                if j > 0:
                    qT, _, _ = _qkv(xs, row(kv_norm), row(norm_mix[li]), kv_w[:, :qdim].astype(BF16),
                                    kv_w[:, qdim:].astype(BF16), b_w_q[j].astype(BF16), DA_HEADS,
                                    DA_HEAD_DIM ** -0.5 * LOG2E, tm)
                lambda_init = 0.8 - 0.6 * math.exp(-0.3 * li)
                lam = jnp.stack([b_lam_q1[j], b_lam_k1[j], b_lam_q2[j], b_lam_k2[j]]).astype(F32)
                mix, (w_mix, w_up, w_dn) = _attention(tbl, lam, b_subln[j].reshape(-1, 1).astype(F32), qT, k3, vT,
                                                      lambda_init, block, [(b_w_o, j), (ffn_w_up, li), (ffn_w_down, li)])
                gate = None
            last = li == depth - 1
            xs = _ffn(xs, mix, gate, w_mix, row(norm_ffn[li]), w_up, ffn_conv_w[li].astype(F32),
                      row(ffn_conv_b[li]), w_dn, row(final_norm), last, tm)
        outs.append(xs)
    return jnp.stack(outs).astype(x.dtype)
```

```python
import functools
import math

import jax
import jax.numpy as jnp
from jax import lax
from jax.experimental import pallas as pl
from jax.experimental.pallas import tpu as pltpu

F32 = jnp.float32
BF16 = jnp.bfloat16
EPS = 1e-6

LANES = 128
SUBLANES = 8
MXU_COLS = 256
VMEM_LIMIT_BYTES = 56 * 2**20

HG_HEADS = 8
DA_HEADS = 8
DA_HEAD_DIM = 64
REL_BUCKETS = 32
REL_MAX_DIST = 128
CONV_W = 3
NEG = -1e30
LOG2E = math.log2(math.e)

HG_CHUNK = 256
HG_PROJ_CHUNKS = 2
ATT_BLOCK = 512
FAR_GROUP = 16
CAST_ROWS = 2 * SUBLANES
FIXED_REF_MIN_SUM = 2.0 ** -100


def _t5_bucket_lower_bounds():
    max_exact = REL_BUCKETS // 2
    lo = list(range(max_exact))
    bucket_of = lambda n: min(
        max_exact + int(math.log(n / max_exact) / math.log(REL_MAX_DIST / max_exact) * (REL_BUCKETS - max_exact)),
        REL_BUCKETS - 1)
    n = max_exact
    for b in range(max_exact, REL_BUCKETS):
        while bucket_of(n) < b:
            n += 1
        lo.append(n)
    return tuple(lo)


T5_BUCKET_LO = _t5_bucket_lower_bounds()


def _dot(a, b):
    return jnp.dot(a, b, preferred_element_type=F32)


def _dot_nt(a, b):
    return lax.dot_general(a, b, (((1,), (1,)), ((), ())), preferred_element_type=F32)


def _dot_tn(a, b):
    return lax.dot_general(a, b, (((0,), (0,)), ((), ())), preferred_element_type=F32)


def _neg_abs(x):
    bits = pltpu.bitcast(x, jnp.uint32) | jnp.uint32(0x80000000)
    return pltpu.bitcast(bits, F32)


def _rms_inv(x):
    return lax.rsqrt(jnp.mean(x * x, axis=-1, keepdims=True) + EPS)


def _resident(shape):
    nd = len(shape)
    return pl.BlockSpec(shape, lambda *_: (0,) * nd, pipeline_mode=pl.Buffered(1))


def _params(sem):
    return pltpu.CompilerParams(dimension_semantics=sem, vmem_limit_bytes=VMEM_LIMIT_BYTES)


def _level_ref(bh, w):
    C, d = bh.shape
    n = 2 * w
    if w >= SUBLANES:
        return jnp.concatenate(
            [jnp.broadcast_to(bh[i * n + w - 1:i * n + w, :], (n, d)) for i in range(C // n)], axis=0)
    b3 = bh.reshape(C // SUBLANES, SUBLANES, d)
    pick = lambda s: jnp.broadcast_to(b3[:, s:s + 1, :], b3.shape)
    sub = lax.broadcasted_iota(jnp.int32, b3.shape, 1)
    r = pick(SUBLANES - n + w - 1)
    for s in range(SUBLANES - 2 * n, -1, -n):
        r = jnp.where(sub < s + n, pick(s + w - 1), r)
    return r.reshape(C, d)


def _hg_mixer_kernel(x_ref, g_ref, w_ref, lbl_ref, o_ref, gs_ref, st_ref, q_s, k_s, v_s, b_s, *, layer, heads, chunk):
    tm = x_ref.shape[0]
    C = chunk
    fd = o_ref.shape[1]
    dk = fd // heads
    dv = fd // heads
    n_chunks = tm // C

    @pl.when(pl.program_id(0) == 0)
    def _zero_state():
        st_ref[...] = jnp.zeros_like(st_ref)

    x = x_ref[...]
    h = (x * _rms_inv(x) * g_ref[...]).astype(BF16)
    lg = lbl_ref[...]
    ex = jnp.exp(lg - jnp.max(lg, axis=0, keepdims=True))
    sm = ex / jnp.sum(ex, axis=0, keepdims=True)
    lb = jnp.sum(sm[:layer + 1], axis=0, keepdims=True)
    log_lb = jnp.log(lb) * LOG2E
    log_1m_lb = jnp.log1p(-lb) * LOG2E

    def softplus_neg_abs(d):
        return jnp.log2(1.0 + jnp.exp2(_neg_abs(d)))

    def silu(z):
        return z / (1.0 + jnp.exp(-z))

    row = lax.broadcasted_iota(jnp.int32, (C, C), 0)
    col = lax.broadcasted_iota(jnp.int32, (C, C), 1)
    tri = jnp.where(row >= col, 1.0, 0.0).astype(BF16)
    tri3 = jnp.concatenate([tri, tri, tri], axis=1)
    levels = [1 << i for i in range(int(math.log2(C)))]
    xor = row ^ col
    below = row > col
    level_mask = [below & ((xor >> int(math.log2(w))) == 1) for w in levels]
    diag = row == col

    G = HG_PROJ_CHUNKS

    def project(i, c0):
        rows = slice(i * C, (i + G) * C)
        hc = h[rows]
        cs = slice(c0, c0 + MXU_COLS)
        wcol = lambda part: w_ref[:, part * fd + c0:part * fd + c0 + MXU_COLS]
        chunk_rows = [(i + n, slice(n * C, (n + 1) * C)) for n in range(G)]

        def stage(buf, val):
            for n, r in chunk_rows:
                buf[n, :, cs] = val[r]

        stage(q_s, silu(_dot(hc, wcol(0))).astype(BF16))
        zf = _dot(hc, wcol(1)) * LOG2E
        stage(v_s, _dot(hc, wcol(2)).astype(BF16))
        c = log_1m_lb[:, cs] + jnp.minimum(zf, 0.0) - softplus_neg_abs(zf)
        a = log_lb[:, cs]
        lf = jnp.maximum(a, c) + softplus_neg_abs(a - c)
        stage(k_s, jnp.exp2(c - zf).astype(BF16))
        gs_ref[rows, cs] = silu(_dot(hc, wcol(3))).astype(gs_ref.dtype)
        p0 = lf.astype(BF16)
        r1 = lf - p0.astype(F32)
        p1 = r1.astype(BF16)
        p2 = (r1 - p1.astype(F32)).astype(BF16)
        for n, r in chunk_rows:
            b_s[n, :, cs] = _dot(tri3, jnp.concatenate([p0[r], p1[r], p2[r]], axis=0))

    decay = lambda expo: jnp.exp2(expo).astype(BF16)

    def recur(i, hh):
        rows = slice(i * C, (i + 1) * C)
        sk = slice(hh * dk, (hh + 1) * dk)
        sv = slice(hh * dv, (hh + 1) * dv)
        bh = b_s[i, :, sk]
        qh = q_s[i, :, sk]
        kh = k_s[i, :, sk]
        vh = v_s[i, :, sv]
        b_last = bh[C - 1:C, :]
        a = jnp.where(diag, _dot_nt(qh, kh), 0.0)
        for w, mk in zip(levels, level_mask):
            e = decay(_neg_abs(bh - _level_ref(bh, w)))
            a = jnp.where(mk, _dot_nt(qh * e, kh * e), a)
        o = _dot(a.astype(BF16), vh)
        st = st_ref[hh]
        o = o + _dot_nt(qh * decay(bh), st.astype(BF16))
        st_ref[hh] = st * jnp.exp2(b_last) + _dot_tn(vh, kh * decay(b_last - bh))
        o_ref[rows, sv] = o.astype(o_ref.dtype)

    col_starts = list(range(0, fd, MXU_COLS))
    per_cols = G * heads // len(col_starts)
    for c0 in col_starts:
        project(0, c0)
    for i in range(0, n_chunks, G):
        units = [(i + n, hh) for n in range(G) for hh in range(heads)]
        for n, c0 in enumerate(col_starts):
            if i + G < n_chunks:
                project(i + G, c0)
            for ci, hh in units[n * per_cols:(n + 1) * per_cols]:
                recur(ci, hh)


def _hg_mixer(x, g, w, lb_logits, layer, heads, tm, chunk):
    T, D = x.shape
    fd = lb_logits.shape[1]
    assert w.shape[1] == 4 * fd
    row = lambda n: pl.BlockSpec((tm, n), lambda i: (i, 0))
    stage = lambda dt: pltpu.VMEM((tm // chunk, chunk, fd), dt)
    return pl.pallas_call(
        functools.partial(_hg_mixer_kernel, layer=layer, heads=heads, chunk=chunk),
        out_shape=(jax.ShapeDtypeStruct((T, fd), BF16), jax.ShapeDtypeStruct((T, fd), BF16)),
        grid=(T // tm,),
        in_specs=[row(D), _resident((1, D)), _resident(w.shape), _resident(lb_logits.shape)],
        out_specs=(row(fd), row(fd)),
        scratch_shapes=[pltpu.VMEM((heads, fd // heads, fd // heads), F32),
                        stage(BF16), stage(BF16), stage(BF16), stage(F32)],
        compiler_params=_params(("arbitrary",)),
        name="hgrn2_mixer",
    )(x, g, w, lb_logits)


def _ffn_kernel(x_ref, a_ref, *refs, gated, final_norm):
    if gated:
        gs_ref, gn_ref, *refs = refs
    wpre_ref, g_ref, wup_ref, cw_ref, cb_ref, wdn_ref, fg_ref, y_ref, u_ref, act_ref = refs
    tm = x_ref.shape[0]
    ff = wdn_ref.shape[0]
    halo = SUBLANES

    @pl.when(pl.program_id(0) == 0)
    def _zero_halo():
        u_ref[0:halo, :] = jnp.zeros((halo, u_ref.shape[1]), F32)

    halves = [slice(r0, r0 + tm // 2) for r0 in range(0, tm, tm // 2)]
    xs, hs = [], []
    for rows in halves:
        a = a_ref[rows, :]
        if gated:
            o = a.astype(F32)
            a = (o * _rms_inv(o) * gn_ref[...] * gs_ref[rows, :].astype(F32)).astype(BF16)
        x = x_ref[rows, :] + _dot(a, wpre_ref[...])
        xs.append(x)
        hs.append((x * _rms_inv(x) * g_ref[...]).astype(BF16))
    h = jnp.concatenate(hs, axis=0)
    up_cols = 512
    for c0 in range(0, 2 * ff, up_cols):
        u_ref[halo:halo + tm, c0:c0 + up_cols] = _dot(h, wup_ref[:, c0:c0 + up_cols])

    def conv(cs):
        acc = cb_ref[:, cs] + cw_ref[CONV_W - 1:CONV_W, cs] * u_ref[halo:halo + tm, cs]
        for j in range(CONV_W - 1):
            off = halo - (CONV_W - 1) + j
            acc = acc + cw_ref[j:j + 1, cs] * u_ref[off:off + tm, cs]
        return acc

    act_cols = 256
    for c0 in range(0, ff, act_cols):
        gate = conv(slice(c0, c0 + act_cols))
        val = conv(slice(ff + c0, ff + c0 + act_cols))
        act_ref[:, c0:c0 + act_cols] = (gate * jax.nn.sigmoid(gate) * val).astype(BF16)

    u_ref[0:halo, :] = u_ref[tm:tm + halo, :]
    for rows, x in zip(halves, xs):
        y = x + _dot(act_ref[rows, :], wdn_ref[...])
        if final_norm:
            y = y * _rms_inv(y) * fg_ref[...]
        y_ref[rows, :] = y


def _ffn(x, a, gate, wpre, g, wup, cw, cb, wdn, fg, final_norm, tm):
    T, D = x.shape
    ff = wdn.shape[0]
    row = lambda n: pl.BlockSpec((tm, n), lambda i: (i, 0))
    resident = [wpre, g, wup, cw, cb, wdn, fg]
    operands = [x, a] + ([gate[0], gate[1]] if gate else []) + resident
    in_specs = ([row(D), row(a.shape[1])] + ([row(gate[0].shape[1]), _resident(gate[1].shape)] if gate else [])
                + [_resident(r.shape) for r in resident])
    return pl.pallas_call(
        functools.partial(_ffn_kernel, gated=gate is not None, final_norm=final_norm),
        out_shape=jax.ShapeDtypeStruct((T, D), F32),
        grid=(T // tm,),
        in_specs=in_specs,
        out_specs=row(D),
        scratch_shapes=[pltpu.VMEM((tm + SUBLANES, 2 * ff), F32), pltpu.VMEM((tm, ff), BF16)],
        compiler_params=_params(("arbitrary",)),
        name="conv_ffn",
    )(*operands)


def _qkv_kernel(x_ref, gkv_ref, gq_ref, wk_ref, wv_ref, wq_ref, qT_ref, k_ref, vT_ref, *, scale):
    heads = k_ref.shape[0]
    hd = k_ref.shape[2]
    x = x_ref[...]
    xn = x * _rms_inv(x)
    hkv = (xn * gkv_ref[...]).astype(BF16)
    hq = (xn * gq_ref[...]).astype(BF16)
    k = _dot(hkv, wk_ref[...]).astype(BF16)
    for h in range(heads):
        k_ref[h] = k[:, h * hd:(h + 1) * hd]
    vT_ref[...] = _dot(hkv, wv_ref[...]).T.astype(BF16)
    qT_ref[...] = (_dot(hq, wq_ref[...]) * scale).T.astype(BF16)


def _qkv(x, gkv, gq, wk, wv, wq, heads, scale, tm):
    T, D = x.shape
    hd = wk.shape[1] // heads
    colblk = lambda n: pl.BlockSpec((n, tm), lambda i: (0, i))
    return pl.pallas_call(
        functools.partial(_qkv_kernel, scale=scale),
        out_shape=(jax.ShapeDtypeStruct((wq.shape[1], T), BF16), jax.ShapeDtypeStruct((heads, T, hd), BF16),
                   jax.ShapeDtypeStruct((wv.shape[1], T), BF16)),
        grid=(T // tm,),
        in_specs=[pl.BlockSpec((tm, D), lambda i: (i, 0)), _resident(gkv.shape), _resident(gq.shape),
                  _resident(wk.shape), _resident(wv.shape), _resident(wq.shape)],
        out_specs=(colblk(wq.shape[1]), pl.BlockSpec((heads, tm, hd), lambda i: (0, i, 0)), colblk(wv.shape[1])),
        compiler_params=_params(("parallel",)),
        name="qkv_proj",
    )(x, gkv, gq, wk, wv, wq)


def _attn_kernel(tbl_ref, lam_ref, g_ref, qT_ref, k_ref, vT_ref, *refs, n_casts, lambda_init):
    cast_in, (o_ref, *refs) = refs[:n_casts], refs[n_casts:]
    cast_out, (qp_ref, dbias_ref, pbias_ref, m_ref, l_ref, acc_ref) = refs[:n_casts], refs[n_casts:]
    for src, dst in zip(cast_in, cast_out):
        dst[...] = src[...].astype(dst.dtype)

    hw, B = qT_ref.shape
    hd = hw // 2
    h = pl.program_id(0)
    qi = pl.program_id(1)
    nsub = B // LANES

    @pl.when(qi == 0)
    def _build_bias_tiles():
        far = tbl_ref[h * REL_BUCKETS + REL_BUCKETS - 1]
        a = lax.broadcasted_iota(jnp.int32, (LANES, LANES), 0)
        b = lax.broadcasted_iota(jnp.int32, (LANES, LANES), 1)

        def rel_bias(n):
            val = jnp.zeros(n.shape, F32)
            for bucket in range(REL_BUCKETS - 2, -1, -1):
                val = jnp.where(n < T5_BUCKET_LO[bucket + 1], (tbl_ref[h * REL_BUCKETS + bucket] - far) * LOG2E, val)
            return val

        e0 = jnp.where(b >= a, rel_bias(b - a), NEG)
        e1 = rel_bias(LANES + b - a)
        zero = jnp.zeros((LANES, LANES), F32)
        masked = jnp.full((LANES, LANES), NEG, F32)
        for r in range(nsub):
            for c in range(nsub):
                rs, cs = slice(r * LANES, (r + 1) * LANES), slice(c * LANES, (c + 1) * LANES)
                dbias_ref[rs, cs] = masked if c < r else e0 if c == r else e1 if c == r + 1 else zero
        for c in range(nsub):
            pbias_ref[:, c * LANES:(c + 1) * LANES] = e1 if c == 0 else zero

    qT = qT_ref[...].astype(F32)
    half = lax.broadcasted_iota(jnp.int32, qT.shape, 0) < hd
    qp_ref[0] = jnp.where(half, qT, 0.0).astype(BF16)
    qp_ref[1] = jnp.where(half, 0.0, qT).astype(BF16)
    l_ref[...] = jnp.zeros(l_ref.shape, F32)
    acc_ref[...] = jnp.zeros(acc_ref.shape, F32)

    def scores(j, kind, c):
        st = pl.multiple_of(j * B, B)
        if kind == "prev":
            cut = B - LANES
            last = pl.multiple_of(st + cut, LANES)
            return jnp.concatenate([_dot(k_ref[pl.ds(st, cut), :], qp_ref[c]),
                                    _dot(k_ref[pl.ds(last, LANES), :], qp_ref[c]) + pbias_ref[...]], axis=0)
        s = _dot(k_ref[pl.ds(st, B), :], qp_ref[c])
        return s + dbias_ref[...] if kind == "diag" else s

    def update(j, sc):
        st = pl.multiple_of(j * B, B)
        vt = vT_ref[:, pl.ds(st, B)]
        for c in range(2):
            m_old = m_ref[c]
            m_new = jnp.maximum(m_old, jnp.max(sc[c], axis=0, keepdims=True))
            p = jnp.exp2(sc[c] - m_new)
            alpha = jnp.exp2(m_old - m_new)
            l_ref[c] = alpha * l_ref[c] + jnp.sum(p, axis=0, keepdims=True)
            acc_ref[c] = alpha * acc_ref[c] + _dot(vt, p.astype(BF16))
            m_ref[c] = m_new

    def process(tiles):
        qk = lambda t: [scores(tiles[t][0], tiles[t][1], c) for c in range(2)]
        nxt = qk(0)
        for t in range(len(tiles)):
            cur, nxt = nxt, (qk(t + 1) if t + 1 < len(tiles) else None)
            update(tiles[t][0], cur)

    def process_fast(tiles):
        qk = lambda t: [scores(tiles[t][0], tiles[t][1], c) for c in range(2)]
        psum = [None, None]
        ps = [[], []]
        vts = []
        nxt = qk(0)
        for t in range(len(tiles)):
            cur, nxt = nxt, (qk(t + 1) if t + 1 < len(tiles) else None)
            vts.append(vT_ref[:, pl.ds(pl.multiple_of(tiles[t][0] * B, B), B)])
            for c in range(2):
                p = jnp.exp2(cur[c])
                part = jnp.sum(p, axis=0, keepdims=True)
                psum[c] = part if psum[c] is None else psum[c] + part
                ps[c].append(p.astype(BF16))
        vt_all = jnp.concatenate(vts, axis=1)
        for c in range(2):
            acc_ref[c] = acc_ref[c] + _dot(vt_all, jnp.concatenate(ps[c], axis=0))
            l_ref[c] = l_ref[c] + psum[c]

    n_far = jnp.maximum(qi - 1, 0)
    far = lambda first, n: [(first + t, "far") for t in range(n)]
    edge = [(qi - 1, "prev"), (qi, "diag")]

    def far_group(g, carry):
        process_fast(far(FAR_GROUP * g, FAR_GROUP))
        return carry

    n_groups = n_far // FAR_GROUP
    lax.fori_loop(0, n_groups, far_group, 0)
    done = n_groups * FAR_GROUP
    rem = n_far - done
    size = FAR_GROUP // 2
    while size >= 1:
        first = done + (rem & ~(2 * size - 1))
        has, is_last = (rem & size) != 0, (rem & (size - 1)) == 0

        @pl.when(jnp.logical_and(has, jnp.logical_not(is_last)))
        def _leftover(first=first, size=size):
            process_fast(far(first, size))

        @pl.when(jnp.logical_and(has, is_last))
        def _leftover_and_edge(first=first, size=size):
            process_fast(far(first, size) + edge)
        size //= 2

    @pl.when(jnp.logical_and(rem == 0, qi >= 1))
    def _edge_only():
        process_fast(edge)

    @pl.when(qi == 0)
    def _diagonal_only():
        process_fast([(qi, "diag")])

    def causal_edge_tiles():
        @pl.when(qi >= 1)
        def _previous_and_diagonal():
            process(edge)

        @pl.when(qi == 0)
        def _diagonal_only_slow():
            process([(qi, "diag")])

    def write_output():
        lam = (jnp.exp(jnp.sum(lam_ref[0:1, :] * lam_ref[1:2, :], axis=-1, keepdims=True))
               - jnp.exp(jnp.sum(lam_ref[2:3, :] * lam_ref[3:4, :], axis=-1, keepdims=True)) + lambda_init)
        oT = acc_ref[0] * (1.0 / l_ref[0]) - acc_ref[1] * (lam / l_ref[1])
        inv = lax.rsqrt(jnp.mean(oT * oT, axis=0, keepdims=True) + EPS)
        y = oT * inv * g_ref[...] * (1.0 - lambda_init)
        o_ref[...] = y.T.astype(o_ref.dtype)

    write_output()
    not_finite = lambda x: jnp.where(jnp.isfinite(x), 0.0, 1.0)
    l_low = jnp.minimum(l_ref[0], l_ref[1])
    flags = (jnp.sum(not_finite(acc_ref[0]) + not_finite(acc_ref[1]), axis=0, keepdims=True)
             + not_finite(l_ref[0]) + not_finite(l_ref[1]) + jnp.where(l_low >= FIXED_REF_MIN_SUM, 0.0, 1.0))

    @pl.when(jnp.sum(flags) > 0.0)
    def _redo_with_running_max():
        m_ref[...] = jnp.full(m_ref.shape, NEG, F32)
        l_ref[...] = jnp.zeros(l_ref.shape, F32)
        acc_ref[...] = jnp.zeros(acc_ref.shape, F32)
        causal_edge_tiles()

        def one_far_tile(j, carry):
            process([(j, "far")])
            return carry

        lax.fori_loop(0, n_far, one_far_tile, 0)
        write_output()


def _attention(tbl, lam, g, qT, k, vT, lambda_init, block, weights):
    heads, T, hw = k.shape
    dv = vT.shape[0] // heads
    B = block
    nq = T // B
    steps = heads * nq
    slabs = [w for w, _ in weights]

    def block_rows(s):
        return next(r for r in range(CAST_ROWS, s.shape[1] + 1, CAST_ROWS)
                    if s.shape[1] % r == 0 and s.shape[1] // r <= max(steps // 4, 1))

    def slab_spec(s, layer=None):
        r = block_rows(s)
        blk = lambda h, i: ((h * nq + i) * (s.shape[1] // r)) // steps
        if layer is None:
            return pl.BlockSpec((r, s.shape[2]), lambda h, i: (blk(h, i), 0))
        return pl.BlockSpec((None, r, s.shape[2]), lambda h, i: (layer, blk(h, i), 0))

    smem = pl.BlockSpec(memory_space=pltpu.SMEM)
    out = pl.pallas_call(
        functools.partial(_attn_kernel, n_casts=len(slabs), lambda_init=lambda_init),
        out_shape=(jax.ShapeDtypeStruct((T, heads * dv), BF16),
                   *[jax.ShapeDtypeStruct(s.shape[1:], BF16) for s in slabs]),
        grid=(heads, nq),
        in_specs=[smem,
                  pl.BlockSpec(lam.shape, lambda h, i: (0, 0)),
                  pl.BlockSpec(g.shape, lambda h, i: (0, 0)),
                  pl.BlockSpec((hw, B), lambda h, i: (h, i)),
                  pl.BlockSpec((None, T, hw), lambda h, i: (h, 0, 0)),
                  pl.BlockSpec((dv, T), lambda h, i: (h, 0)),
                  *[slab_spec(s, layer) for s, (_, layer) in zip(slabs, weights)]],
        out_specs=(pl.BlockSpec((B, dv), lambda h, i: (i, h)), *[slab_spec(s) for s in slabs]),
        scratch_shapes=[pltpu.VMEM((2, hw, B), BF16), pltpu.VMEM((B, B), F32), pltpu.VMEM((LANES, B), F32),
                        pltpu.VMEM((2, 1, B), F32), pltpu.VMEM((2, 1, B), F32), pltpu.VMEM((2, dv, B), F32)],
        compiler_params=_params(("arbitrary", "arbitrary")),
        name="diff_attention",
    )(tbl, lam, g, qT, k, vT, *slabs)
    return out[0], list(out[1:])


def kernel(x, a_w_in, a_w_out, a_gnorm, a_lb_logits, b_w_q, b_w_o, b_lam_q1, b_lam_k1, b_lam_q2, b_lam_k2,
           b_subln, kv_norm, kv_w, rel_table, norm_mix, norm_ffn, ffn_w_up, ffn_conv_w, ffn_conv_b,
           ffn_w_down, final_norm):
    batch, T, D = x.shape
    depth = norm_mix.shape[0]
    n_a = a_w_in.shape[0]
    qdim = b_w_q.shape[2]
    row = lambda v: v.reshape(1, -1).astype(F32)
    tm = min(512, T)
    chunk = min(HG_CHUNK, T)
    block = min(ATT_BLOCK, T)
    tbl = rel_table.astype(F32).T.reshape(-1)

    outs = []
    for bi in range(batch):
        xs = x[bi]
        qT = k3 = vT = None
        for li in range(depth):
            if li == n_a:
                j = 0
                qT, k3, vT = _qkv(xs, row(kv_norm), row(norm_mix[li]), kv_w[:, :qdim].astype(BF16),
                                  kv_w[:, qdim:].astype(BF16), b_w_q[j].astype(BF16), DA_HEADS,
                                  DA_HEAD_DIM ** -0.5 * LOG2E, tm)
            if li < n_a:
                mix, gs = _hg_mixer(xs, row(norm_mix[li]), a_w_in[li].astype(BF16), a_lb_logits.astype(F32), li,
                                    HG_HEADS, tm, chunk)
                gate, w_mix = (gs, row(a_gnorm[li])), a_w_out[li].astype(BF16)
                w_up, w_dn = ffn_w_up[li].astype(BF16), ffn_w_down[li].astype(BF16)
            else:
                j = li - n_a
                if j > 0:
                    qT, _, _ = _qkv(xs, row(kv_norm), row(norm_mix[li]), kv_w[:, :qdim].astype(BF16),
                                    kv_w[:, qdim:].astype(BF16), b_w_q[j].astype(BF16), DA_HEADS,
                                    DA_HEAD_DIM ** -0.5 * LOG2E, tm)
                lambda_init = 0.8 - 0.6 * math.exp(-0.3 * li)
                lam = jnp.stack([b_lam_q1[j], b_lam_k1[j], b_lam_q2[j], b_lam_k2[j]]).astype(F32)
                mix, (w_mix, w_up, w_dn) = _attention(tbl, lam, b_subln[j].reshape(-1, 1).astype(F32), qT, k3, vT,
                                                      lambda_init, block, [(b_w_o, j), (ffn_w_up, li), (ffn_w_down, li)])
                gate = None
            last = li == depth - 1
            xs = _ffn(xs, mix, gate, w_mix, row(norm_ffn[li]), w_up, ffn_conv_w[li].astype(F32),
                      row(ffn_conv_b[li]), w_dn, row(final_norm), last, tm)
        outs.append(xs)
    return jnp.stack(outs).astype(x.dtype)
```

```python
import functools
import math

import jax
import jax.numpy as jnp
from jax import lax
from jax.experimental import pallas as pl
from jax.experimental.pallas import tpu as pltpu

F32 = jnp.float32
BF16 = jnp.bfloat16
EPS = 1e-6

LANES = 128
SUBLANES = 8
MXU_COLS = 256
VMEM_LIMIT_BYTES = 56 * 2**20

HG_HEADS = 8
DA_HEADS = 8
DA_HEAD_DIM = 64
REL_BUCKETS = 32
REL_MAX_DIST = 128
CONV_W = 3
NEG = -1e30
LOG2E = math.log2(math.e)

HG_CHUNK = 256
HG_PROJ_CHUNKS = 2
HG_TILE = 1024
ATT_BLOCK = 512
FAR_GROUP = 16
CAST_ROWS = 2 * SUBLANES
FIXED_REF_MIN_SUM = 2.0 ** -100


def _t5_bucket_lower_bounds():
    max_exact = REL_BUCKETS // 2
    lo = list(range(max_exact))
    bucket_of = lambda n: min(
        max_exact + int(math.log(n / max_exact) / math.log(REL_MAX_DIST / max_exact) * (REL_BUCKETS - max_exact)),
        REL_BUCKETS - 1)
    n = max_exact
    for b in range(max_exact, REL_BUCKETS):
        while bucket_of(n) < b:
            n += 1
        lo.append(n)
    return tuple(lo)


T5_BUCKET_LO = _t5_bucket_lower_bounds()


def _dot(a, b):
    return jnp.dot(a, b, preferred_element_type=F32)


def _dot_nt(a, b):
    return lax.dot_general(a, b, (((1,), (1,)), ((), ())), preferred_element_type=F32)


def _dot_tn(a, b):
    return lax.dot_general(a, b, (((0,), (0,)), ((), ())), preferred_element_type=F32)


def _neg_abs(x):
    bits = pltpu.bitcast(x, jnp.uint32) | jnp.uint32(0x80000000)
    return pltpu.bitcast(bits, F32)


def _rms_inv(x):
    return lax.rsqrt(jnp.mean(x * x, axis=-1, keepdims=True) + EPS)


def _resident(shape):
    nd = len(shape)
    return pl.BlockSpec(shape, lambda *_: (0,) * nd, pipeline_mode=pl.Buffered(1))


def _params(sem):
    return pltpu.CompilerParams(dimension_semantics=sem, vmem_limit_bytes=VMEM_LIMIT_BYTES)


def _level_ref(bh, w):
    C, d = bh.shape
    n = 2 * w
    if w >= SUBLANES:
        return jnp.concatenate(
            [jnp.broadcast_to(bh[i * n + w - 1:i * n + w, :], (n, d)) for i in range(C // n)], axis=0)
    b3 = bh.reshape(C // SUBLANES, SUBLANES, d)
    pick = lambda s: jnp.broadcast_to(b3[:, s:s + 1, :], b3.shape)
    sub = lax.broadcasted_iota(jnp.int32, b3.shape, 1)
    r = pick(SUBLANES - n + w - 1)
    for s in range(SUBLANES - 2 * n, -1, -n):
        r = jnp.where(sub < s + n, pick(s + w - 1), r)
    return r.reshape(C, d)


def _hg_mixer_kernel(x_ref, g_ref, w_ref, lbl_ref, o_ref, gs_ref, st_ref, q_s, k_s, v_s, b_s, *, layer, heads, chunk):
    tm = x_ref.shape[0]
    C = chunk
    fd = o_ref.shape[1]
    dk = fd // heads
    dv = fd // heads
    n_chunks = tm // C

    @pl.when(pl.program_id(0) == 0)
    def _zero_state():
        st_ref[...] = jnp.zeros_like(st_ref)

    x = x_ref[...]
    h = (x * _rms_inv(x) * g_ref[...]).astype(BF16)
    lg = lbl_ref[...]
    ex = jnp.exp(lg - jnp.max(lg, axis=0, keepdims=True))
    sm = ex / jnp.sum(ex, axis=0, keepdims=True)
    lb = jnp.sum(sm[:layer + 1], axis=0, keepdims=True)
    log_lb = jnp.log(lb) * LOG2E
    log_1m_lb = jnp.log1p(-lb) * LOG2E

    def softplus_neg_abs(d):
        return jnp.log2(1.0 + jnp.exp2(_neg_abs(d)))

    def silu(z):
        return z / (1.0 + jnp.exp(-z))

    row = lax.broadcasted_iota(jnp.int32, (C, C), 0)
    col = lax.broadcasted_iota(jnp.int32, (C, C), 1)
    tri = jnp.where(row >= col, 1.0, 0.0).astype(BF16)
    tri3 = jnp.concatenate([tri, tri, tri], axis=1)
    levels = [1 << i for i in range(int(math.log2(C)))]
    xor = row ^ col
    below = row > col
    level_mask = [below & ((xor >> int(math.log2(w))) == 1) for w in levels]
    diag = row == col

    G = HG_PROJ_CHUNKS

    def project(i, c0):
        rows = slice(i * C, (i + G) * C)
        hc = h[rows]
        cs = slice(c0, c0 + MXU_COLS)
        wcol = lambda part: w_ref[:, part * fd + c0:part * fd + c0 + MXU_COLS]
        chunk_rows = [(i + n, slice(n * C, (n + 1) * C)) for n in range(G)]

        def stage(buf, val):
            for n, r in chunk_rows:
                buf[n, :, cs] = val[r]

        stage(q_s, silu(_dot(hc, wcol(0))).astype(BF16))
        zf = _dot(hc, wcol(1)) * LOG2E
        stage(v_s, _dot(hc, wcol(2)).astype(BF16))
        c = log_1m_lb[:, cs] + jnp.minimum(zf, 0.0) - softplus_neg_abs(zf)
        a = log_lb[:, cs]
        lf = jnp.maximum(a, c) + softplus_neg_abs(a - c)
        stage(k_s, jnp.exp2(c - zf).astype(BF16))
        gs_ref[rows, cs] = silu(_dot(hc, wcol(3))).astype(gs_ref.dtype)
        p0 = lf.astype(BF16)
        r1 = lf - p0.astype(F32)
        p1 = r1.astype(BF16)
        p2 = (r1 - p1.astype(F32)).astype(BF16)
        for n, r in chunk_rows:
            b_s[n, :, cs] = _dot(tri3, jnp.concatenate([p0[r], p1[r], p2[r]], axis=0))

    decay = lambda expo: jnp.exp2(expo).astype(BF16)

    def recur(i, hh):
        rows = slice(i * C, (i + 1) * C)
        sk = slice(hh * dk, (hh + 1) * dk)
        sv = slice(hh * dv, (hh + 1) * dv)
        bh = b_s[i, :, sk]
        qh = q_s[i, :, sk]
        kh = k_s[i, :, sk]
        vh = v_s[i, :, sv]
        b_last = bh[C - 1:C, :]
        a = jnp.where(diag, _dot_nt(qh, kh), 0.0)
        for w, mk in zip(levels, level_mask):
            e = decay(_neg_abs(bh - _level_ref(bh, w)))
            a = jnp.where(mk, _dot_nt(qh * e, kh * e), a)
        o = _dot(a.astype(BF16), vh)
        st = st_ref[hh]
        o = o + _dot_nt(qh * decay(bh), st.astype(BF16))
        st_ref[hh] = st * jnp.exp2(b_last) + _dot_tn(vh, kh * decay(b_last - bh))
        o_ref[rows, sv] = o.astype(o_ref.dtype)

    col_starts = list(range(0, fd, MXU_COLS))
    per_cols = G * heads // len(col_starts)
    for c0 in col_starts:
        project(0, c0)
    for i in range(0, n_chunks, G):
        units = [(i + n, hh) for n in range(G) for hh in range(heads)]
        for n, c0 in enumerate(col_starts):
            if i + G < n_chunks:
                project(i + G, c0)
            for ci, hh in units[n * per_cols:(n + 1) * per_cols]:
                recur(ci, hh)


def _hg_mixer(x, g, w, lb_logits, layer, heads, tm, chunk):
    T, D = x.shape
    fd = lb_logits.shape[1]
    assert w.shape[1] == 4 * fd
    row = lambda n: pl.BlockSpec((tm, n), lambda i: (i, 0))
    stage = lambda dt: pltpu.VMEM((tm // chunk, chunk, fd), dt)
    return pl.pallas_call(
        functools.partial(_hg_mixer_kernel, layer=layer, heads=heads, chunk=chunk),
        out_shape=(jax.ShapeDtypeStruct((T, fd), BF16), jax.ShapeDtypeStruct((T, fd), BF16)),
        grid=(T // tm,),
        in_specs=[row(D), _resident((1, D)), _resident(w.shape), _resident(lb_logits.shape)],
        out_specs=(row(fd), row(fd)),
        scratch_shapes=[pltpu.VMEM((heads, fd // heads, fd // heads), F32),
                        stage(BF16), stage(BF16), stage(BF16), stage(F32)],
        compiler_params=_params(("arbitrary",)),
        name="hgrn2_mixer",
    )(x, g, w, lb_logits)


def _ffn_kernel(x_ref, a_ref, *refs, gated, final_norm):
    if gated:
        gs_ref, gn_ref, *refs = refs
    wpre_ref, g_ref, wup_ref, cw_ref, cb_ref, wdn_ref, fg_ref, y_ref, u_ref, act_ref = refs
    tm = x_ref.shape[0]
    ff = wdn_ref.shape[0]
    halo = SUBLANES

    @pl.when(pl.program_id(0) == 0)
    def _zero_halo():
        u_ref[0:halo, :] = jnp.zeros((halo, u_ref.shape[1]), F32)

    halves = [slice(r0, r0 + tm // 2) for r0 in range(0, tm, tm // 2)]
    xs, hs = [], []
    for rows in halves:
        a = a_ref[rows, :]
        if gated:
            o = a.astype(F32)
            a = (o * _rms_inv(o) * gn_ref[...] * gs_ref[rows, :].astype(F32)).astype(BF16)
        x = x_ref[rows, :] + _dot(a, wpre_ref[...])
        xs.append(x)
        hs.append((x * _rms_inv(x) * g_ref[...]).astype(BF16))
    h = jnp.concatenate(hs, axis=0)
    up_cols = 512
    for c0 in range(0, 2 * ff, up_cols):
        u_ref[halo:halo + tm, c0:c0 + up_cols] = _dot(h, wup_ref[:, c0:c0 + up_cols])

    def conv(cs):
        acc = cb_ref[:, cs] + cw_ref[CONV_W - 1:CONV_W, cs] * u_ref[halo:halo + tm, cs]
        for j in range(CONV_W - 1):
            off = halo - (CONV_W - 1) + j
            acc = acc + cw_ref[j:j + 1, cs] * u_ref[off:off + tm, cs]
        return acc

    act_cols = 256
    for c0 in range(0, ff, act_cols):
        gate = conv(slice(c0, c0 + act_cols))
        val = conv(slice(ff + c0, ff + c0 + act_cols))
        act_ref[:, c0:c0 + act_cols] = (gate * jax.nn.sigmoid(gate) * val).astype(BF16)

    u_ref[0:halo, :] = u_ref[tm:tm + halo, :]
    for rows, x in zip(halves, xs):
        y = x + _dot(act_ref[rows, :], wdn_ref[...])
        if final_norm:
            y = y * _rms_inv(y) * fg_ref[...]
        y_ref[rows, :] = y


def _ffn(x, a, gate, wpre, g, wup, cw, cb, wdn, fg, final_norm, tm):
    T, D = x.shape
    ff = wdn.shape[0]
    row = lambda n: pl.BlockSpec((tm, n), lambda i: (i, 0))
    resident = [wpre, g, wup, cw, cb, wdn, fg]
    operands = [x, a] + ([gate[0], gate[1]] if gate else []) + resident
    in_specs = ([row(D), row(a.shape[1])] + ([row(gate[0].shape[1]), _resident(gate[1].shape)] if gate else [])
                + [_resident(r.shape) for r in resident])
    return pl.pallas_call(
        functools.partial(_ffn_kernel, gated=gate is not None, final_norm=final_norm),
        out_shape=jax.ShapeDtypeStruct((T, D), F32),
        grid=(T // tm,),
        in_specs=in_specs,
        out_specs=row(D),
        scratch_shapes=[pltpu.VMEM((tm + SUBLANES, 2 * ff), F32), pltpu.VMEM((tm, ff), BF16)],
        compiler_params=_params(("arbitrary",)),
        name="conv_ffn",
    )(*operands)


def _qkv_kernel(x_ref, gkv_ref, gq_ref, wk_ref, wv_ref, wq_ref, qT_ref, k_ref, vT_ref, *, scale):
    heads = k_ref.shape[0]
    hd = k_ref.shape[2]
    x = x_ref[...]
    xn = x * _rms_inv(x)
    hkv = (xn * gkv_ref[...]).astype(BF16)
    hq = (xn * gq_ref[...]).astype(BF16)
    k = _dot(hkv, wk_ref[...]).astype(BF16)
    for h in range(heads):
        k_ref[h] = k[:, h * hd:(h + 1) * hd]
    vT_ref[...] = _dot(hkv, wv_ref[...]).T.astype(BF16)
    qT_ref[...] = (_dot(hq, wq_ref[...]) * scale).T.astype(BF16)


def _qkv(x, gkv, gq, wk, wv, wq, heads, scale, tm):
    T, D = x.shape
    hd = wk.shape[1] // heads
    colblk = lambda n: pl.BlockSpec((n, tm), lambda i: (0, i))
    return pl.pallas_call(
        functools.partial(_qkv_kernel, scale=scale),
        out_shape=(jax.ShapeDtypeStruct((wq.shape[1], T), BF16), jax.ShapeDtypeStruct((heads, T, hd), BF16),
                   jax.ShapeDtypeStruct((wv.shape[1], T), BF16)),
        grid=(T // tm,),
        in_specs=[pl.BlockSpec((tm, D), lambda i: (i, 0)), _resident(gkv.shape), _resident(gq.shape),
                  _resident(wk.shape), _resident(wv.shape), _resident(wq.shape)],
        out_specs=(colblk(wq.shape[1]), pl.BlockSpec((heads, tm, hd), lambda i: (0, i, 0)), colblk(wv.shape[1])),
        compiler_params=_params(("parallel",)),
        name="qkv_proj",
    )(x, gkv, gq, wk, wv, wq)


def _attn_kernel(tbl_ref, lam_ref, g_ref, qT_ref, k_ref, vT_ref, *refs, n_casts, lambda_init):
    cast_in, (o_ref, *refs) = refs[:n_casts], refs[n_casts:]
    cast_out, (qp_ref, dbias_ref, pbias_ref, m_ref, l_ref, acc_ref) = refs[:n_casts], refs[n_casts:]
    for src, dst in zip(cast_in, cast_out):
        dst[...] = src[...].astype(dst.dtype)

    hw, B = qT_ref.shape
    hd = hw // 2
    h = pl.program_id(0)
    qi = pl.program_id(1)
    nsub = B // LANES

    @pl.when(qi == 0)
    def _build_bias_tiles():
        far = tbl_ref[h * REL_BUCKETS + REL_BUCKETS - 1]
        a = lax.broadcasted_iota(jnp.int32, (LANES, LANES), 0)
        b = lax.broadcasted_iota(jnp.int32, (LANES, LANES), 1)

        def rel_bias(n):
            val = jnp.zeros(n.shape, F32)
            for bucket in range(REL_BUCKETS - 2, -1, -1):
                val = jnp.where(n < T5_BUCKET_LO[bucket + 1], (tbl_ref[h * REL_BUCKETS + bucket] - far) * LOG2E, val)
            return val

        e0 = jnp.where(b >= a, rel_bias(b - a), NEG)
        e1 = rel_bias(LANES + b - a)
        zero = jnp.zeros((LANES, LANES), F32)
        masked = jnp.full((LANES, LANES), NEG, F32)
        for r in range(nsub):
            for c in range(nsub):
                rs, cs = slice(r * LANES, (r + 1) * LANES), slice(c * LANES, (c + 1) * LANES)
                dbias_ref[rs, cs] = masked if c < r else e0 if c == r else e1 if c == r + 1 else zero
        for c in range(nsub):
            pbias_ref[:, c * LANES:(c + 1) * LANES] = e1 if c == 0 else zero

    qT = qT_ref[...].astype(F32)
    half = lax.broadcasted_iota(jnp.int32, qT.shape, 0) < hd
    qp_ref[0] = jnp.where(half, qT, 0.0).astype(BF16)
    qp_ref[1] = jnp.where(half, 0.0, qT).astype(BF16)
    l_ref[...] = jnp.zeros(l_ref.shape, F32)
    acc_ref[...] = jnp.zeros(acc_ref.shape, F32)

    def scores(j, kind, c):
        st = pl.multiple_of(j * B, B)
        if kind == "prev":
            cut = B - LANES
            last = pl.multiple_of(st + cut, LANES)
            return jnp.concatenate([_dot(k_ref[pl.ds(st, cut), :], qp_ref[c]),
                                    _dot(k_ref[pl.ds(last, LANES), :], qp_ref[c]) + pbias_ref[...]], axis=0)
        s = _dot(k_ref[pl.ds(st, B), :], qp_ref[c])
        return s + dbias_ref[...] if kind == "diag" else s

    def update(j, sc):
        st = pl.multiple_of(j * B, B)
        vt = vT_ref[:, pl.ds(st, B)]
        for c in range(2):
            m_old = m_ref[c]
            m_new = jnp.maximum(m_old, jnp.max(sc[c], axis=0, keepdims=True))
            p = jnp.exp2(sc[c] - m_new)
            alpha = jnp.exp2(m_old - m_new)
            l_ref[c] = alpha * l_ref[c] + jnp.sum(p, axis=0, keepdims=True)
            acc_ref[c] = alpha * acc_ref[c] + _dot(vt, p.astype(BF16))
            m_ref[c] = m_new

    def process(tiles):
        qk = lambda t: [scores(tiles[t][0], tiles[t][1], c) for c in range(2)]
        nxt = qk(0)
        for t in range(len(tiles)):
            cur, nxt = nxt, (qk(t + 1) if t + 1 < len(tiles) else None)
            update(tiles[t][0], cur)

    def process_fast(tiles):
        qk = lambda t: [scores(tiles[t][0], tiles[t][1], c) for c in range(2)]
        psum = [None, None]
        ps = [[], []]
        vts = []
        nxt = qk(0)
        for t in range(len(tiles)):
            cur, nxt = nxt, (qk(t + 1) if t + 1 < len(tiles) else None)
            vts.append(vT_ref[:, pl.ds(pl.multiple_of(tiles[t][0] * B, B), B)])
            for c in range(2):
                p = jnp.exp2(cur[c])
                part = jnp.sum(p, axis=0, keepdims=True)
                psum[c] = part if psum[c] is None else psum[c] + part
                ps[c].append(p.astype(BF16))
        vt_all = jnp.concatenate(vts, axis=1)
        for c in range(2):
            acc_ref[c] = acc_ref[c] + _dot(vt_all, jnp.concatenate(ps[c], axis=0))
            l_ref[c] = l_ref[c] + psum[c]

    n_far = jnp.maximum(qi - 1, 0)
    far = lambda first, n: [(first + t, "far") for t in range(n)]
    edge = [(qi - 1, "prev"), (qi, "diag")]

    def far_group(g, carry):
        process_fast(far(FAR_GROUP * g, FAR_GROUP))
        return carry

    n_groups = n_far // FAR_GROUP
    lax.fori_loop(0, n_groups, far_group, 0)
    done = n_groups * FAR_GROUP
    rem = n_far - done
    size = FAR_GROUP // 2
    while size >= 1:
        first = done + (rem & ~(2 * size - 1))
        has, is_last = (rem & size) != 0, (rem & (size - 1)) == 0

        @pl.when(jnp.logical_and(has, jnp.logical_not(is_last)))
        def _leftover(first=first, size=size):
            process_fast(far(first, size))

        @pl.when(jnp.logical_and(has, is_last))
        def _leftover_and_edge(first=first, size=size):
            process_fast(far(first, size) + edge)
        size //= 2

    @pl.when(jnp.logical_and(rem == 0, qi >= 1))
    def _edge_only():
        process_fast(edge)

    @pl.when(qi == 0)
    def _diagonal_only():
        process_fast([(qi, "diag")])

    def causal_edge_tiles():
        @pl.when(qi >= 1)
        def _previous_and_diagonal():
            process(edge)

        @pl.when(qi == 0)
        def _diagonal_only_slow():
            process([(qi, "diag")])

    def write_output():
        lam = (jnp.exp(jnp.sum(lam_ref[0:1, :] * lam_ref[1:2, :], axis=-1, keepdims=True))
               - jnp.exp(jnp.sum(lam_ref[2:3, :] * lam_ref[3:4, :], axis=-1, keepdims=True)) + lambda_init)
        oT = acc_ref[0] * (1.0 / l_ref[0]) - acc_ref[1] * (lam / l_ref[1])
        inv = lax.rsqrt(jnp.mean(oT * oT, axis=0, keepdims=True) + EPS)
        y = oT * inv * g_ref[...] * (1.0 - lambda_init)
        o_ref[...] = y.T.astype(o_ref.dtype)

    write_output()
    not_finite = lambda x: jnp.where(jnp.isfinite(x), 0.0, 1.0)
    l_low = jnp.minimum(l_ref[0], l_ref[1])
    flags = (jnp.sum(not_finite(acc_ref[0]) + not_finite(acc_ref[1]), axis=0, keepdims=True)
             + not_finite(l_ref[0]) + not_finite(l_ref[1]) + jnp.where(l_low >= FIXED_REF_MIN_SUM, 0.0, 1.0))

    @pl.when(jnp.sum(flags) > 0.0)
    def _redo_with_running_max():
        m_ref[...] = jnp.full(m_ref.shape, NEG, F32)
        l_ref[...] = jnp.zeros(l_ref.shape, F32)
        acc_ref[...] = jnp.zeros(acc_ref.shape, F32)
        causal_edge_tiles()

        def one_far_tile(j, carry):
            process([(j, "far")])
            return carry

        lax.fori_loop(0, n_far, one_far_tile, 0)
        write_output()


def _attention(tbl, lam, g, qT, k, vT, lambda_init, block, weights):
    heads, T, hw = k.shape
    dv = vT.shape[0] // heads
    B = block
    nq = T // B
    steps = heads * nq
    slabs = [w for w, _ in weights]

    def block_rows(s):
        return next(r for r in range(CAST_ROWS, s.shape[1] + 1, CAST_ROWS)
                    if s.shape[1] % r == 0 and s.shape[1] // r <= max(steps // 4, 1))

    def slab_spec(s, layer=None):
        r = block_rows(s)
        blk = lambda h, i: ((h * nq + i) * (s.shape[1] // r)) // steps
        if layer is None:
            return pl.BlockSpec((r, s.shape[2]), lambda h, i: (blk(h, i), 0))
        return pl.BlockSpec((None, r, s.shape[2]), lambda h, i: (layer, blk(h, i), 0))

    smem = pl.BlockSpec(memory_space=pltpu.SMEM)
    out = pl.pallas_call(
        functools.partial(_attn_kernel, n_casts=len(slabs), lambda_init=lambda_init),
        out_shape=(jax.ShapeDtypeStruct((T, heads * dv), BF16),
                   *[jax.ShapeDtypeStruct(s.shape[1:], BF16) for s in slabs]),
        grid=(heads, nq),
        in_specs=[smem,
                  pl.BlockSpec(lam.shape, lambda h, i: (0, 0)),
                  pl.BlockSpec(g.shape, lambda h, i: (0, 0)),
                  pl.BlockSpec((hw, B), lambda h, i: (h, i)),
                  pl.BlockSpec((None, T, hw), lambda h, i: (h, 0, 0)),
                  pl.BlockSpec((dv, T), lambda h, i: (h, 0)),
                  *[slab_spec(s, layer) for s, (_, layer) in zip(slabs, weights)]],
        out_specs=(pl.BlockSpec((B, dv), lambda h, i: (i, h)), *[slab_spec(s) for s in slabs]),
        scratch_shapes=[pltpu.VMEM((2, hw, B), BF16), pltpu.VMEM((B, B), F32), pltpu.VMEM((LANES, B), F32),
                        pltpu.VMEM((2, 1, B), F32), pltpu.VMEM((2, 1, B), F32), pltpu.VMEM((2, dv, B), F32)],
        compiler_params=_params(("arbitrary", "arbitrary")),
        name="diff_attention",
    )(tbl, lam, g, qT, k, vT, *slabs)
    return out[0], list(out[1:])


def kernel(x, a_w_in, a_w_out, a_gnorm, a_lb_logits, b_w_q, b_w_o, b_lam_q1, b_lam_k1, b_lam_q2, b_lam_k2,
           b_subln, kv_norm, kv_w, rel_table, norm_mix, norm_ffn, ffn_w_up, ffn_conv_w, ffn_conv_b,
           ffn_w_down, final_norm):
    batch, T, D = x.shape
    depth = norm_mix.shape[0]
    n_a = a_w_in.shape[0]
    qdim = b_w_q.shape[2]
    row = lambda v: v.reshape(1, -1).astype(F32)
    tm = min(512, T)
    chunk = min(HG_CHUNK, T)
    block = min(ATT_BLOCK, T)
    tbl = rel_table.astype(F32).T.reshape(-1)

    outs = []
    for bi in range(batch):
        xs = x[bi]
        qT = k3 = vT = None
        for li in range(depth):
            if li == n_a:
                j = 0
                qT, k3, vT = _qkv(xs, row(kv_norm), row(norm_mix[li]), kv_w[:, :qdim].astype(BF16),
                                  kv_w[:, qdim:].astype(BF16), b_w_q[j].astype(BF16), DA_HEADS,
                                  DA_HEAD_DIM ** -0.5 * LOG2E, tm)
            if li < n_a:
                mix, gs = _hg_mixer(xs, row(norm_mix[li]), a_w_in[li].astype(BF16), a_lb_logits.astype(F32), li,
                                    HG_HEADS, min(HG_TILE, T), chunk)
                gate, w_mix = (gs, row(a_gnorm[li])), a_w_out[li].astype(BF16)
                w_up, w_dn = ffn_w_up[li].astype(BF16), ffn_w_down[li].astype(BF16)
            else:
                j = li - n_a
                if j > 0:
                    qT, _, _ = _qkv(xs, row(kv_norm), row(norm_mix[li]), kv_w[:, :qdim].astype(BF16),
                                    kv_w[:, qdim:].astype(BF16), b_w_q[j].astype(BF16), DA_HEADS,
                                    DA_HEAD_DIM ** -0.5 * LOG2E, tm)
                lambda_init = 0.8 - 0.6 * math.exp(-0.3 * li)
                lam = jnp.stack([b_lam_q1[j], b_lam_k1[j], b_lam_q2[j], b_lam_k2[j]]).astype(F32)
                mix, (w_mix, w_up, w_dn) = _attention(tbl, lam, b_subln[j].reshape(-1, 1).astype(F32), qT, k3, vT,
                                                      lambda_init, block, [(b_w_o, j), (ffn_w_up, li), (ffn_w_down, li)])
                gate = None
            last = li == depth - 1
            xs = _ffn(xs, mix, gate, w_mix, row(norm_ffn[li]), w_up, ffn_conv_w[li].astype(F32),
                      row(ffn_conv_b[li]), w_dn, row(final_norm), last, tm)
        outs.append(xs)
    return jnp.stack(outs).astype(x.dtype)
```

```python
import functools
import math

import jax
import jax.numpy as jnp
from jax import lax
from jax.experimental import pallas as pl
from jax.experimental.pallas import tpu as pltpu

F32 = jnp.float32
BF16 = jnp.bfloat16
EPS = 1e-6

LANES = 128
SUBLANES = 8
MXU_COLS = 256
VMEM_LIMIT_BYTES = 56 * 2**20

HG_HEADS = 8
DA_HEADS = 8
DA_HEAD_DIM = 64
REL_BUCKETS = 32
REL_MAX_DIST = 128
CONV_W = 3
NEG = -1e30
LOG2E = math.log2(math.e)

HG_CHUNK = 256
HG_PROJ_CHUNKS = 2
ATT_BLOCK = 512
FAR_GROUP = 16
CAST_ROWS = 2 * SUBLANES
FIXED_REF_MIN_SUM = 2.0 ** -100


def _t5_bucket_lower_bounds():
    max_exact = REL_BUCKETS // 2
    lo = list(range(max_exact))
    bucket_of = lambda n: min(
        max_exact + int(math.log(n / max_exact) / math.log(REL_MAX_DIST / max_exact) * (REL_BUCKETS - max_exact)),
        REL_BUCKETS - 1)
    n = max_exact
    for b in range(max_exact, REL_BUCKETS):
        while bucket_of(n) < b:
            n += 1
        lo.append(n)
    return tuple(lo)


T5_BUCKET_LO = _t5_bucket_lower_bounds()


def _dot(a, b):
    return jnp.dot(a, b, preferred_element_type=F32)


def _dot_nt(a, b):
    return lax.dot_general(a, b, (((1,), (1,)), ((), ())), preferred_element_type=F32)


def _dot_tn(a, b):
    return lax.dot_general(a, b, (((0,), (0,)), ((), ())), preferred_element_type=F32)


def _neg_abs(x):
    bits = pltpu.bitcast(x, jnp.uint32) | jnp.uint32(0x80000000)
    return pltpu.bitcast(bits, F32)


def _rms_inv(x):
    return lax.rsqrt(jnp.mean(x * x, axis=-1, keepdims=True) + EPS)


def _resident(shape):
    nd = len(shape)
    return pl.BlockSpec(shape, lambda *_: (0,) * nd, pipeline_mode=pl.Buffered(1))


def _params(sem):
    return pltpu.CompilerParams(dimension_semantics=sem, vmem_limit_bytes=VMEM_LIMIT_BYTES)


def _cast_blocks(weights, steps, step_of):
    in_specs, out_specs = [], []
    for w, layer in weights:
        rows, cols = w.shape[1:]
        r = next(r for r in range(CAST_ROWS, rows + 1, CAST_ROWS) if rows % r == 0 and rows // r <= steps)
        blk = lambda *g, n=rows // r: (step_of(*g) * n) // steps
        in_specs.append(pl.BlockSpec((None, r, cols), lambda *g, blk=blk, layer=layer: (layer, blk(*g), 0)))
        out_specs.append(pl.BlockSpec((r, cols), lambda *g, blk=blk: (blk(*g), 0)))
    shapes = [jax.ShapeDtypeStruct(w.shape[1:], BF16) for w, _ in weights]
    return [w for w, _ in weights], in_specs, out_specs, shapes


def _level_ref(bh, w):
    C, d = bh.shape
    n = 2 * w
    if w >= SUBLANES:
        return jnp.concatenate(
            [jnp.broadcast_to(bh[i * n + w - 1:i * n + w, :], (n, d)) for i in range(C // n)], axis=0)
    b3 = bh.reshape(C // SUBLANES, SUBLANES, d)
    pick = lambda s: jnp.broadcast_to(b3[:, s:s + 1, :], b3.shape)
    sub = lax.broadcasted_iota(jnp.int32, b3.shape, 1)
    r = pick(SUBLANES - n + w - 1)
    for s in range(SUBLANES - 2 * n, -1, -n):
        r = jnp.where(sub < s + n, pick(s + w - 1), r)
    return r.reshape(C, d)


def _hg_mixer_kernel(x_ref, g_ref, w_ref, lbl_ref, o_ref, gs_ref, st_ref, q_s, k_s, v_s, b_s, *, layer, heads, chunk):
    tm = x_ref.shape[0]
    C = chunk
    fd = o_ref.shape[1]
    dk = fd // heads
    dv = fd // heads
    n_chunks = tm // C

    @pl.when(pl.program_id(0) == 0)
    def _zero_state():
        st_ref[...] = jnp.zeros_like(st_ref)

    x = x_ref[...]
    h = (x * _rms_inv(x) * g_ref[...]).astype(BF16)
    lg = lbl_ref[...]
    ex = jnp.exp(lg - jnp.max(lg, axis=0, keepdims=True))
    sm = ex / jnp.sum(ex, axis=0, keepdims=True)
    lb = jnp.sum(sm[:layer + 1], axis=0, keepdims=True)
    log_lb = jnp.log(lb) * LOG2E
    log_1m_lb = jnp.log1p(-lb) * LOG2E

    def softplus_neg_abs(d):
        return jnp.log2(1.0 + jnp.exp2(_neg_abs(d)))

    def silu(z):
        return z / (1.0 + jnp.exp(-z))

    row = lax.broadcasted_iota(jnp.int32, (C, C), 0)
    col = lax.broadcasted_iota(jnp.int32, (C, C), 1)
    tri = jnp.where(row >= col, 1.0, 0.0).astype(BF16)
    tri3 = jnp.concatenate([tri, tri, tri], axis=1)
    levels = [1 << i for i in range(int(math.log2(C)))]
    xor = row ^ col
    below = row > col
    level_mask = [below & ((xor >> int(math.log2(w))) == 1) for w in levels]
    diag = row == col

    G = HG_PROJ_CHUNKS

    def project(i, c0):
        rows = slice(i * C, (i + G) * C)
        hc = h[rows]
        cs = slice(c0, c0 + MXU_COLS)
        wcol = lambda part: w_ref[:, part * fd + c0:part * fd + c0 + MXU_COLS]
        chunk_rows = [(i + n, slice(n * C, (n + 1) * C)) for n in range(G)]

        def stage(buf, val):
            for n, r in chunk_rows:
                buf[n, :, cs] = val[r]

        stage(q_s, silu(_dot(hc, wcol(0))).astype(BF16))
        zf = _dot(hc, wcol(1)) * LOG2E
        stage(v_s, _dot(hc, wcol(2)).astype(BF16))
        c = log_1m_lb[:, cs] + jnp.minimum(zf, 0.0) - softplus_neg_abs(zf)
        a = log_lb[:, cs]
        lf = jnp.maximum(a, c) + softplus_neg_abs(a - c)
        stage(k_s, jnp.exp2(c - zf).astype(BF16))
        gs_ref[rows, cs] = silu(_dot(hc, wcol(3))).astype(gs_ref.dtype)
        p0 = lf.astype(BF16)
        r1 = lf - p0.astype(F32)
        p1 = r1.astype(BF16)
        p2 = (r1 - p1.astype(F32)).astype(BF16)
        for n, r in chunk_rows:
            b_s[n, :, cs] = _dot(tri3, jnp.concatenate([p0[r], p1[r], p2[r]], axis=0))

    decay = lambda expo: jnp.exp2(expo).astype(BF16)

    def recur(i, hh):
        rows = slice(i * C, (i + 1) * C)
        sk = slice(hh * dk, (hh + 1) * dk)
        sv = slice(hh * dv, (hh + 1) * dv)
        bh = b_s[i, :, sk]
        qh = q_s[i, :, sk]
        kh = k_s[i, :, sk]
        vh = v_s[i, :, sv]
        b_last = bh[C - 1:C, :]
        a = jnp.where(diag, _dot_nt(qh, kh), 0.0)
        for w, mk in zip(levels, level_mask):
            e = decay(_neg_abs(bh - _level_ref(bh, w)))
            a = jnp.where(mk, _dot_nt(qh * e, kh * e), a)
        o = _dot(a.astype(BF16), vh)
        st = st_ref[hh]
        o = o + _dot_nt(qh * decay(bh), st.astype(BF16))
        st_ref[hh] = st * jnp.exp2(b_last) + _dot_tn(vh, kh * decay(b_last - bh))
        o_ref[rows, sv] = o.astype(o_ref.dtype)

    col_starts = list(range(0, fd, MXU_COLS))
    per_cols = G * heads // len(col_starts)
    for c0 in col_starts:
        project(0, c0)
    for i in range(0, n_chunks, G):
        units = [(i + n, hh) for n in range(G) for hh in range(heads)]
        for n, c0 in enumerate(col_starts):
            if i + G < n_chunks:
                project(i + G, c0)
            for ci, hh in units[n * per_cols:(n + 1) * per_cols]:
                recur(ci, hh)


def _hg_mixer(x, g, w, lb_logits, layer, heads, tm, chunk):
    T, D = x.shape
    fd = lb_logits.shape[1]
    assert w.shape[1] == 4 * fd
    row = lambda n: pl.BlockSpec((tm, n), lambda i: (i, 0))
    stage = lambda dt: pltpu.VMEM((tm // chunk, chunk, fd), dt)
    return pl.pallas_call(
        functools.partial(_hg_mixer_kernel, layer=layer, heads=heads, chunk=chunk),
        out_shape=(jax.ShapeDtypeStruct((T, fd), BF16), jax.ShapeDtypeStruct((T, fd), BF16)),
        grid=(T // tm,),
        in_specs=[row(D), _resident((1, D)), _resident(w.shape), _resident(lb_logits.shape)],
        out_specs=(row(fd), row(fd)),
        scratch_shapes=[pltpu.VMEM((heads, fd // heads, fd // heads), F32),
                        stage(BF16), stage(BF16), stage(BF16), stage(F32)],
        compiler_params=_params(("arbitrary",)),
        name="hgrn2_mixer",
    )(x, g, w, lb_logits)


def _ffn_kernel(x_ref, a_ref, *refs, gated, final_norm):
    if gated:
        gs_ref, gn_ref, *refs = refs
    wpre_ref, g_ref, wup_ref, cw_ref, cb_ref, wdn_ref, fg_ref, y_ref, u_ref, act_ref = refs
    tm = x_ref.shape[0]
    ff = wdn_ref.shape[0]
    halo = SUBLANES

    @pl.when(pl.program_id(0) == 0)
    def _zero_halo():
        u_ref[0:halo, :] = jnp.zeros((halo, u_ref.shape[1]), F32)

    halves = [slice(r0, r0 + tm // 2) for r0 in range(0, tm, tm // 2)]
    xs, hs = [], []
    for rows in halves:
        a = a_ref[rows, :]
        if gated:
            o = a.astype(F32)
            a = (o * _rms_inv(o) * gn_ref[...] * gs_ref[rows, :].astype(F32)).astype(BF16)
        x = x_ref[rows, :] + _dot(a, wpre_ref[...])
        xs.append(x)
        hs.append((x * _rms_inv(x) * g_ref[...]).astype(BF16))
    h = jnp.concatenate(hs, axis=0)
    up_cols = 512
    for c0 in range(0, 2 * ff, up_cols):
        u_ref[halo:halo + tm, c0:c0 + up_cols] = _dot(h, wup_ref[:, c0:c0 + up_cols])

    def conv(cs):
        acc = cb_ref[:, cs] + cw_ref[CONV_W - 1:CONV_W, cs] * u_ref[halo:halo + tm, cs]
        for j in range(CONV_W - 1):
            off = halo - (CONV_W - 1) + j
            acc = acc + cw_ref[j:j + 1, cs] * u_ref[off:off + tm, cs]
        return acc

    act_cols = 256
    for c0 in range(0, ff, act_cols):
        gate = conv(slice(c0, c0 + act_cols))
        val = conv(slice(ff + c0, ff + c0 + act_cols))
        act_ref[:, c0:c0 + act_cols] = (gate * jax.nn.sigmoid(gate) * val).astype(BF16)

    u_ref[0:halo, :] = u_ref[tm:tm + halo, :]
    for rows, x in zip(halves, xs):
        y = x + _dot(act_ref[rows, :], wdn_ref[...])
        if final_norm:
            y = y * _rms_inv(y) * fg_ref[...]
        y_ref[rows, :] = y


def _ffn(x, a, gate, wpre, g, wup, cw, cb, wdn, fg, final_norm, tm):
    T, D = x.shape
    ff = wdn.shape[0]
    row = lambda n: pl.BlockSpec((tm, n), lambda i: (i, 0))
    resident = [wpre, g, wup, cw, cb, wdn, fg]
    operands = [x, a] + ([gate[0], gate[1]] if gate else []) + resident
    in_specs = ([row(D), row(a.shape[1])] + ([row(gate[0].shape[1]), _resident(gate[1].shape)] if gate else [])
                + [_resident(r.shape) for r in resident])
    return pl.pallas_call(
        functools.partial(_ffn_kernel, gated=gate is not None, final_norm=final_norm),
        out_shape=jax.ShapeDtypeStruct((T, D), F32),
        grid=(T // tm,),
        in_specs=in_specs,
        out_specs=row(D),
        scratch_shapes=[pltpu.VMEM((tm + SUBLANES, 2 * ff), F32), pltpu.VMEM((tm, ff), BF16)],
        compiler_params=_params(("arbitrary",)),
        name="conv_ffn",
    )(*operands)


def _qkv_kernel(x_ref, gkv_ref, gq_ref, wk_ref, wv_ref, wq_ref, *refs, n_casts, scale):
    cast_in, (qT_ref, k_ref, vT_ref, *cast_out) = refs[:n_casts], refs[n_casts:]
    for src, dst in zip(cast_in, cast_out):
        dst[...] = src[...].astype(dst.dtype)
    heads = k_ref.shape[0]
    hd = k_ref.shape[2]
    x = x_ref[...]
    xn = x * _rms_inv(x)
    hkv = (xn * gkv_ref[...]).astype(BF16)
    hq = (xn * gq_ref[...]).astype(BF16)
    k = _dot(hkv, wk_ref[...]).astype(BF16)
    for h in range(heads):
        k_ref[h] = k[:, h * hd:(h + 1) * hd]
    vT_ref[...] = _dot(hkv, wv_ref[...]).T.astype(BF16)
    qT_ref[...] = (_dot(hq, wq_ref[...]) * scale).T.astype(BF16)


def _qkv(x, gkv, gq, wk, wv, wq, heads, scale, tm, weights=()):
    T, D = x.shape
    hd = wk.shape[1] // heads
    colblk = lambda n: pl.BlockSpec((n, tm), lambda i: (0, i))
    cast_ops, cast_in, cast_out, cast_shapes = _cast_blocks(weights, T // tm, lambda i: i)
    out = pl.pallas_call(
        functools.partial(_qkv_kernel, n_casts=len(cast_ops), scale=scale),
        out_shape=(jax.ShapeDtypeStruct((wq.shape[1], T), BF16), jax.ShapeDtypeStruct((heads, T, hd), BF16),
                   jax.ShapeDtypeStruct((wv.shape[1], T), BF16), *cast_shapes),
        grid=(T // tm,),
        in_specs=[pl.BlockSpec((tm, D), lambda i: (i, 0)), _resident(gkv.shape), _resident(gq.shape),
                  _resident(wk.shape), _resident(wv.shape), _resident(wq.shape), *cast_in],
        out_specs=(colblk(wq.shape[1]), pl.BlockSpec((heads, tm, hd), lambda i: (0, i, 0)), colblk(wv.shape[1]),
                   *cast_out),
        compiler_params=_params(("arbitrary",)),
        name="qkv_proj",
    )(x, gkv, gq, wk, wv, wq, *cast_ops)
    return out[0], out[1], out[2], list(out[3:])


def _attn_kernel(tbl_ref, lam_ref, g_ref, qT_ref, k_ref, vT_ref, o_ref,
                 qp_ref, dbias_ref, pbias_ref, m_ref, l_ref, acc_ref, *, lambda_init):
    hw, B = qT_ref.shape
    hd = hw // 2
    h = pl.program_id(0)
    qi = pl.program_id(1)
    nsub = B // LANES

    @pl.when(qi == 0)
    def _build_bias_tiles():
        far = tbl_ref[h * REL_BUCKETS + REL_BUCKETS - 1]
        a = lax.broadcasted_iota(jnp.int32, (LANES, LANES), 0)
        b = lax.broadcasted_iota(jnp.int32, (LANES, LANES), 1)

        def rel_bias(n):
            val = jnp.zeros(n.shape, F32)
            for bucket in range(REL_BUCKETS - 2, -1, -1):
                val = jnp.where(n < T5_BUCKET_LO[bucket + 1], (tbl_ref[h * REL_BUCKETS + bucket] - far) * LOG2E, val)
            return val

        e0 = jnp.where(b >= a, rel_bias(b - a), NEG)
        e1 = rel_bias(LANES + b - a)
        zero = jnp.zeros((LANES, LANES), F32)
        masked = jnp.full((LANES, LANES), NEG, F32)
        for r in range(nsub):
            for c in range(nsub):
                rs, cs = slice(r * LANES, (r + 1) * LANES), slice(c * LANES, (c + 1) * LANES)
                dbias_ref[rs, cs] = masked if c < r else e0 if c == r else e1 if c == r + 1 else zero
        for c in range(nsub):
            pbias_ref[:, c * LANES:(c + 1) * LANES] = e1 if c == 0 else zero

    qT = qT_ref[...].astype(F32)
    half = lax.broadcasted_iota(jnp.int32, qT.shape, 0) < hd
    qp_ref[0] = jnp.where(half, qT, 0.0).astype(BF16)
    qp_ref[1] = jnp.where(half, 0.0, qT).astype(BF16)
    l_ref[...] = jnp.zeros(l_ref.shape, F32)
    acc_ref[...] = jnp.zeros(acc_ref.shape, F32)

    def scores(j, kind, c):
        st = pl.multiple_of(j * B, B)
        if kind == "prev":
            cut = B - LANES
            last = pl.multiple_of(st + cut, LANES)
            return jnp.concatenate([_dot(k_ref[pl.ds(st, cut), :], qp_ref[c]),
                                    _dot(k_ref[pl.ds(last, LANES), :], qp_ref[c]) + pbias_ref[...]], axis=0)
        s = _dot(k_ref[pl.ds(st, B), :], qp_ref[c])
        return s + dbias_ref[...] if kind == "diag" else s

    def update(j, sc):
        st = pl.multiple_of(j * B, B)
        vt = vT_ref[:, pl.ds(st, B)]
        for c in range(2):
            m_old = m_ref[c]
            m_new = jnp.maximum(m_old, jnp.max(sc[c], axis=0, keepdims=True))
            p = jnp.exp2(sc[c] - m_new)
            alpha = jnp.exp2(m_old - m_new)
            l_ref[c] = alpha * l_ref[c] + jnp.sum(p, axis=0, keepdims=True)
            acc_ref[c] = alpha * acc_ref[c] + _dot(vt, p.astype(BF16))
            m_ref[c] = m_new

    def process(tiles):
        qk = lambda t: [scores(tiles[t][0], tiles[t][1], c) for c in range(2)]
        nxt = qk(0)
        for t in range(len(tiles)):
            cur, nxt = nxt, (qk(t + 1) if t + 1 < len(tiles) else None)
            update(tiles[t][0], cur)

    def process_fast(tiles):
        qk = lambda t: [scores(tiles[t][0], tiles[t][1], c) for c in range(2)]
        psum = [None, None]
        ps = [[], []]
        vts = []
        nxt = qk(0)
        for t in range(len(tiles)):
            cur, nxt = nxt, (qk(t + 1) if t + 1 < len(tiles) else None)
            vts.append(vT_ref[:, pl.ds(pl.multiple_of(tiles[t][0] * B, B), B)])
            for c in range(2):
                p = jnp.exp2(cur[c])
                part = jnp.sum(p, axis=0, keepdims=True)
                psum[c] = part if psum[c] is None else psum[c] + part
                ps[c].append(p.astype(BF16))
        vt_all = jnp.concatenate(vts, axis=1)
        for c in range(2):
            acc_ref[c] = acc_ref[c] + _dot(vt_all, jnp.concatenate(ps[c], axis=0))
            l_ref[c] = l_ref[c] + psum[c]

    n_far = jnp.maximum(qi - 1, 0)
    far = lambda first, n: [(first + t, "far") for t in range(n)]
    edge = [(qi - 1, "prev"), (qi, "diag")]

    def far_group(g, carry):
        process_fast(far(FAR_GROUP * g, FAR_GROUP))
        return carry

    n_groups = n_far // FAR_GROUP
    lax.fori_loop(0, n_groups, far_group, 0)
    done = n_groups * FAR_GROUP
    rem = n_far - done
    size = FAR_GROUP // 2
    while size >= 1:
        first = done + (rem & ~(2 * size - 1))
        has, is_last = (rem & size) != 0, (rem & (size - 1)) == 0

        @pl.when(jnp.logical_and(has, jnp.logical_not(is_last)))
        def _leftover(first=first, size=size):
            process_fast(far(first, size))

        @pl.when(jnp.logical_and(has, is_last))
        def _leftover_and_edge(first=first, size=size):
            process_fast(far(first, size) + edge)
        size //= 2

    @pl.when(jnp.logical_and(rem == 0, qi >= 1))
    def _edge_only():
        process_fast(edge)

    @pl.when(qi == 0)
    def _diagonal_only():
        process_fast([(qi, "diag")])

    def causal_edge_tiles():
        @pl.when(qi >= 1)
        def _previous_and_diagonal():
            process(edge)

        @pl.when(qi == 0)
        def _diagonal_only_slow():
            process([(qi, "diag")])

    def write_output():
        lam = (jnp.exp(jnp.sum(lam_ref[0:1, :] * lam_ref[1:2, :], axis=-1, keepdims=True))
               - jnp.exp(jnp.sum(lam_ref[2:3, :] * lam_ref[3:4, :], axis=-1, keepdims=True)) + lambda_init)
        oT = acc_ref[0] * (1.0 / l_ref[0]) - acc_ref[1] * (lam / l_ref[1])
        inv = lax.rsqrt(jnp.mean(oT * oT, axis=0, keepdims=True) + EPS)
        y = oT * inv * g_ref[...] * (1.0 - lambda_init)
        o_ref[...] = y.T.astype(o_ref.dtype)

    write_output()
    not_finite = lambda x: jnp.where(jnp.isfinite(x), 0.0, 1.0)
    l_low = jnp.minimum(l_ref[0], l_ref[1])
    flags = (jnp.sum(not_finite(acc_ref[0]) + not_finite(acc_ref[1]), axis=0, keepdims=True)
             + not_finite(l_ref[0]) + not_finite(l_ref[1]) + jnp.where(l_low >= FIXED_REF_MIN_SUM, 0.0, 1.0))

    @pl.when(jnp.sum(flags) > 0.0)
    def _redo_with_running_max():
        m_ref[...] = jnp.full(m_ref.shape, NEG, F32)
        l_ref[...] = jnp.zeros(l_ref.shape, F32)
        acc_ref[...] = jnp.zeros(acc_ref.shape, F32)
        causal_edge_tiles()

        def one_far_tile(j, carry):
            process([(j, "far")])
            return carry

        lax.fori_loop(0, n_far, one_far_tile, 0)
        write_output()


def _attention(tbl, lam, g, qT, k, vT, lambda_init, block):
    heads, T, hw = k.shape
    dv = vT.shape[0] // heads
    B = block
    smem = pl.BlockSpec(memory_space=pltpu.SMEM)
    return pl.pallas_call(
        functools.partial(_attn_kernel, lambda_init=lambda_init),
        out_shape=jax.ShapeDtypeStruct((T, heads * dv), BF16),
        grid=(heads, T // B),
        in_specs=[smem,
                  pl.BlockSpec(lam.shape, lambda h, i: (0, 0)),
                  pl.BlockSpec(g.shape, lambda h, i: (0, 0)),
                  pl.BlockSpec((hw, B), lambda h, i: (h, i)),
                  pl.BlockSpec((None, T, hw), lambda h, i: (h, 0, 0)),
                  pl.BlockSpec((dv, T), lambda h, i: (h, 0))],
        out_specs=pl.BlockSpec((B, dv), lambda h, i: (i, h)),
        scratch_shapes=[pltpu.VMEM((2, hw, B), BF16), pltpu.VMEM((B, B), F32), pltpu.VMEM((LANES, B), F32),
                        pltpu.VMEM((2, 1, B), F32), pltpu.VMEM((2, 1, B), F32), pltpu.VMEM((2, dv, B), F32)],
        compiler_params=_params(("arbitrary", "arbitrary")),
        name="diff_attention",
    )(tbl, lam, g, qT, k, vT)


def kernel(x, a_w_in, a_w_out, a_gnorm, a_lb_logits, b_w_q, b_w_o, b_lam_q1, b_lam_k1, b_lam_q2, b_lam_k2,
           b_subln, kv_norm, kv_w, rel_table, norm_mix, norm_ffn, ffn_w_up, ffn_conv_w, ffn_conv_b,
           ffn_w_down, final_norm):
    batch, T, D = x.shape
    depth = norm_mix.shape[0]
    n_a = a_w_in.shape[0]
    qdim = b_w_q.shape[2]
    row = lambda v: v.reshape(1, -1).astype(F32)
    tm = min(512, T)
    chunk = min(HG_CHUNK, T)
    block = min(ATT_BLOCK, T)
    tbl = rel_table.astype(F32).T.reshape(-1)

    outs = []
    for bi in range(batch):
        xs = x[bi]
        qT = k3 = vT = None
        for li in range(depth):
            if li == n_a:
                j = 0
                qT, k3, vT, cast = _qkv(xs, row(kv_norm), row(norm_mix[li]), kv_w[:, :qdim].astype(BF16),
                                        kv_w[:, qdim:].astype(BF16), b_w_q[j].astype(BF16), DA_HEADS,
                                        DA_HEAD_DIM ** -0.5 * LOG2E, tm,
                                        [(b_w_o, j), (ffn_w_up, li), (ffn_w_down, li)])
            if li < n_a:
                mix, gs = _hg_mixer(xs, row(norm_mix[li]), a_w_in[li].astype(BF16), a_lb_logits.astype(F32), li,
                                    HG_HEADS, tm, chunk)
                gate, w_mix = (gs, row(a_gnorm[li])), a_w_out[li].astype(BF16)
                w_up, w_dn = ffn_w_up[li].astype(BF16), ffn_w_down[li].astype(BF16)
            else:
                j = li - n_a
                if j > 0:
                    qT, _, _, cast = _qkv(xs, row(kv_norm), row(norm_mix[li]), kv_w[:, :qdim].astype(BF16),
                                          kv_w[:, qdim:].astype(BF16), b_w_q[j].astype(BF16), DA_HEADS,
                                          DA_HEAD_DIM ** -0.5 * LOG2E, tm,
                                          [(b_w_o, j), (ffn_w_up, li), (ffn_w_down, li)])
                lambda_init = 0.8 - 0.6 * math.exp(-0.3 * li)
                lam = jnp.stack([b_lam_q1[j], b_lam_k1[j], b_lam_q2[j], b_lam_k2[j]]).astype(F32)
                mix = _attention(tbl, lam, b_subln[j].reshape(-1, 1).astype(F32), qT, k3, vT, lambda_init, block)
                gate, (w_mix, w_up, w_dn) = None, cast
            last = li == depth - 1
            xs = _ffn(xs, mix, gate, w_mix, row(norm_ffn[li]), w_up, ffn_conv_w[li].astype(F32),
                      row(ffn_conv_b[li]), w_dn, row(final_norm), last, tm)
        outs.append(xs)
    return jnp.stack(outs).astype(x.dtype)
```

```python
import functools
import math

import jax
import jax.numpy as jnp
from jax import lax
from jax.experimental import pallas as pl
from jax.experimental.pallas import tpu as pltpu

F32 = jnp.float32
BF16 = jnp.bfloat16
EPS = 1e-6

LANES = 128
SUBLANES = 8
MXU_COLS = 256
VMEM_LIMIT_BYTES = 56 * 2**20

HG_HEADS = 8
DA_HEADS = 8
DA_HEAD_DIM = 64
REL_BUCKETS = 32
REL_MAX_DIST = 128
CONV_W = 3
NEG = -1e30
LOG2E = math.log2(math.e)

HG_CHUNK = 256
HG_PROJ_CHUNKS = 2
ATT_BLOCK = 512
FAR_GROUP = 16
CAST_ROWS = 2 * SUBLANES
FIXED_REF_MIN_SUM = 2.0 ** -100


def _t5_bucket_lower_bounds():
    max_exact = REL_BUCKETS // 2
    lo = list(range(max_exact))
    bucket_of = lambda n: min(
        max_exact + int(math.log(n / max_exact) / math.log(REL_MAX_DIST / max_exact) * (REL_BUCKETS - max_exact)),
        REL_BUCKETS - 1)
    n = max_exact
    for b in range(max_exact, REL_BUCKETS):
        while bucket_of(n) < b:
            n += 1
        lo.append(n)
    return tuple(lo)


T5_BUCKET_LO = _t5_bucket_lower_bounds()


def _dot(a, b):
    return jnp.dot(a, b, preferred_element_type=F32)


def _dot_nt(a, b):
    return lax.dot_general(a, b, (((1,), (1,)), ((), ())), preferred_element_type=F32)


def _dot_tn(a, b):
    return lax.dot_general(a, b, (((0,), (0,)), ((), ())), preferred_element_type=F32)


def _neg_abs(x):
    bits = pltpu.bitcast(x, jnp.uint32) | jnp.uint32(0x80000000)
    return pltpu.bitcast(bits, F32)


def _rms_inv(x):
    return lax.rsqrt(jnp.mean(x * x, axis=-1, keepdims=True) + EPS)


def _resident(shape):
    nd = len(shape)
    return pl.BlockSpec(shape, lambda *_: (0,) * nd, pipeline_mode=pl.Buffered(1))


def _params(sem):
    return pltpu.CompilerParams(dimension_semantics=sem, vmem_limit_bytes=VMEM_LIMIT_BYTES)


def _cast_blocks(weights, steps, step_of):
    in_specs, out_specs = [], []
    for w, layer in weights:
        rows, cols = w.shape[1:]
        r = next(r for r in range(CAST_ROWS, rows + 1, CAST_ROWS) if rows % r == 0 and rows // r <= steps)
        blk = lambda *g, n=rows // r: (step_of(*g) * n) // steps
        in_specs.append(pl.BlockSpec((None, r, cols), lambda *g, blk=blk, layer=layer: (layer, blk(*g), 0)))
        out_specs.append(pl.BlockSpec((r, cols), lambda *g, blk=blk: (blk(*g), 0)))
    shapes = [jax.ShapeDtypeStruct(w.shape[1:], BF16) for w, _ in weights]
    return [w for w, _ in weights], in_specs, out_specs, shapes


def _level_ref(bh, w):
    C, d = bh.shape
    n = 2 * w
    if w >= SUBLANES:
        return jnp.concatenate(
            [jnp.broadcast_to(bh[i * n + w - 1:i * n + w, :], (n, d)) for i in range(C // n)], axis=0)
    b3 = bh.reshape(C // SUBLANES, SUBLANES, d)
    pick = lambda s: jnp.broadcast_to(b3[:, s:s + 1, :], b3.shape)
    sub = lax.broadcasted_iota(jnp.int32, b3.shape, 1)
    r = pick(SUBLANES - n + w - 1)
    for s in range(SUBLANES - 2 * n, -1, -n):
        r = jnp.where(sub < s + n, pick(s + w - 1), r)
    return r.reshape(C, d)


def _hg_mixer_kernel(x_ref, g_ref, w_ref, lbl_ref, *refs, n_casts, layer, heads, chunk):
    cast_in, (o_ref, gs_ref, *refs) = refs[:n_casts], refs[n_casts:]
    cast_out, (st_ref, q_s, k_s, v_s, b_s) = refs[:n_casts], refs[n_casts:]
    for src, dst in zip(cast_in, cast_out):
        dst[...] = src[...].astype(dst.dtype)
    tm = x_ref.shape[0]
    C = chunk
    fd = o_ref.shape[1]
    dk = fd // heads
    dv = fd // heads
    n_chunks = tm // C

    @pl.when(pl.program_id(0) == 0)
    def _zero_state():
        st_ref[...] = jnp.zeros_like(st_ref)

    x = x_ref[...]
    h = (x * _rms_inv(x) * g_ref[...]).astype(BF16)
    lg = lbl_ref[...]
    ex = jnp.exp(lg - jnp.max(lg, axis=0, keepdims=True))
    sm = ex / jnp.sum(ex, axis=0, keepdims=True)
    lb = jnp.sum(sm[:layer + 1], axis=0, keepdims=True)
    log_lb = jnp.log(lb) * LOG2E
    log_1m_lb = jnp.log1p(-lb) * LOG2E

    def softplus_neg_abs(d):
        return jnp.log2(1.0 + jnp.exp2(_neg_abs(d)))

    def silu(z):
        return z / (1.0 + jnp.exp(-z))

    row = lax.broadcasted_iota(jnp.int32, (C, C), 0)
    col = lax.broadcasted_iota(jnp.int32, (C, C), 1)
    tri = jnp.where(row >= col, 1.0, 0.0).astype(BF16)
    tri3 = jnp.concatenate([tri, tri, tri], axis=1)
    levels = [1 << i for i in range(int(math.log2(C)))]
    xor = row ^ col
    below = row > col
    level_mask = [below & ((xor >> int(math.log2(w))) == 1) for w in levels]
    diag = row == col

    G = HG_PROJ_CHUNKS

    def project(i, c0):
        rows = slice(i * C, (i + G) * C)
        hc = h[rows]
        cs = slice(c0, c0 + MXU_COLS)
        wcol = lambda part: w_ref[:, part * fd + c0:part * fd + c0 + MXU_COLS]
        chunk_rows = [(i + n, slice(n * C, (n + 1) * C)) for n in range(G)]

        def stage(buf, val):
            for n, r in chunk_rows:
                buf[n, :, cs] = val[r]

        stage(q_s, silu(_dot(hc, wcol(0))).astype(BF16))
        zf = _dot(hc, wcol(1)) * LOG2E
        stage(v_s, _dot(hc, wcol(2)).astype(BF16))
        c = log_1m_lb[:, cs] + jnp.minimum(zf, 0.0) - softplus_neg_abs(zf)
        a = log_lb[:, cs]
        lf = jnp.maximum(a, c) + softplus_neg_abs(a - c)
        stage(k_s, jnp.exp2(c - zf).astype(BF16))
        gs_ref[rows, cs] = silu(_dot(hc, wcol(3))).astype(gs_ref.dtype)
        p0 = lf.astype(BF16)
        r1 = lf - p0.astype(F32)
        p1 = r1.astype(BF16)
        p2 = (r1 - p1.astype(F32)).astype(BF16)
        for n, r in chunk_rows:
            b_s[n, :, cs] = _dot(tri3, jnp.concatenate([p0[r], p1[r], p2[r]], axis=0))

    decay = lambda expo: jnp.exp2(expo).astype(BF16)

    def recur(i, hh):
        rows = slice(i * C, (i + 1) * C)
        sk = slice(hh * dk, (hh + 1) * dk)
        sv = slice(hh * dv, (hh + 1) * dv)
        bh = b_s[i, :, sk]
        qh = q_s[i, :, sk]
        kh = k_s[i, :, sk]
        vh = v_s[i, :, sv]
        b_last = bh[C - 1:C, :]
        a = jnp.where(diag, _dot_nt(qh, kh), 0.0)
        for w, mk in zip(levels, level_mask):
            e = decay(_neg_abs(bh - _level_ref(bh, w)))
            a = jnp.where(mk, _dot_nt(qh * e, kh * e), a)
        o = _dot(a.astype(BF16), vh)
        st = st_ref[hh]
        o = o + _dot_nt(qh * decay(bh), st.astype(BF16))
        st_ref[hh] = st * jnp.exp2(b_last) + _dot_tn(vh, kh * decay(b_last - bh))
        o_ref[rows, sv] = o.astype(o_ref.dtype)

    col_starts = list(range(0, fd, MXU_COLS))
    per_cols = G * heads // len(col_starts)
    for c0 in col_starts:
        project(0, c0)
    for i in range(0, n_chunks, G):
        units = [(i + n, hh) for n in range(G) for hh in range(heads)]
        for n, c0 in enumerate(col_starts):
            if i + G < n_chunks:
                project(i + G, c0)
            for ci, hh in units[n * per_cols:(n + 1) * per_cols]:
                recur(ci, hh)


def _hg_mixer(x, g, w, lb_logits, layer, heads, tm, chunk, weights=()):
    T, D = x.shape
    fd = lb_logits.shape[1]
    assert w.shape[1] == 4 * fd
    row = lambda n: pl.BlockSpec((tm, n), lambda i: (i, 0))
    stage = lambda dt: pltpu.VMEM((tm // chunk, chunk, fd), dt)
    cast_ops, cast_in, cast_out, cast_shapes = _cast_blocks(weights, T // tm, lambda i: i)
    out = pl.pallas_call(
        functools.partial(_hg_mixer_kernel, n_casts=len(cast_ops), layer=layer, heads=heads, chunk=chunk),
        out_shape=(jax.ShapeDtypeStruct((T, fd), BF16), jax.ShapeDtypeStruct((T, fd), BF16), *cast_shapes),
        grid=(T // tm,),
        in_specs=[row(D), _resident((1, D)), _resident(w.shape), _resident(lb_logits.shape), *cast_in],
        out_specs=(row(fd), row(fd), *cast_out),
        scratch_shapes=[pltpu.VMEM((heads, fd // heads, fd // heads), F32),
                        stage(BF16), stage(BF16), stage(BF16), stage(F32)],
        compiler_params=_params(("arbitrary",)),
        name="hgrn2_mixer",
    )(x, g, w, lb_logits, *cast_ops)
    return out[0], out[1], list(out[2:])


def _ffn_kernel(x_ref, a_ref, *refs, gated, final_norm):
    if gated:
        gs_ref, gn_ref, *refs = refs
    wpre_ref, g_ref, wup_ref, cw_ref, cb_ref, wdn_ref, fg_ref, y_ref, u_ref, act_ref = refs
    tm = x_ref.shape[0]
    ff = wdn_ref.shape[0]
    halo = SUBLANES

    @pl.when(pl.program_id(0) == 0)
    def _zero_halo():
        u_ref[0:halo, :] = jnp.zeros((halo, u_ref.shape[1]), F32)

    halves = [slice(r0, r0 + tm // 2) for r0 in range(0, tm, tm // 2)]
    xs, hs = [], []
    for rows in halves:
        a = a_ref[rows, :]
        if gated:
            o = a.astype(F32)
            a = (o * _rms_inv(o) * gn_ref[...] * gs_ref[rows, :].astype(F32)).astype(BF16)
        x = x_ref[rows, :] + _dot(a, wpre_ref[...])
        xs.append(x)
        hs.append((x * _rms_inv(x) * g_ref[...]).astype(BF16))
    h = jnp.concatenate(hs, axis=0)
    up_cols = 512
    for c0 in range(0, 2 * ff, up_cols):
        u_ref[halo:halo + tm, c0:c0 + up_cols] = _dot(h, wup_ref[:, c0:c0 + up_cols])

    def conv(cs):
        acc = cb_ref[:, cs] + cw_ref[CONV_W - 1:CONV_W, cs] * u_ref[halo:halo + tm, cs]
        for j in range(CONV_W - 1):
            off = halo - (CONV_W - 1) + j
            acc = acc + cw_ref[j:j + 1, cs] * u_ref[off:off + tm, cs]
        return acc

    act_cols = 256
    for c0 in range(0, ff, act_cols):
        gate = conv(slice(c0, c0 + act_cols))
        val = conv(slice(ff + c0, ff + c0 + act_cols))
        act_ref[:, c0:c0 + act_cols] = (gate * jax.nn.sigmoid(gate) * val).astype(BF16)

    u_ref[0:halo, :] = u_ref[tm:tm + halo, :]
    for rows, x in zip(halves, xs):
        y = x + _dot(act_ref[rows, :], wdn_ref[...])
        if final_norm:
            y = y * _rms_inv(y) * fg_ref[...]
        y_ref[rows, :] = y


def _ffn(x, a, gate, wpre, g, wup, cw, cb, wdn, fg, final_norm, tm):
    T, D = x.shape
    ff = wdn.shape[0]
    row = lambda n: pl.BlockSpec((tm, n), lambda i: (i, 0))
    resident = [wpre, g, wup, cw, cb, wdn, fg]
    operands = [x, a] + ([gate[0], gate[1]] if gate else []) + resident
    in_specs = ([row(D), row(a.shape[1])] + ([row(gate[0].shape[1]), _resident(gate[1].shape)] if gate else [])
                + [_resident(r.shape) for r in resident])
    return pl.pallas_call(
        functools.partial(_ffn_kernel, gated=gate is not None, final_norm=final_norm),
        out_shape=jax.ShapeDtypeStruct((T, D), F32),
        grid=(T // tm,),
        in_specs=in_specs,
        out_specs=row(D),
        scratch_shapes=[pltpu.VMEM((tm + SUBLANES, 2 * ff), F32), pltpu.VMEM((tm, ff), BF16)],
        compiler_params=_params(("arbitrary",)),
        name="conv_ffn",
    )(*operands)


def _qkv_kernel(x_ref, gkv_ref, gq_ref, wk_ref, wv_ref, wq_ref, *refs, n_casts, scale):
    cast_in, (qT_ref, k_ref, vT_ref, *cast_out) = refs[:n_casts], refs[n_casts:]
    for src, dst in zip(cast_in, cast_out):
        dst[...] = src[...].astype(dst.dtype)
    heads = k_ref.shape[0]
    hd = k_ref.shape[2]
    x = x_ref[...]
    xn = x * _rms_inv(x)
    hkv = (xn * gkv_ref[...]).astype(BF16)
    hq = (xn * gq_ref[...]).astype(BF16)
    k = _dot(hkv, wk_ref[...]).astype(BF16)
    for h in range(heads):
        k_ref[h] = k[:, h * hd:(h + 1) * hd]
    vT_ref[...] = _dot(hkv, wv_ref[...]).T.astype(BF16)
    qT_ref[...] = (_dot(hq, wq_ref[...]) * scale).T.astype(BF16)


def _qkv(x, gkv, gq, wk, wv, wq, heads, scale, tm, weights=()):
    T, D = x.shape
    hd = wk.shape[1] // heads
    colblk = lambda n: pl.BlockSpec((n, tm), lambda i: (0, i))
    cast_ops, cast_in, cast_out, cast_shapes = _cast_blocks(weights, T // tm, lambda i: i)
    out = pl.pallas_call(
        functools.partial(_qkv_kernel, n_casts=len(cast_ops), scale=scale),
        out_shape=(jax.ShapeDtypeStruct((wq.shape[1], T), BF16), jax.ShapeDtypeStruct((heads, T, hd), BF16),
                   jax.ShapeDtypeStruct((wv.shape[1], T), BF16), *cast_shapes),
        grid=(T // tm,),
        in_specs=[pl.BlockSpec((tm, D), lambda i: (i, 0)), _resident(gkv.shape), _resident(gq.shape),
                  _resident(wk.shape), _resident(wv.shape), _resident(wq.shape), *cast_in],
        out_specs=(colblk(wq.shape[1]), pl.BlockSpec((heads, tm, hd), lambda i: (0, i, 0)), colblk(wv.shape[1]),
                   *cast_out),
        compiler_params=_params(("arbitrary",)),
        name="qkv_proj",
    )(x, gkv, gq, wk, wv, wq, *cast_ops)
    return out[0], out[1], out[2], list(out[3:])


def _attn_kernel(tbl_ref, lam_ref, g_ref, qT_ref, k_ref, vT_ref, o_ref,
                 qp_ref, dbias_ref, pbias_ref, m_ref, l_ref, acc_ref, *, lambda_init):
    hw, B = qT_ref.shape
    hd = hw // 2
    h = pl.program_id(0)
    qi = pl.program_id(1)
    nsub = B // LANES

    @pl.when(qi == 0)
    def _build_bias_tiles():
        far = tbl_ref[h * REL_BUCKETS + REL_BUCKETS - 1]
        a = lax.broadcasted_iota(jnp.int32, (LANES, LANES), 0)
        b = lax.broadcasted_iota(jnp.int32, (LANES, LANES), 1)

        def rel_bias(n):
            val = jnp.zeros(n.shape, F32)
            for bucket in range(REL_BUCKETS - 2, -1, -1):
                val = jnp.where(n < T5_BUCKET_LO[bucket + 1], (tbl_ref[h * REL_BUCKETS + bucket] - far) * LOG2E, val)
            return val

        e0 = jnp.where(b >= a, rel_bias(b - a), NEG)
        e1 = rel_bias(LANES + b - a)
        zero = jnp.zeros((LANES, LANES), F32)
        masked = jnp.full((LANES, LANES), NEG, F32)
        for r in range(nsub):
            for c in range(nsub):
                rs, cs = slice(r * LANES, (r + 1) * LANES), slice(c * LANES, (c + 1) * LANES)
                dbias_ref[rs, cs] = masked if c < r else e0 if c == r else e1 if c == r + 1 else zero
        for c in range(nsub):
            pbias_ref[:, c * LANES:(c + 1) * LANES] = e1 if c == 0 else zero

    qT = qT_ref[...].astype(F32)
    half = lax.broadcasted_iota(jnp.int32, qT.shape, 0) < hd
    qp_ref[0] = jnp.where(half, qT, 0.0).astype(BF16)
    qp_ref[1] = jnp.where(half, 0.0, qT).astype(BF16)
    l_ref[...] = jnp.zeros(l_ref.shape, F32)
    acc_ref[...] = jnp.zeros(acc_ref.shape, F32)

    def scores(j, kind, c):
        st = pl.multiple_of(j * B, B)
        if kind == "prev":
            cut = B - LANES
            last = pl.multiple_of(st + cut, LANES)
            return jnp.concatenate([_dot(k_ref[pl.ds(st, cut), :], qp_ref[c]),
                                    _dot(k_ref[pl.ds(last, LANES), :], qp_ref[c]) + pbias_ref[...]], axis=0)
        s = _dot(k_ref[pl.ds(st, B), :], qp_ref[c])
        return s + dbias_ref[...] if kind == "diag" else s

    def update(j, sc):
        st = pl.multiple_of(j * B, B)
        vt = vT_ref[:, pl.ds(st, B)]
        for c in range(2):
            m_old = m_ref[c]
            m_new = jnp.maximum(m_old, jnp.max(sc[c], axis=0, keepdims=True))
            p = jnp.exp2(sc[c] - m_new)
            alpha = jnp.exp2(m_old - m_new)
            l_ref[c] = alpha * l_ref[c] + jnp.sum(p, axis=0, keepdims=True)
            acc_ref[c] = alpha * acc_ref[c] + _dot(vt, p.astype(BF16))
            m_ref[c] = m_new

    def process(tiles):
        qk = lambda t: [scores(tiles[t][0], tiles[t][1], c) for c in range(2)]
        nxt = qk(0)
        for t in range(len(tiles)):
            cur, nxt = nxt, (qk(t + 1) if t + 1 < len(tiles) else None)
            update(tiles[t][0], cur)

    def process_fast(tiles):
        qk = lambda t: [scores(tiles[t][0], tiles[t][1], c) for c in range(2)]
        psum = [None, None]
        ps = [[], []]
        vts = []
        nxt = qk(0)
        for t in range(len(tiles)):
            cur, nxt = nxt, (qk(t + 1) if t + 1 < len(tiles) else None)
            vts.append(vT_ref[:, pl.ds(pl.multiple_of(tiles[t][0] * B, B), B)])
            for c in range(2):
                p = jnp.exp2(cur[c])
                part = jnp.sum(p, axis=0, keepdims=True)
                psum[c] = part if psum[c] is None else psum[c] + part
                ps[c].append(p.astype(BF16))
        vt_all = jnp.concatenate(vts, axis=1)
        for c in range(2):
            acc_ref[c] = acc_ref[c] + _dot(vt_all, jnp.concatenate(ps[c], axis=0))
            l_ref[c] = l_ref[c] + psum[c]

    n_far = jnp.maximum(qi - 1, 0)
    far = lambda first, n: [(first + t, "far") for t in range(n)]
    edge = [(qi - 1, "prev"), (qi, "diag")]

    def far_group(g, carry):
        process_fast(far(FAR_GROUP * g, FAR_GROUP))
        return carry

    n_groups = n_far // FAR_GROUP
    lax.fori_loop(0, n_groups, far_group, 0)
    done = n_groups * FAR_GROUP
    rem = n_far - done
    size = FAR_GROUP // 2
    while size >= 1:
        first = done + (rem & ~(2 * size - 1))
        has, is_last = (rem & size) != 0, (rem & (size - 1)) == 0

        @pl.when(jnp.logical_and(has, jnp.logical_not(is_last)))
        def _leftover(first=first, size=size):
            process_fast(far(first, size))

        @pl.when(jnp.logical_and(has, is_last))
        def _leftover_and_edge(first=first, size=size):
            process_fast(far(first, size) + edge)
        size //= 2

    @pl.when(jnp.logical_and(rem == 0, qi >= 1))
    def _edge_only():
        process_fast(edge)

    @pl.when(qi == 0)
    def _diagonal_only():
        process_fast([(qi, "diag")])

    def causal_edge_tiles():
        @pl.when(qi >= 1)
        def _previous_and_diagonal():
            process(edge)

        @pl.when(qi == 0)
        def _diagonal_only_slow():
            process([(qi, "diag")])

    def write_output():
        lam = (jnp.exp(jnp.sum(lam_ref[0:1, :] * lam_ref[1:2, :], axis=-1, keepdims=True))
               - jnp.exp(jnp.sum(lam_ref[2:3, :] * lam_ref[3:4, :], axis=-1, keepdims=True)) + lambda_init)
        oT = acc_ref[0] * (1.0 / l_ref[0]) - acc_ref[1] * (lam / l_ref[1])
        inv = lax.rsqrt(jnp.mean(oT * oT, axis=0, keepdims=True) + EPS)
        y = oT * inv * g_ref[...] * (1.0 - lambda_init)
        o_ref[...] = y.T.astype(o_ref.dtype)

    write_output()
    not_finite = lambda x: jnp.where(jnp.isfinite(x), 0.0, 1.0)
    l_low = jnp.minimum(l_ref[0], l_ref[1])
    flags = (jnp.sum(not_finite(acc_ref[0]) + not_finite(acc_ref[1]), axis=0, keepdims=True)
             + not_finite(l_ref[0]) + not_finite(l_ref[1]) + jnp.where(l_low >= FIXED_REF_MIN_SUM, 0.0, 1.0))

    @pl.when(jnp.sum(flags) > 0.0)
    def _redo_with_running_max():
        m_ref[...] = jnp.full(m_ref.shape, NEG, F32)
        l_ref[...] = jnp.zeros(l_ref.shape, F32)
        acc_ref[...] = jnp.zeros(acc_ref.shape, F32)
        causal_edge_tiles()

        def one_far_tile(j, carry):
            process([(j, "far")])
            return carry

        lax.fori_loop(0, n_far, one_far_tile, 0)
        write_output()


def _attention(tbl, lam, g, qT, k, vT, lambda_init, block):
    heads, T, hw = k.shape
    dv = vT.shape[0] // heads
    B = block
    smem = pl.BlockSpec(memory_space=pltpu.SMEM)
    return pl.pallas_call(
        functools.partial(_attn_kernel, lambda_init=lambda_init),
        out_shape=jax.ShapeDtypeStruct((T, heads * dv), BF16),
        grid=(heads, T // B),
        in_specs=[smem,
                  pl.BlockSpec(lam.shape, lambda h, i: (0, 0)),
                  pl.BlockSpec(g.shape, lambda h, i: (0, 0)),
                  pl.BlockSpec((hw, B), lambda h, i: (h, i)),
                  pl.BlockSpec((None, T, hw), lambda h, i: (h, 0, 0)),
                  pl.BlockSpec((dv, T), lambda h, i: (h, 0))],
        out_specs=pl.BlockSpec((B, dv), lambda h, i: (i, h)),
        scratch_shapes=[pltpu.VMEM((2, hw, B), BF16), pltpu.VMEM((B, B), F32), pltpu.VMEM((LANES, B), F32),
                        pltpu.VMEM((2, 1, B), F32), pltpu.VMEM((2, 1, B), F32), pltpu.VMEM((2, dv, B), F32)],
        compiler_params=_params(("arbitrary", "arbitrary")),
        name="diff_attention",
    )(tbl, lam, g, qT, k, vT)


def kernel(x, a_w_in, a_w_out, a_gnorm, a_lb_logits, b_w_q, b_w_o, b_lam_q1, b_lam_k1, b_lam_q2, b_lam_k2,
           b_subln, kv_norm, kv_w, rel_table, norm_mix, norm_ffn, ffn_w_up, ffn_conv_w, ffn_conv_b,
           ffn_w_down, final_norm):
    batch, T, D = x.shape
    depth = norm_mix.shape[0]
    n_a = a_w_in.shape[0]
    qdim = b_w_q.shape[2]
    row = lambda v: v.reshape(1, -1).astype(F32)
    tm = min(512, T)
    chunk = min(HG_CHUNK, T)
    block = min(ATT_BLOCK, T)
    tbl = rel_table.astype(F32).T.reshape(-1)

    outs = []
    for bi in range(batch):
        xs = x[bi]
        qT = k3 = vT = None
        for li in range(depth):
            if li == n_a:
                j = 0
                qT, k3, vT, cast = _qkv(xs, row(kv_norm), row(norm_mix[li]), kv_w[:, :qdim].astype(BF16),
                                        kv_w[:, qdim:].astype(BF16), b_w_q[j].astype(BF16), DA_HEADS,
                                        DA_HEAD_DIM ** -0.5 * LOG2E, tm,
                                        [(b_w_o, j), (ffn_w_up, li), (ffn_w_down, li)])
            if li < n_a:
                mix, gs, (w_mix, w_up, w_dn) = _hg_mixer(
                    xs, row(norm_mix[li]), a_w_in[li].astype(BF16), a_lb_logits.astype(F32), li, HG_HEADS, tm, chunk,
                    [(a_w_out, li), (ffn_w_up, li), (ffn_w_down, li)])
                gate = (gs, row(a_gnorm[li]))
            else:
                j = li - n_a
                if j > 0:
                    qT, _, _, cast = _qkv(xs, row(kv_norm), row(norm_mix[li]), kv_w[:, :qdim].astype(BF16),
                                          kv_w[:, qdim:].astype(BF16), b_w_q[j].astype(BF16), DA_HEADS,
                                          DA_HEAD_DIM ** -0.5 * LOG2E, tm,
                                          [(b_w_o, j), (ffn_w_up, li), (ffn_w_down, li)])
                lambda_init = 0.8 - 0.6 * math.exp(-0.3 * li)
                lam = jnp.stack([b_lam_q1[j], b_lam_k1[j], b_lam_q2[j], b_lam_k2[j]]).astype(F32)
                mix = _attention(tbl, lam, b_subln[j].reshape(-1, 1).astype(F32), qT, k3, vT, lambda_init, block)
                gate, (w_mix, w_up, w_dn) = None, cast
            last = li == depth - 1
            xs = _ffn(xs, mix, gate, w_mix, row(norm_ffn[li]), w_up, ffn_conv_w[li].astype(F32),
                      row(ffn_conv_b[li]), w_dn, row(final_norm), last, tm)
        outs.append(xs)
    return jnp.stack(outs).astype(x.dtype)
```
